```python
import numpy as np
import jax
import jax.numpy as jnp
from jax import lax

D_MODEL = 1024
BATCH = 8
SEQ = 2048
DEPTH = 1

GRID_W = 64
CTX_LEN = 256
ADALN_CHUNKS = 6
NORM_EPS = 1e-6

GLA_HEADS = 4
GLA_DV = D_MODEL // GLA_HEADS
GLA_DK = GLA_DV // 2
GLA_KEY_W = GLA_HEADS * GLA_DK
GLA_VAL_W = GLA_HEADS * GLA_DV
GLA_LORA = 16
GLA_GATE_NORM = 16.0
GLA_CHUNK = 64

RWKV_N = 64
RWKV_HEADS = D_MODEL // RWKV_N
RWKV_W = RWKV_HEADS * RWKV_N
RWKV_DECAY_LORA = 64
RWKV_AAA_LORA = 64
RWKV_GATE_LORA = 160
RWKV_LN_EPS = 64e-5

GLA_SPLITS = (GLA_KEY_W, GLA_KEY_W, GLA_VAL_W, GLA_VAL_W, GLA_LORA, GLA_LORA)
RWKV_SPLITS = (RWKV_W, RWKV_W, RWKV_W, RWKV_DECAY_LORA, RWKV_DECAY_LORA, RWKV_AAA_LORA, RWKV_GATE_LORA)
GLA_COLS = sum(GLA_SPLITS)
RWKV_COLS = sum(RWKV_SPLITS)
BRANCH_GATE_COLS = 2 * D_MODEL
IN_COLS = GLA_COLS + RWKV_COLS + BRANCH_GATE_COLS

PEER_HEADS = 8
PEER_N_KEYS = 128
PEER_EXPERTS = PEER_N_KEYS * PEER_N_KEYS
PEER_HALF = 128
PEER_TOPK = 16
PEER_BLOCK = 128

kernel_name = 'hybrid_gla_rwkv7_peer_prefix_dit'


def _split(z, sizes):
    return jnp.split(z, np.cumsum(sizes)[:-1].tolist(), axis=-1)


def _rmsnorm(x, w, eps=NORM_EPS):
    xf = x.astype(jnp.float32)
    y = xf * lax.rsqrt(jnp.mean(xf * xf, axis=-1, keepdims=True) + eps)
    return (y * w.astype(jnp.float32)).astype(x.dtype)


def _modulate(h, shift, scale):
    return h * (1 + scale) + shift


def _grid_shift(z, rows):
    b, l, ch = z.shape
    zg = z.reshape(b, rows, GRID_W, ch // 4, 4)
    left = jnp.pad(zg[:, :, :-1, :, 0], ((0, 0), (0, 0), (1, 0), (0, 0)))
    right = jnp.pad(zg[:, :, 1:, :, 1], ((0, 0), (0, 0), (0, 1), (0, 0)))
    up = jnp.pad(zg[:, :-1, :, :, 2], ((0, 0), (1, 0), (0, 0), (0, 0)))
    down = jnp.pad(zg[:, 1:, :, :, 3], ((0, 0), (0, 1), (0, 0), (0, 0)))
    return jnp.stack([left, right, up, down], axis=-1).reshape(b, l, ch)


def _seq_shift(z):
    b, l, ch = z.shape
    zs = z.reshape(b, l, ch // 2, 2)
    prev = jnp.pad(zs[:, :-1, :, 0], ((0, 0), (1, 0), (0, 0)))
    nxt = jnp.pad(zs[:, 1:, :, 1], ((0, 0), (0, 1), (0, 0)))
    return jnp.stack([prev, nxt], axis=-1).reshape(b, l, ch)


def _gla_scan(q, k, v, log_a, s0):
    b, l, h, _ = q.shape
    n = l // GLA_CHUNK

    def chunks(t):
        return t.reshape(b, n, GLA_CHUNK, h, t.shape[-1]).transpose(1, 0, 3, 2, 4)

    mask = jnp.tril(jnp.ones((GLA_CHUNK, GLA_CHUNK), dtype=bool))

    def step(s, xs):
        qc, kc, vc, gc = xs
        bcum = jnp.cumsum(gc, axis=-2)
        blast = bcum[..., -1:, :]
        q_dec = qc * jnp.exp(bcum)
        k_dec = kc * jnp.exp(-bcum)
        att = jnp.where(mask, jnp.einsum('bhid,bhjd->bhij', q_dec, k_dec), 0.0)
        o = jnp.einsum('bhid,bhde->bhie', q_dec, s) + jnp.einsum('bhij,bhje->bhie', att, vc)
        s_new = (jnp.exp(blast[..., 0, :])[..., None] * s
                 + jnp.einsum('bhjd,bhje->bhde', kc * jnp.exp(blast - bcum), vc))
        return s_new, o

    s_fin, o = lax.scan(step, s0, (chunks(q), chunks(k), chunks(v), chunks(log_a)))
    o = o.transpose(1, 0, 3, 2, 4).reshape(b, l, h, v.shape[-1])
    return o, s_fin


def _rwkv7_scan(r, decay, k, v, kk, a, s0, reverse):
    xs = tuple(jnp.moveaxis(t, 1, 0) for t in (r, decay, k, v, kk, a))

    def step(s, xt):
        rt, dt, kt, vt, kkt, at = xt
        sa = jnp.einsum('bhvk,bhk->bhv', s, kkt)
        s = (s * dt[:, :, None, :] - sa[..., None] * (kkt * at)[:, :, None, :]
             + vt[..., None] * kt[:, :, None, :])
        return s, jnp.einsum('bhvk,bhk->bhv', s, rt)

    s_fin, o = lax.scan(step, s0, xs, reverse=reverse)
    return jnp.moveaxis(o, 0, 1), s_fin


def _token_mixer(hm, rows, init_states, need_out, lp):
    b, l, _ = hm.shape
    f32 = jnp.float32

    def heads(t, nh):
        return t.reshape(b, l, nh, -1).astype(f32)

    def flip(t):
        return jnp.flip(t, axis=1)

    z = hm @ lp['w_in']
    z_gla, z_rwkv, z_gate = _split(z, (GLA_COLS, RWKV_COLS, BRANCH_GATE_COLS))
    sg_f0, sg_b0, sr_f0, sr_b0 = init_states

    q, k, v, g_out, alo_f, alo_b = _split(z_gla, GLA_SPLITS)

    def gla_log_decay(alo, d):
        return jax.nn.log_sigmoid(alo @ lp['gla_w_a2'][d] + lp['gla_b_a'][d]) / GLA_GATE_NORM

    qh = heads(q, GLA_HEADS) * GLA_DK ** -0.5
    kh = heads(k, GLA_HEADS)
    vh = heads(v, GLA_HEADS)
    lf = heads(gla_log_decay(alo_f, 0), GLA_HEADS)
    lb = heads(gla_log_decay(alo_b, 1), GLA_HEADS)
    og_f, sg_f = _gla_scan(qh, kh, vh, lf, sg_f0)
    og_b, sg_b = _gla_scan(flip(qh), flip(kh), flip(vh), flip(lb), sg_b0)

    zs = _seq_shift(z_rwkv) if rows is None else _grid_shift(z_rwkv, rows)
    zm = z_rwkv + (zs - z_rwkv) * lp['rwkv_mu']
    r, kr, vr, wlo_f, wlo_b, alo, glo = _split(zm, RWKV_SPLITS)

    def rwkv_decay(wlo, d):
        w = -jax.nn.softplus(-(lp['rwkv_w0'][d] + jnp.tanh(wlo) @ lp['rwkv_w2'][d])) - 0.5
        return jnp.exp(-jnp.exp(heads(w, RWKV_HEADS)))

    a = jax.nn.sigmoid(lp['rwkv_a0'] + alo @ lp['rwkv_a2'])
    kk = heads(kr * lp['rwkv_k_k'], RWKV_HEADS)
    kk = kk / jnp.maximum(jnp.linalg.norm(kk, axis=-1, keepdims=True), 1e-12)
    kr = kr * (1 + (a - 1) * lp['rwkv_k_a'])
    rh = heads(r, RWKV_HEADS)
    krh = heads(kr, RWKV_HEADS)
    vrh = heads(vr, RWKV_HEADS)
    ah = heads(a, RWKV_HEADS)
    or_f, sr_f = _rwkv7_scan(rh, rwkv_decay(wlo_f, 0), krh, vrh, kk, ah, sr_f0, False)
    or_b, sr_b = _rwkv7_scan(rh, rwkv_decay(wlo_b, 1), krh, vrh, kk, ah, sr_b0, True)
    states = (sg_f, sg_b, sr_f, sr_b)
    if not need_out:
        return None, states

    og = og_f + flip(og_b)
    og = og * lax.rsqrt(jnp.mean(og * og, axis=-1, keepdims=True) + NORM_EPS)
    og = og * lp['gla_norm_w'].reshape(GLA_HEADS, GLA_DV).astype(f32)
    y_gla = og.reshape(b, l, GLA_VAL_W).astype(hm.dtype) * jax.nn.silu(g_out)

    o_r = or_f + or_b
    mu = jnp.mean(o_r, axis=-1, keepdims=True)
    var = jnp.mean(jnp.square(o_r - mu), axis=-1, keepdims=True)
    o_r = ((o_r - mu) * lax.rsqrt(var + RWKV_LN_EPS) * lp['rwkv_ln_w'].reshape(RWKV_HEADS, RWKV_N)
           + lp['rwkv_ln_b'].reshape(RWKV_HEADS, RWKV_N))
    o_r = o_r + jnp.sum(rh * krh * lp['rwkv_r_k'], axis=-1, keepdims=True) * vrh
    y_rwkv = o_r.reshape(b, l, RWKV_W).astype(hm.dtype) * (jax.nn.sigmoid(glo) @ lp['rwkv_g2'])

    gate_gla, gate_rwkv = jnp.split(z_gate, 2, axis=-1)
    y = jax.nn.sigmoid(gate_gla) * y_gla + jax.nn.sigmoid(gate_rwkv) * y_rwkv
    return y @ lp['w_out'], states


def _peer(h, w_q, sub_keys, exp_u, exp_v):
    b, l, d = h.shape
    q = (h @ w_q).reshape(b, l, PEER_HEADS, 2, PEER_HALF)
    s = jnp.einsum('blhpd,hpnd->blhpn', q, sub_keys).astype(jnp.float32)
    s1, i1 = lax.top_k(s[..., 0, :], PEER_TOPK)
    s2, i2 = lax.top_k(s[..., 1, :], PEER_TOPK)
    cand_s = (s1[..., :, None] + s2[..., None, :]).reshape(b, l, PEER_HEADS, PEER_TOPK * PEER_TOPK)
    cand_i = (i1[..., :, None] * PEER_N_KEYS + i2[..., None, :]).reshape(b, l, PEER_HEADS, PEER_TOPK * PEER_TOPK)
    top_s, pos = lax.top_k(cand_s, PEER_TOPK)
    idx = jnp.take_along_axis(cand_i, pos, axis=-1)
    gates = jax.nn.softmax(top_s, axis=-1).astype(h.dtype)
    n_blocks = (b * l) // PEER_BLOCK
    slots = PEER_HEADS * PEER_TOPK
    xs = (h.reshape(n_blocks, PEER_BLOCK, d),
          idx.reshape(n_blocks, PEER_BLOCK, slots),
          gates.reshape(n_blocks, PEER_BLOCK, slots))

    def block(args):
        xt, it, gt = args
        u = jnp.take(exp_u, it, axis=0)
        act = jax.nn.gelu(jnp.einsum('tsd,td->ts', u, xt), approximate=False)
        vv = jnp.take(exp_v, it, axis=0)
        return jnp.einsum('ts,tsd->td', gt * act, vv)

    return lax.map(block, xs).reshape(b, l, d)


def setup_inputs(seed: int = 0) -> dict:
    key = jax.random.key(seed)
    ks = iter(jax.random.split(key, 29))
    D = D_MODEL
    L = DEPTH

    def nrm(shape, scale):
        return scale * jax.random.normal(next(ks), shape, jnp.float32)

    def uni(shape, lo, hi):
        return jax.random.uniform(next(ks), shape, jnp.float32, lo, hi)

    return {
        'x': nrm((BATCH, SEQ, D), 1.0),
        'c': nrm((BATCH, D), 1.0),
        'ctx': nrm((BATCH, CTX_LEN, D), 1.0),
        'c_ctx': nrm((D,), 1.0),
        'norm1_w': 1.0 + nrm((L, D), 0.02),
        'w_mod': nrm((L, D, ADALN_CHUNKS * D), 0.5 * D ** -0.5),
        'b_mod': nrm((L, ADALN_CHUNKS * D), 0.02),
        'w_in': nrm((L, D, IN_COLS), D ** -0.5),
        'gla_w_a2': nrm((L, 2, GLA_LORA, GLA_KEY_W), GLA_LORA ** -0.5),
        'gla_b_a': nrm((L, 2, GLA_KEY_W), 0.1),
        'gla_norm_w': 1.0 + nrm((L, GLA_VAL_W), 0.02),
        'rwkv_mu': uni((L, RWKV_COLS), 0.2, 0.8),
        'rwkv_w0': uni((L, 2, RWKV_W), -6.0, -1.0),
        'rwkv_w2': nrm((L, 2, RWKV_DECAY_LORA, RWKV_W), 0.5 * RWKV_DECAY_LORA ** -0.5),
        'rwkv_a0': nrm((L, RWKV_W), 0.1),
        'rwkv_a2': nrm((L, RWKV_AAA_LORA, RWKV_W), 0.5 * RWKV_AAA_LORA ** -0.5),
        'rwkv_g2': nrm((L, RWKV_GATE_LORA, RWKV_W), RWKV_GATE_LORA ** -0.5),
        'rwkv_k_k': 0.85 + nrm((L, RWKV_W), 0.02),
        'rwkv_k_a': 1.0 + nrm((L, RWKV_W), 0.02),
        'rwkv_r_k': nrm((L, RWKV_HEADS, RWKV_N), 0.1),
        'rwkv_ln_w': 1.0 + nrm((L, RWKV_W), 0.02),
        'rwkv_ln_b': nrm((L, RWKV_W), 0.02),
        'w_out': nrm((L, D, D), D ** -0.5),
        'norm2_w': 1.0 + nrm((L, D), 0.02),
        'peer_w_q': nrm((L, D, PEER_HEADS * 2 * PEER_HALF), D ** -0.5),
        'peer_sub_keys': nrm((L, PEER_HEADS, 2, PEER_N_KEYS, PEER_HALF), PEER_HALF ** -0.5),
        'peer_u': nrm((L, PEER_EXPERTS, D), D ** -0.5),
        'peer_v': nrm((L, PEER_EXPERTS, D), 1.0),
        'final_norm_w': 1.0 + nrm((D,), 0.02),
    }


def reference(x, c, ctx, c_ctx, norm1_w, w_mod, b_mod, w_in, gla_w_a2, gla_b_a, gla_norm_w,
              rwkv_mu, rwkv_w0, rwkv_w2, rwkv_a0, rwkv_a2, rwkv_g2, rwkv_k_k, rwkv_k_a, rwkv_r_k,
              rwkv_ln_w, rwkv_ln_b, w_out, norm2_w, peer_w_q, peer_sub_keys, peer_u, peer_v,
              final_norm_w):
    b, seq, _ = x.shape
    rows = seq // GRID_W
    f32 = jnp.float32
    zero_states = (jnp.zeros((b, GLA_HEADS, GLA_DK, GLA_DV), f32),
                   jnp.zeros((b, GLA_HEADS, GLA_DK, GLA_DV), f32),
                   jnp.zeros((b, RWKV_HEADS, RWKV_N, RWKV_N), f32),
                   jnp.zeros((b, RWKV_HEADS, RWKV_N, RWKV_N), f32))
    h_lat, h_ctx = x, ctx
    for layer in range(DEPTH):
        last = layer == DEPTH - 1
        lp = {
            'w_in': w_in[layer], 'gla_w_a2': gla_w_a2[layer], 'gla_b_a': gla_b_a[layer],
            'gla_norm_w': gla_norm_w[layer], 'rwkv_mu': rwkv_mu[layer], 'rwkv_w0': rwkv_w0[layer],
            'rwkv_w2': rwkv_w2[layer], 'rwkv_a0': rwkv_a0[layer], 'rwkv_a2': rwkv_a2[layer],
            'rwkv_g2': rwkv_g2[layer], 'rwkv_k_k': rwkv_k_k[layer], 'rwkv_k_a': rwkv_k_a[layer],
            'rwkv_r_k': rwkv_r_k[layer], 'rwkv_ln_w': rwkv_ln_w[layer], 'rwkv_ln_b': rwkv_ln_b[layer],
            'w_out': w_out[layer],
        }
        m_lat = jnp.split((jax.nn.silu(c) @ w_mod[layer] + b_mod[layer])[:, None, :], ADALN_CHUNKS, axis=-1)
        m_ctx = jnp.split(jax.nn.silu(c_ctx) @ w_mod[layer] + b_mod[layer], ADALN_CHUNKS, axis=-1)

        a_ctx = _modulate(_rmsnorm(h_ctx, norm1_w[layer]), m_ctx[0], m_ctx[1])
        y_ctx, ctx_states = _token_mixer(a_ctx, None, zero_states, not last, lp)

        a_lat = _modulate(_rmsnorm(h_lat, norm1_w[layer]), m_lat[0], m_lat[1])
        y_lat, _ = _token_mixer(a_lat, rows, ctx_states, True, lp)
        h_lat = h_lat + m_lat[2] * y_lat
        f_lat = _peer(_modulate(_rmsnorm(h_lat, norm2_w[layer]), m_lat[3], m_lat[4]),
                      peer_w_q[layer], peer_sub_keys[layer], peer_u[layer], peer_v[layer])
        h_lat = h_lat + m_lat[5] * f_lat

        if not last:
            h_ctx = h_ctx + m_ctx[2] * y_ctx
            f_ctx = _peer(_modulate(_rmsnorm(h_ctx, norm2_w[layer]), m_ctx[3], m_ctx[4]),
                          peer_w_q[layer], peer_sub_keys[layer], peer_u[layer], peer_v[layer])
            h_ctx = h_ctx + m_ctx[5] * f_ctx
    return _rmsnorm(h_lat, final_norm_w)
```

```python
import functools

import numpy as np
import jax
import jax.numpy as jnp
from jax import lax
from jax.experimental import pallas as pl
from jax.experimental.pallas import tpu as pltpu

F32 = jnp.float32
BF16 = jnp.bfloat16
HI = lax.Precision.HIGHEST

D = 1024
GRID_COLS = 64
N_CTX = 256
EPS = 1e-6
N_MOD = 6

G_HEADS = 4
G_DV = D // G_HEADS
G_DK = G_DV // 2
G_LORA = 16
G_GATE_NORM = 16.0
G_CHUNK = 64

R_N = 64
R_HEADS = D // R_N
R_LN_EPS = 64e-5
R_W_LORA = 64
R_A_LORA = 64
R_G_LORA = 160

P_HEADS = 8
P_KEYS = 128
P_TOPK = 16

C_RWKV = 0
C_GLA = 3072
C_GATE = 6144
C_LORA = 8192
N_PROJ = 8704
L_WF, L_WB, L_A, L_G = 32, 96, 160, 224

ROWS = 256
V7X_VMEM_BYTES = 64 * 1024 * 1024


def _cparams(sem, vmem_mb):
    return pltpu.CompilerParams(dimension_semantics=sem,
                                vmem_limit_bytes=min(vmem_mb * 1024 * 1024, V7X_VMEM_BYTES - (8 << 20)))


def _sigmoid(x):
    return 1.0 / (1.0 + jnp.exp(-x))


def _softplus(x):
    return jnp.maximum(x, 0.0) + jnp.log(1.0 + jnp.exp(-jnp.abs(x)))


def _dot(a, b, precision=None):
    return jnp.dot(a, b, preferred_element_type=F32, precision=precision)


def _dot_nt(a, b, precision=None):
    return lax.dot_general(a, b, (((1,), (1,)), ((), ())), preferred_element_type=F32, precision=precision)


def _dot_tn(a, b, precision=None):
    return lax.dot_general(a, b, (((0,), (0,)), ((), ())), preferred_element_type=F32, precision=precision)


def _mod_kernel(c_ref, w_ref, b_ref, o_ref):
    c = c_ref[...]
    s = c * _sigmoid(c)
    o_ref[...] = _dot(s.astype(BF16), w_ref[...].astype(BF16)) + b_ref[...]


def _mod(c16, w_mod, b_mod):
    n = w_mod.shape[1]
    tn = 768
    return pl.pallas_call(
        _mod_kernel,
        grid=(n // tn,),
        in_specs=[pl.BlockSpec((16, D), lambda j: (0, 0)),
                  pl.BlockSpec((D, tn), lambda j: (0, j)),
                  pl.BlockSpec((1, tn), lambda j: (0, j))],
        out_specs=pl.BlockSpec((16, tn), lambda j: (0, j)),
        out_shape=jax.ShapeDtypeStruct((16, n), F32),
        compiler_params=_cparams(("parallel",), 24),
        name="mod",
    )(c16, w_mod, b_mod)


def _inproj_kernel(h_ref, nw_ref, sc_ref, sh_ref, w_ref, o_ref, a_scr):
    @pl.when(pl.program_id(2) == 0)
    def _():
        x = h_ref[0]
        y = x * lax.rsqrt(jnp.mean(x * x, axis=-1, keepdims=True) + EPS) * nw_ref[...]
        a_scr[...] = (y * (1.0 + sc_ref[0, 0]) + sh_ref[0, 0]).astype(BF16)

    o_ref[0] = _dot(a_scr[...], w_ref[...])


def _inproj(h_all, norm_w, scale, shift, w_p):
    b, t, _ = h_all.shape
    tn = 512
    mod_idx = lambda i, j, n: (i, jnp.minimum(j, 1), 0, 0)
    return pl.pallas_call(
        _inproj_kernel,
        grid=(b, t // ROWS, N_PROJ // tn),
        in_specs=[pl.BlockSpec((1, ROWS, D), lambda i, j, n: (i, j, 0)),
                  pl.BlockSpec((1, D), lambda i, j, n: (0, 0)),
                  pl.BlockSpec((1, 1, 1, D), mod_idx),
                  pl.BlockSpec((1, 1, 1, D), mod_idx),
                  pl.BlockSpec((D, tn), lambda i, j, n: (0, n))],
        out_specs=pl.BlockSpec((1, ROWS, tn), lambda i, j, n: (i, j, n)),
        out_shape=jax.ShapeDtypeStruct((b, t, N_PROJ), F32),
        scratch_shapes=[pltpu.VMEM((ROWS, D), BF16)],
        compiler_params=_cparams(("parallel", "parallel", "arbitrary"), 24),
        name="inproj",
    )(h_all, norm_w, scale, shift, w_p)


def _shift_mix(zc, zp, zn, mu, is_ctx):
    rows, w = zc.shape
    row = lax.broadcasted_iota(jnp.int32, (rows, w), 0)
    lane = lax.broadcasted_iota(jnp.int32, (rows, w), 1)
    pmask = jnp.where(is_ctx, 1, 3)
    rmask = jnp.where(is_ctx, rows - 1, GRID_COLS - 1)
    cls = lane & pmask
    rr = row & rmask
    prev = pltpu.roll(zc, 1, 0)
    nxt = pltpu.roll(zc, rows - 1, 0)
    up = jnp.concatenate([zp, zc[:rows - GRID_COLS]], axis=0)
    down = jnp.concatenate([zc[GRID_COLS:], zn], axis=0)
    zs = jnp.where((cls == 0) & (rr != 0), prev,
                   jnp.where((cls == 1) & (rr != rmask), nxt,
                             jnp.where(cls == 2, up, jnp.where(cls == 3, down, 0.0))))
    return zc + (zs - zc) * mu


def _rprep_kernel(zc_ref, zp_ref, zn_ref, lc_ref, lp_ref, ln_ref,
                  mu_ref, mul_ref, w0_ref, w2f_ref, w2b_ref, a0_ref, a2_ref, g2_ref,
                  kk_w_ref, ka_w_ref, rk_w_ref, e_ref, et_ref,
                  r_o, k_o, v_o, kk_o, kka_o, df_o, db_o, bonus_o, gate_o):
    tb = pl.program_id(1)
    ntb = pl.num_programs(1)
    is_ctx = tb == 0
    has_up = (tb > 1).astype(F32)
    has_down = (tb < ntb - 1).astype(F32)

    zl = _shift_mix(lc_ref[0], lp_ref[0] * has_up, ln_ref[0] * has_down, mul_ref[...], is_ctx)
    def decay(window, w2p_ref, d):
        w = -_softplus(-(w0_ref[d] + _dot(jnp.tanh(window), w2p_ref[...], HI))) - 0.5
        return jnp.exp(-jnp.exp(w))

    df_o[0] = decay(zl[:, 0:128], w2f_ref, 0)
    db_o[0] = decay(zl[:, 0:256], w2b_ref, 1)
    a = _sigmoid(a0_ref[...] + _dot(zl[:, 128:256], a2_ref[...], HI))
    gate_o[0] = _dot(_sigmoid(zl[:, 128:384]), g2_ref[...], HI)

    def piece(i):
        sl = slice(i * D, (i + 1) * D)
        return _shift_mix(zc_ref[0, :, sl], zp_ref[0, :, sl] * has_up, zn_ref[0, :, sl] * has_down,
                          mu_ref[:, sl], is_ctx)

    r = piece(0)
    kr = piece(1)
    vr = piece(2)
    e = e_ref[...]
    et = et_ref[...]
    kkraw = kr * kk_w_ref[...]
    nrm = jnp.sqrt(_dot(kkraw * kkraw, e, HI))
    inv = 1.0 / jnp.maximum(nrm, 1e-12)
    kk = kkraw * _dot(inv, et, HI)
    k2 = kr * (1.0 + (a - 1.0) * ka_w_ref[...])
    rk = _dot(r * k2 * rk_w_ref[...], e, HI)
    r_o[0] = r
    k_o[0] = k2
    v_o[0] = vr
    kk_o[0] = kk
    kka_o[0] = kk * a
    bonus_o[0] = _dot(rk, et, HI) * vr


def _rprep(z, mu_main, mu_lora, w0, w2f, w2b, a0, a2, g2, k_k, k_a, r_k, e, et):
    b, t, _ = z.shape
    ntb = t // ROWS
    q = ROWS // GRID_COLS
    last64 = t // GRID_COLS - 1
    wm = 3 * D
    cm = C_RWKV // wm
    cl = C_LORA // 512
    cur = lambda c: (lambda i, j: (i, j, c))
    prv = lambda c: (lambda i, j: (i, jnp.maximum(j * q - 1, 0), c))
    nxt = lambda c: (lambda i, j: (i, jnp.minimum(j * q + q, last64), c))
    full = lambda *s: pl.BlockSpec(s, lambda i, j: (0,) * len(s))
    out_spec = pl.BlockSpec((1, ROWS, D), lambda i, j: (i, j, 0))
    out_sd = jax.ShapeDtypeStruct((b, t, D), F32)
    return pl.pallas_call(
        _rprep_kernel,
        grid=(b, ntb),
        in_specs=[pl.BlockSpec((1, ROWS, wm), cur(cm)),
                  pl.BlockSpec((1, GRID_COLS, wm), prv(cm)),
                  pl.BlockSpec((1, GRID_COLS, wm), nxt(cm)),
                  pl.BlockSpec((1, ROWS, 512), cur(cl)),
                  pl.BlockSpec((1, GRID_COLS, 512), prv(cl)),
                  pl.BlockSpec((1, GRID_COLS, 512), nxt(cl)),
                  full(1, wm), full(1, 512), full(2, 1, D), full(128, D), full(256, D),
                  full(1, D), full(128, D), full(256, D),
                  full(1, D), full(1, D), full(1, D), full(D, R_HEADS), full(R_HEADS, D)],
        out_specs=[out_spec] * 9,
        out_shape=[out_sd] * 9,
        compiler_params=_cparams(("parallel", "parallel"), 52),
        name="rprep",
    )(z, z, z, z, z, z, mu_main, mu_lora, w0, w2f, w2b, a0, a2, g2, k_k, k_a, r_k, e, et)


R_STEPS = 64


def _rwkv_kernel(r_ref, k_ref, v_ref, kk_ref, kka_ref, d_ref, o_ref, s_ref):
    dirn = pl.program_id(0)

    @pl.when(pl.program_id(1) == 0)
    def _():
        s_ref[...] = jnp.zeros_like(s_ref)

    def step(i, carry):
        t = jnp.where(dirn == 0, i, R_STEPS - 1 - i)
        kk_t = kk_ref[t]
        sa = s_ref[0] * kk_t[0:1, :]
        for k in range(1, R_N):
            sa = sa + s_ref[k] * kk_t[k:k + 1, :]
        d_t = d_ref[0, t]
        kka_t = kka_ref[t]
        k_t = k_ref[t]
        r_t = r_ref[t]
        vv = v_ref[t]
        out = None
        for k in range(R_N):
            sk = s_ref[k] * d_t[k:k + 1, :] - sa * kka_t[k:k + 1, :] + vv * k_t[k:k + 1, :]
            s_ref[k] = sk
            term = sk * r_t[k:k + 1, :]
            out = term if out is None else out + term
        o_ref[0, t] = out
        return carry

    lax.fori_loop(0, R_STEPS, step, 0)


def _rwkv_order(dirn, nb, n_ctx_blk, n_blk):
    back = jnp.where(nb < n_ctx_blk, n_ctx_blk - 1 - nb, n_blk + n_ctx_blk - 1 - nb)
    return jnp.where(dirn == 0, nb, back)


def _rwkv(r_t, k_t, v_t, kk_t, kka_t, d_t):
    t, _, nch = r_t.shape
    n_blk = t // R_STEPS
    n_ctx_blk = N_CTX // R_STEPS
    blk = lambda d, n: (_rwkv_order(d, n, n_ctx_blk, n_blk), 0, 0)
    blk4 = lambda d, n: (d, _rwkv_order(d, n, n_ctx_blk, n_blk), 0, 0)
    spec = pl.BlockSpec((R_STEPS, R_N, nch), blk)
    spec4 = pl.BlockSpec((1, R_STEPS, R_N, nch), blk4)
    return pl.pallas_call(
        _rwkv_kernel,
        grid=(2, n_blk),
        in_specs=[spec, spec, spec, spec, spec, spec4],
        out_specs=spec4,
        out_shape=jax.ShapeDtypeStruct((2, t, R_N, nch), F32),
        scratch_shapes=[pltpu.VMEM((R_N, R_N, nch), F32)],
        compiler_params=_cparams(("arbitrary", "arbitrary"), 48),
        name="rwkv",
    )(r_t, k_t, v_t, kk_t, kka_t, d_t)


def _log_sigmoid(x):
    return jnp.minimum(x, 0.0) - jnp.log(1.0 + jnp.exp(-jnp.abs(x)))


def _gla_dir(q, k, v, alo, wa, ba, st_ref, reverse):
    c = q.shape[0]
    g = _log_sigmoid(_dot(alo, wa, HI) + ba) * (1.0 / G_GATE_NORM)
    row = lax.broadcasted_iota(jnp.int32, (c, c), 0)
    col = lax.broadcasted_iota(jnp.int32, (c, c), 1)
    tri = (row <= col) if reverse else (row >= col)
    bcum = _dot(tri.astype(F32), g, HI)
    blast = bcum[0:1] if reverse else bcum[c - 1:c]
    qd = q * (G_DK ** -0.5) * jnp.exp(bcum)
    kd = k * jnp.exp(-bcum)
    att = jnp.where(tri, _dot_nt(qd.astype(BF16), kd.astype(BF16)), 0.0)
    st = st_ref[...]
    o = _dot_nt(qd.astype(BF16), st.astype(BF16)) + _dot(att.astype(BF16), v.astype(BF16))
    kl = k * jnp.exp(blast - bcum)
    st_ref[...] = st * jnp.exp(blast) + _dot_tn(v.astype(BF16), kl.astype(BF16))
    return o


def _gla_kernel(qf, kf, vf, lf, qb, kb, vb, lb, wa_ref, ba_ref, of_ref, ob_ref, sf_ref, sb_ref):
    @pl.when(pl.program_id(2) == 0)
    def _():
        sf_ref[...] = jnp.zeros_like(sf_ref)
        sb_ref[...] = jnp.zeros_like(sb_ref)

    of_ref[0] = _gla_dir(qf[0], kf[0], vf[0], lf[0], wa_ref[0], ba_ref[0], sf_ref, False)
    ob_ref[0] = _gla_dir(qb[0], kb[0], vb[0], lb[0], wa_ref[1], ba_ref[1], sb_ref, True)


def _gla(z, w_a2, b_a):
    b, t, _ = z.shape
    n_blk = t // G_CHUNK
    n_ctx_blk = N_CTX // G_CHUNK
    fwd = lambda n: n
    bwd = lambda n: _rwkv_order(1, n, n_ctx_blk, n_blk)
    cq, ck, cv = C_GLA // G_DK, (C_GLA + 512) // G_DK, (C_GLA + 1024) // G_DV
    cl = C_LORA // 128

    def specs(order):
        return [pl.BlockSpec((1, G_CHUNK, G_DK), lambda i, h, n: (i, order(n), cq + h)),
                pl.BlockSpec((1, G_CHUNK, G_DK), lambda i, h, n: (i, order(n), ck + h)),
                pl.BlockSpec((1, G_CHUNK, G_DV), lambda i, h, n: (i, order(n), cv + h)),
                pl.BlockSpec((1, G_CHUNK, 128), lambda i, h, n: (i, order(n), cl))]

    out_sd = jax.ShapeDtypeStruct((b, t, D), F32)
    return pl.pallas_call(
        _gla_kernel,
        grid=(b, G_HEADS, n_blk),
        in_specs=specs(fwd) + specs(bwd) + [
            pl.BlockSpec((2, 128, G_DK), lambda i, h, n: (0, 0, h)),
            pl.BlockSpec((2, 1, G_DK), lambda i, h, n: (0, 0, h))],
        out_specs=[pl.BlockSpec((1, G_CHUNK, G_DV), lambda i, h, n: (i, fwd(n), h)),
                   pl.BlockSpec((1, G_CHUNK, G_DV), lambda i, h, n: (i, bwd(n), h))],
        out_shape=[out_sd, out_sd],
        scratch_shapes=[pltpu.VMEM((G_DV, G_DK), F32), pltpu.VMEM((G_DV, G_DK), F32)],
        compiler_params=_cparams(("parallel", "parallel", "arbitrary"), 24),
        name="gla",
    )(z, z, z, z, z, z, z, z, w_a2, b_a)


def _mix_kernel(ogf_ref, ogb_ref, gout_ref, orf_ref, orb_ref, bonus_ref, gr_ref, gg_ref, grw_ref, x_ref,
                gnw_ref, lnw_ref, lnb_ref, e_ref, et_ref, wout_ref, m2_ref, n2w_ref, m3_ref, m4_ref,
                h1_o, hm_o, hmb_o):
    og = ogf_ref[0] + ogb_ref[0]
    parts = []
    for h in range(G_HEADS):
        seg = og[:, h * G_DV:(h + 1) * G_DV]
        parts.append(seg * lax.rsqrt(jnp.mean(seg * seg, axis=-1, keepdims=True) + EPS))
    gout = gout_ref[0]
    y_gla = jnp.concatenate(parts, axis=-1) * gnw_ref[...] * (gout * _sigmoid(gout))

    e = e_ref[...]
    et = et_ref[...]
    o_r = orf_ref[0] + orb_ref[0]
    mu = _dot(_dot(o_r, e, HI) * (1.0 / R_N), et, HI)
    dlt = o_r - mu
    var = _dot(dlt * dlt, e, HI) * (1.0 / R_N)
    o_r = dlt * _dot(lax.rsqrt(var + R_LN_EPS), et, HI) * lnw_ref[...] + lnb_ref[...] + bonus_ref[0]
    y_rwkv = o_r * gr_ref[0]

    y = _sigmoid(gg_ref[0]) * y_gla + _sigmoid(grw_ref[0]) * y_rwkv
    h1 = x_ref[0] + m2_ref[0] * _dot(y.astype(BF16), wout_ref[...])
    h1_o[0] = h1
    hn = h1 * lax.rsqrt(jnp.mean(h1 * h1, axis=-1, keepdims=True) + EPS) * n2w_ref[...]
    hm = hn * (1.0 + m4_ref[0]) + m3_ref[0]
    hm_o[0] = hm
    hmb_o[0] = hm.astype(BF16)


def _mix(og_f, og_b, z, or_f, or_b, bonus, gate_r, x, gnw, lnw, lnb, e, et, w_out, m2, n2w, m3, m4):
    b, tl, _ = x.shape
    off = N_CTX // ROWS
    seq = lambda c: pl.BlockSpec((1, ROWS, D), lambda i, j: (i, j + off, c))
    full = lambda *s: pl.BlockSpec(s, lambda i, j: (0,) * len(s))
    per_b = pl.BlockSpec((1, 1, D), lambda i, j: (i, 0, 0))
    lat = pl.BlockSpec((1, ROWS, D), lambda i, j: (i, j, 0))
    return pl.pallas_call(
        _mix_kernel,
        grid=(b, tl // ROWS),
        in_specs=[seq(0), seq(0), seq((C_GLA + 2048) // D), seq(0), seq(0), seq(0), seq(0),
                  seq(C_GATE // D), seq(C_GATE // D + 1), lat,
                  full(1, D), full(1, D), full(1, D), full(D, R_HEADS), full(R_HEADS, D), full(D, D),
                  per_b, full(1, D), per_b, per_b],
        out_specs=[lat, lat, lat],
        out_shape=[jax.ShapeDtypeStruct((b, tl, D), F32), jax.ShapeDtypeStruct((b, tl, D), F32),
                   jax.ShapeDtypeStruct((b, tl, D), BF16)],
        compiler_params=_cparams(("parallel", "parallel"), 52),
        name="mix",
    )(og_f, og_b, z, or_f, or_b, bonus, gate_r, z, z, x, gnw, lnw, lnb, e, et, w_out, m2, n2w, m3, m4)


ROUTE_TOK = 256


def _top_values(work, top_ref, n):
    def body(i, w):
        m = jnp.max(w, axis=0, keepdims=True)
        top_ref[pl.ds(i, 1), :] = m
        return jnp.where(w == m, -jnp.inf, w)
    lax.fori_loop(0, n, body, work)


def _route_kernel(x_ref, wq_ref, keys_ref, s1_o, e1_o, s2_o, e2_o, tau_o, q_scr, t1_scr, t2_scr, cand_scr):
    q_scr[...] = _dot_nt(wq_ref[...], x_ref[...], HI)
    ninf = -jnp.inf

    def head(h, carry):
        q1 = q_scr[pl.ds(pl.multiple_of(h * 2 * P_KEYS, P_KEYS), P_KEYS), :]
        q2 = q_scr[pl.ds(pl.multiple_of(h * 2 * P_KEYS + P_KEYS, P_KEYS), P_KEYS), :]
        s1 = _dot(keys_ref[2 * h], q1, HI)
        s2 = _dot(keys_ref[2 * h + 1], q2, HI)
        _top_values(s1, t1_scr, P_TOPK)
        _top_values(s2, t2_scr, P_TOPK)
        t1 = t1_scr[...]
        t2 = t2_scr[...]
        for i in range(P_TOPK):
            cand_scr[i] = t1_scr[pl.ds(i, 1), :] + t2
        cand = cand_scr[...]
        cmax = t1[0:1] + t2[0:1]

        def pick(i, st):
            w, z, _ = st
            m = jnp.max(jnp.max(w, axis=0), axis=0, keepdims=True)
            return jnp.where(w == m[None], ninf, w), z + jnp.exp(m - cmax), m

        _, zsum, tau = lax.fori_loop(0, P_TOPK, pick, (cand, jnp.zeros_like(cmax), cmax))
        sel1 = s1 >= t1[P_TOPK - 1:P_TOPK]
        sel2 = s2 >= t2[P_TOPK - 1:P_TOPK]
        s1_o[h] = jnp.where(sel1, s1, ninf)
        s2_o[h] = jnp.where(sel2, s2, ninf)
        e1_o[h] = jnp.where(sel1, jnp.exp(s1 - t1[0:1]), 0.0)
        e2_o[h] = jnp.where(sel2, jnp.exp(s2 - t2[0:1]) / zsum, 0.0)
        tau_o[pl.ds(h, 1), :] = tau
        return carry

    lax.fori_loop(0, P_HEADS, head, 0)


def _route(hm, wq_t, keys):
    n = hm.shape[0]
    tn = ROUTE_TOK
    big = pl.BlockSpec((P_HEADS, P_KEYS, tn), lambda i: (0, 0, i))
    big_sd = jax.ShapeDtypeStruct((P_HEADS, P_KEYS, n), F32)
    return pl.pallas_call(
        _route_kernel,
        grid=(n // tn,),
        in_specs=[pl.BlockSpec((tn, D), lambda i: (i, 0)),
                  pl.BlockSpec((2 * P_HEADS * P_KEYS, D), lambda i: (0, 0)),
                  pl.BlockSpec((2 * P_HEADS, P_KEYS, P_KEYS), lambda i: (0, 0, 0))],
        out_specs=[big, big, big, big, pl.BlockSpec((P_HEADS, tn), lambda i: (0, i))],
        out_shape=[big_sd, big_sd, big_sd, big_sd, jax.ShapeDtypeStruct((P_HEADS, n), F32)],
        scratch_shapes=[pltpu.VMEM((2 * P_HEADS * P_KEYS, tn), F32),
                        pltpu.VMEM((P_TOPK, tn), F32), pltpu.VMEM((P_TOPK, tn), F32),
                        pltpu.VMEM((P_TOPK, P_TOPK, tn), F32)],
        compiler_params=_cparams(("parallel",), 48),
        name="route",
    )(hm, wq_t, keys)


PEER_TOK = 512
PEER_EXP = 256


def _gelu(x):
    return 0.5 * x * (1.0 + lax.erf(x * (2.0 ** -0.5)))


def _peer_kernel(x_ref, u_ref, vt_ref, s1_ref, e1_ref, s2_ref, e2_ref, tau_ref, o_ref, acc_ref):
    j = pl.program_id(1)

    @pl.when(j == 0)
    def _():
        acc_ref[...] = jnp.zeros_like(acc_ref)

    act = _gelu(_dot_nt(u_ref[...], x_ref[...]))
    n_a = PEER_EXP // P_KEYS
    gates = []
    for al in range(n_a):
        a = j * n_a + al
        g = None
        for h in range(P_HEADS):
            s1 = s1_ref[h, pl.ds(a, 1), :]
            e1 = e1_ref[h, pl.ds(a, 1), :]
            hit = (s1 + s2_ref[h]) >= tau_ref[h:h + 1, :]
            term = jnp.where(hit, e1 * e2_ref[h], 0.0)
            g = term if g is None else g + term
        gates.append(g)
    p = (act * jnp.concatenate(gates, axis=0)).astype(BF16)
    acc_ref[...] += _dot(vt_ref[...], p)

    @pl.when(j == pl.num_programs(1) - 1)
    def _():
        o_ref[...] = acc_ref[...]


def _peer(hm_b, u_b, vt_b, s1, e1, s2, e2, tau):
    n = hm_b.shape[0]
    n_exp = u_b.shape[0]
    tn, te = PEER_TOK, PEER_EXP
    big = pl.BlockSpec((P_HEADS, P_KEYS, tn), lambda i, j: (0, 0, i))
    return pl.pallas_call(
        _peer_kernel,
        grid=(n // tn, n_exp // te),
        in_specs=[pl.BlockSpec((tn, D), lambda i, j: (i, 0)),
                  pl.BlockSpec((te, D), lambda i, j: (j, 0)),
                  pl.BlockSpec((D, te), lambda i, j: (0, j)),
                  big, big, big, big,
                  pl.BlockSpec((P_HEADS, tn), lambda i, j: (0, i))],
        out_specs=pl.BlockSpec((D, tn), lambda i, j: (0, i)),
        out_shape=jax.ShapeDtypeStruct((D, n), F32),
        scratch_shapes=[pltpu.VMEM((D, tn), F32)],
        compiler_params=_cparams(("parallel", "arbitrary"), 52),
        name="peer",
    )(hm_b, u_b, vt_b, s1, e1, s2, e2, tau)


def _final_kernel(h1_ref, f_ref, m5_ref, w_ref, o_ref):
    h = h1_ref[0] + m5_ref[0] * f_ref[0]
    o_ref[0] = h * lax.rsqrt(jnp.mean(h * h, axis=-1, keepdims=True) + EPS) * w_ref[...]


def _final(h1, f, m5, w):
    b, tl, _ = h1.shape
    lat = pl.BlockSpec((1, ROWS, D), lambda i, j: (i, j, 0))
    return pl.pallas_call(
        _final_kernel,
        grid=(b, tl // ROWS),
        in_specs=[lat, lat, pl.BlockSpec((1, 1, D), lambda i, j: (i, 0, 0)),
                  pl.BlockSpec((1, D), lambda i, j: (0, 0))],
        out_specs=lat,
        out_shape=jax.ShapeDtypeStruct((b, tl, D), F32),
        compiler_params=_cparams(("parallel", "parallel"), 24),
        name="final",
    )(h1, f, m5, w)


def _reorder_cols(w):
    gla_main, gla_lora = w[..., 0:3072], w[..., 3072:3104]
    rw_main, rw_lora = w[..., 3104:6176], w[..., 6176:6528]
    gates = w[..., 6528:8576]
    pad = jnp.zeros(w.shape[:-1] + (512 - 32 - 352,), w.dtype)
    return jnp.concatenate([rw_main, gla_main, gates, gla_lora, rw_lora, pad], axis=-1)


def _to_chains(a):
    b, t, _ = a.shape
    return a.reshape(b, t, R_HEADS, R_N).transpose(1, 3, 0, 2).reshape(t, R_N, b * R_HEADS)


def _from_chains(a, b):
    t = a.shape[0]
    return a.reshape(t, R_N, b, R_HEADS).transpose(2, 0, 3, 1).reshape(b, t, D)


def kernel(x, c, ctx, c_ctx, norm1_w, w_mod, b_mod, w_in, gla_w_a2, gla_b_a, gla_norm_w, rwkv_mu, rwkv_w0, rwkv_w2, rwkv_a0, rwkv_a2, rwkv_g2, rwkv_k_k, rwkv_k_a, rwkv_r_k, rwkv_ln_w, rwkv_ln_b, w_out, norm2_w, peer_w_q, peer_sub_keys, peer_u, peer_v, final_norm_w):
    b, seq, _ = x.shape
    assert w_in.shape[0] == 1 and ctx.shape[1] == N_CTX and seq % ROWS == 0
    row = lambda v: v.reshape(1, -1)

    c16 = jnp.zeros((16, D), F32).at[:b].set(c).at[b].set(c_ctx)
    m = _mod(c16, w_mod[0], row(b_mod[0])).reshape(16, N_MOD, D)
    m_lat, m_ctx = m[:b], m[b]
    per_b = lambda i: m_lat[:, i].reshape(b, 1, D)
    shift1 = jnp.stack([jnp.broadcast_to(m_ctx[0], (b, D)), m_lat[:, 0]], axis=1).reshape(b, 2, 1, D)
    scale1 = jnp.stack([jnp.broadcast_to(m_ctx[1], (b, D)), m_lat[:, 1]], axis=1).reshape(b, 2, 1, D)

    h_all = jnp.concatenate([ctx, x], axis=1)
    z = _inproj(h_all, row(norm1_w[0]), scale1, shift1, _reorder_cols(w_in[0]).astype(BF16))

    mu = rwkv_mu[0]
    mu_lora = jnp.concatenate([jnp.zeros((32,), F32), mu[3072:], jnp.zeros((128,), F32)]).reshape(1, 512)
    e = jnp.repeat(jnp.eye(R_HEADS, dtype=F32), R_N, axis=0)
    w2f = jnp.zeros((128, D), F32).at[L_WF:L_WF + R_W_LORA].set(rwkv_w2[0, 0])
    w2b = jnp.zeros((256, D), F32).at[L_WB:L_WB + R_W_LORA].set(rwkv_w2[0, 1])
    a2 = jnp.zeros((128, D), F32).at[L_A - 128:L_A - 128 + R_A_LORA].set(rwkv_a2[0])
    g2 = jnp.zeros((256, D), F32).at[L_G - 128:L_G - 128 + R_G_LORA].set(rwkv_g2[0])
    wa = jnp.zeros((2, 128, G_HEADS * G_DK), F32)
    wa = wa.at[0, 0:G_LORA].set(gla_w_a2[0, 0]).at[1, G_LORA:2 * G_LORA].set(gla_w_a2[0, 1])
    r, k2, vr, kk, kka, d_f, d_b, bonus, gate_r = _rprep(
        z, row(mu[:3072]), mu_lora, rwkv_w0[0].reshape(2, 1, D), w2f, w2b, row(rwkv_a0[0]), a2,
        g2, row(rwkv_k_k[0]), row(rwkv_k_a[0]), row(rwkv_r_k[0]), e, e.T)

    o_r = _rwkv(_to_chains(r), _to_chains(k2), _to_chains(vr), _to_chains(kk), _to_chains(kka),
                jnp.stack([_to_chains(d_f), _to_chains(d_b)]))
    or_f, or_b = _from_chains(o_r[0], b), _from_chains(o_r[1], b)

    og_f, og_b = _gla(z, wa, gla_b_a[0].reshape(2, 1, -1))

    h1, hm, hm_b = _mix(og_f, og_b, z, or_f, or_b, bonus, gate_r, x, row(gla_norm_w[0]), row(rwkv_ln_w[0]),
                        row(rwkv_ln_b[0]), e, e.T, w_out[0].astype(BF16), per_b(2), row(norm2_w[0]),
                        per_b(3), per_b(4))

    n_tok = b * seq
    s1, e1, s2, e2, tau = _route(hm.reshape(n_tok, D), peer_w_q[0].T,
                                 peer_sub_keys[0].reshape(2 * P_HEADS, P_KEYS, -1))
    f_t = _peer(hm_b.reshape(n_tok, D), peer_u[0].astype(BF16), peer_v[0].T.astype(BF16), s1, e1, s2, e2, tau)
    return _final(h1, f_t.T.reshape(b, seq, D), per_b(5), row(final_norm_w))
```

```python
import functools

import numpy as np
import jax
import jax.numpy as jnp
from jax import lax
from jax.experimental import pallas as pl
from jax.experimental.pallas import tpu as pltpu

F32 = jnp.float32
BF16 = jnp.bfloat16
HI = lax.Precision.HIGHEST

D = 1024
GRID_COLS = 64
N_CTX = 256
EPS = 1e-6
N_MOD = 6

G_HEADS = 4
G_DV = D // G_HEADS
G_DK = G_DV // 2
G_LORA = 16
G_GATE_NORM = 16.0
G_CHUNK = 64

R_N = 64
R_HEADS = D // R_N
R_LN_EPS = 64e-5
R_W_LORA = 64
R_A_LORA = 64
R_G_LORA = 160

P_HEADS = 8
P_KEYS = 128
P_TOPK = 16

C_RWKV = 0
C_GLA = 3072
C_GATE = 6144
C_LORA = 8192
N_PROJ = 8704
L_WF, L_WB, L_A, L_G = 32, 96, 160, 224

ROWS = 256
V7X_VMEM_BYTES = 64 * 1024 * 1024


def _cparams(sem, vmem_mb):
    return pltpu.CompilerParams(dimension_semantics=sem,
                                vmem_limit_bytes=min(vmem_mb * 1024 * 1024, V7X_VMEM_BYTES - (8 << 20)))


def _sigmoid(x):
    return 1.0 / (1.0 + jnp.exp(-x))


def _softplus(x):
    return jnp.maximum(x, 0.0) + jnp.log(1.0 + jnp.exp(-jnp.abs(x)))


def _dot(a, b, precision=None):
    return jnp.dot(a, b, preferred_element_type=F32, precision=precision)


def _dot_nt(a, b, precision=None):
    return lax.dot_general(a, b, (((1,), (1,)), ((), ())), preferred_element_type=F32, precision=precision)


def _dot_tn(a, b, precision=None):
    return lax.dot_general(a, b, (((0,), (0,)), ((), ())), preferred_element_type=F32, precision=precision)


def _mod_kernel(c_ref, w_ref, b_ref, o_ref):
    c = c_ref[...]
    s = c * _sigmoid(c)
    o_ref[...] = _dot(s.astype(BF16), w_ref[...].astype(BF16)) + b_ref[...]


def _mod(c16, w_mod, b_mod):
    n = w_mod.shape[1]
    tn = 768
    return pl.pallas_call(
        _mod_kernel,
        grid=(n // tn,),
        in_specs=[pl.BlockSpec((16, D), lambda j: (0, 0)),
                  pl.BlockSpec((D, tn), lambda j: (0, j)),
                  pl.BlockSpec((1, tn), lambda j: (0, j))],
        out_specs=pl.BlockSpec((16, tn), lambda j: (0, j)),
        out_shape=jax.ShapeDtypeStruct((16, n), F32),
        compiler_params=_cparams(("parallel",), 24),
        name="mod",
    )(c16, w_mod, b_mod)


def _inproj_kernel(h_ref, nw_ref, sc_ref, sh_ref, w_ref, o_ref, a_scr):
    @pl.when(pl.program_id(2) == 0)
    def _():
        x = h_ref[0]
        y = x * lax.rsqrt(jnp.mean(x * x, axis=-1, keepdims=True) + EPS) * nw_ref[...]
        a_scr[...] = (y * (1.0 + sc_ref[0, 0]) + sh_ref[0, 0]).astype(BF16)

    o_ref[0] = _dot(a_scr[...], w_ref[...])


def _inproj(h_all, norm_w, scale, shift, w_p):
    b, t, _ = h_all.shape
    tn = 512
    mod_idx = lambda i, j, n: (i, jnp.minimum(j, 1), 0, 0)
    return pl.pallas_call(
        _inproj_kernel,
        grid=(b, t // ROWS, N_PROJ // tn),
        in_specs=[pl.BlockSpec((1, ROWS, D), lambda i, j, n: (i, j, 0)),
                  pl.BlockSpec((1, D), lambda i, j, n: (0, 0)),
                  pl.BlockSpec((1, 1, 1, D), mod_idx),
                  pl.BlockSpec((1, 1, 1, D), mod_idx),
                  pl.BlockSpec((D, tn), lambda i, j, n: (0, n))],
        out_specs=pl.BlockSpec((1, ROWS, tn), lambda i, j, n: (i, j, n)),
        out_shape=jax.ShapeDtypeStruct((b, t, N_PROJ), F32),
        scratch_shapes=[pltpu.VMEM((ROWS, D), BF16)],
        compiler_params=_cparams(("parallel", "parallel", "arbitrary"), 24),
        name="inproj",
    )(h_all, norm_w, scale, shift, w_p)


def _shift_mix(zc, zp, zn, mu, is_ctx):
    rows, w = zc.shape
    row = lax.broadcasted_iota(jnp.int32, (rows, w), 0)
    lane = lax.broadcasted_iota(jnp.int32, (rows, w), 1)
    pmask = jnp.where(is_ctx, 1, 3)
    rmask = jnp.where(is_ctx, rows - 1, GRID_COLS - 1)
    cls = lane & pmask
    rr = row & rmask
    prev = pltpu.roll(zc, 1, 0)
    nxt = pltpu.roll(zc, rows - 1, 0)
    up = jnp.concatenate([zp, zc[:rows - GRID_COLS]], axis=0)
    down = jnp.concatenate([zc[GRID_COLS:], zn], axis=0)
    zs = jnp.where((cls == 0) & (rr != 0), prev,
                   jnp.where((cls == 1) & (rr != rmask), nxt,
                             jnp.where(cls == 2, up, jnp.where(cls == 3, down, 0.0))))
    return zc + (zs - zc) * mu


def _rprep_kernel(zc_ref, zp_ref, zn_ref, lc_ref, lp_ref, ln_ref,
                  mu_ref, mul_ref, w0_ref, w2f_ref, w2b_ref, a0_ref, a2_ref, g2_ref,
                  kk_w_ref, ka_w_ref, rk_w_ref, e_ref, et_ref,
                  r_o, k_o, v_o, kk_o, kka_o, df_o, db_o, bonus_o, gate_o):
    tb = pl.program_id(1)
    ntb = pl.num_programs(1)
    is_ctx = tb == 0
    has_up = (tb > 1).astype(F32)
    has_down = (tb < ntb - 1).astype(F32)

    zl = _shift_mix(lc_ref[0], lp_ref[0] * has_up, ln_ref[0] * has_down, mul_ref[...], is_ctx)
    def decay(window, w2p_ref, d):
        w = -_softplus(-(w0_ref[d] + _dot(jnp.tanh(window), w2p_ref[...], HI))) - 0.5
        return jnp.exp(-jnp.exp(w))

    df_o[0] = decay(zl[:, 0:128], w2f_ref, 0)
    db_o[0] = decay(zl[:, 0:256], w2b_ref, 1)
    a = _sigmoid(a0_ref[...] + _dot(zl[:, 128:256], a2_ref[...], HI))
    gate_o[0] = _dot(_sigmoid(zl[:, 128:384]), g2_ref[...], HI)

    def piece(i):
        sl = slice(i * D, (i + 1) * D)
        return _shift_mix(zc_ref[0, :, sl], zp_ref[0, :, sl] * has_up, zn_ref[0, :, sl] * has_down,
                          mu_ref[:, sl], is_ctx)

    r = piece(0)
    kr = piece(1)
    vr = piece(2)
    e = e_ref[...]
    et = et_ref[...]
    kkraw = kr * kk_w_ref[...]
    nrm = jnp.sqrt(_dot(kkraw * kkraw, e, HI))
    inv = 1.0 / jnp.maximum(nrm, 1e-12)
    kk = kkraw * _dot(inv, et, HI)
    k2 = kr * (1.0 + (a - 1.0) * ka_w_ref[...])
    rk = _dot(r * k2 * rk_w_ref[...], e, HI)
    r_o[0] = r
    k_o[0] = k2
    v_o[0] = vr
    kk_o[0] = kk
    kka_o[0] = kk * a
    bonus_o[0] = _dot(rk, et, HI) * vr


def _rprep(z, mu_main, mu_lora, w0, w2f, w2b, a0, a2, g2, k_k, k_a, r_k, e, et):
    b, t, _ = z.shape
    ntb = t // ROWS
    q = ROWS // GRID_COLS
    last64 = t // GRID_COLS - 1
    wm = 3 * D
    cm = C_RWKV // wm
    cl = C_LORA // 512
    cur = lambda c: (lambda i, j: (i, j, c))
    prv = lambda c: (lambda i, j: (i, jnp.maximum(j * q - 1, 0), c))
    nxt = lambda c: (lambda i, j: (i, jnp.minimum(j * q + q, last64), c))
    full = lambda *s: pl.BlockSpec(s, lambda i, j: (0,) * len(s))
    out_spec = pl.BlockSpec((1, ROWS, D), lambda i, j: (i, j, 0))
    out_sd = jax.ShapeDtypeStruct((b, t, D), F32)
    return pl.pallas_call(
        _rprep_kernel,
        grid=(b, ntb),
        in_specs=[pl.BlockSpec((1, ROWS, wm), cur(cm)),
                  pl.BlockSpec((1, GRID_COLS, wm), prv(cm)),
                  pl.BlockSpec((1, GRID_COLS, wm), nxt(cm)),
                  pl.BlockSpec((1, ROWS, 512), cur(cl)),
                  pl.BlockSpec((1, GRID_COLS, 512), prv(cl)),
                  pl.BlockSpec((1, GRID_COLS, 512), nxt(cl)),
                  full(1, wm), full(1, 512), full(2, 1, D), full(128, D), full(256, D),
                  full(1, D), full(128, D), full(256, D),
                  full(1, D), full(1, D), full(1, D), full(D, R_HEADS), full(R_HEADS, D)],
        out_specs=[out_spec] * 9,
        out_shape=[out_sd] * 9,
        compiler_params=_cparams(("parallel", "parallel"), 52),
        name="rprep",
    )(z, z, z, z, z, z, mu_main, mu_lora, w0, w2f, w2b, a0, a2, g2, k_k, k_a, r_k, e, et)


R_STEPS = 64


def _rwkv_kernel(r_ref, k_ref, v_ref, kk_ref, kka_ref, d_ref, o_ref, s_ref):
    dirn = pl.program_id(0)

    @pl.when(pl.program_id(1) == 0)
    def _():
        s_ref[...] = jnp.zeros_like(s_ref)

    def step(i, carry):
        t = jnp.where(dirn == 0, i, R_STEPS - 1 - i)
        kk_t = kk_ref[t]
        sa = s_ref[0] * kk_t[0:1, :]
        for k in range(1, R_N):
            sa = sa + s_ref[k] * kk_t[k:k + 1, :]
        d_t = d_ref[0, t]
        kka_t = kka_ref[t]
        k_t = k_ref[t]
        r_t = r_ref[t]
        vv = v_ref[t]
        out = None
        for k in range(R_N):
            sk = s_ref[k] * d_t[k:k + 1, :] - sa * kka_t[k:k + 1, :] + vv * k_t[k:k + 1, :]
            s_ref[k] = sk
            term = sk * r_t[k:k + 1, :]
            out = term if out is None else out + term
        o_ref[0, t] = out
        return carry

    lax.fori_loop(0, R_STEPS, step, 0)


def _rwkv_order(dirn, nb, n_ctx_blk, n_blk):
    back = jnp.where(nb < n_ctx_blk, n_ctx_blk - 1 - nb, n_blk + n_ctx_blk - 1 - nb)
    return jnp.where(dirn == 0, nb, back)


def _rwkv(r_t, k_t, v_t, kk_t, kka_t, d_t):
    t, _, nch = r_t.shape
    n_blk = t // R_STEPS
    n_ctx_blk = N_CTX // R_STEPS
    blk = lambda d, n: (_rwkv_order(d, n, n_ctx_blk, n_blk), 0, 0)
    blk4 = lambda d, n: (d, _rwkv_order(d, n, n_ctx_blk, n_blk), 0, 0)
    spec = pl.BlockSpec((R_STEPS, R_N, nch), blk)
    spec4 = pl.BlockSpec((1, R_STEPS, R_N, nch), blk4)
    return pl.pallas_call(
        _rwkv_kernel,
        grid=(2, n_blk),
        in_specs=[spec, spec, spec, spec, spec, spec4],
        out_specs=spec4,
        out_shape=jax.ShapeDtypeStruct((2, t, R_N, nch), F32),
        scratch_shapes=[pltpu.VMEM((R_N, R_N, nch), F32)],
        compiler_params=_cparams(("arbitrary", "arbitrary"), 48),
        name="rwkv",
    )(r_t, k_t, v_t, kk_t, kka_t, d_t)


def _log_sigmoid(x):
    return jnp.minimum(x, 0.0) - jnp.log(1.0 + jnp.exp(-jnp.abs(x)))


def _gla_dir(q, k, v, alo, wa, ba, st_ref, reverse):
    c = q.shape[0]
    g = _log_sigmoid(_dot(alo, wa, HI) + ba) * (1.0 / G_GATE_NORM)
    row = lax.broadcasted_iota(jnp.int32, (c, c), 0)
    col = lax.broadcasted_iota(jnp.int32, (c, c), 1)
    tri = (row <= col) if reverse else (row >= col)
    bcum = _dot(tri.astype(F32), g, HI)
    blast = bcum[0:1] if reverse else bcum[c - 1:c]
    qd = q * (G_DK ** -0.5) * jnp.exp(bcum)
    kd = k * jnp.exp(-bcum)
    att = jnp.where(tri, _dot_nt(qd.astype(BF16), kd.astype(BF16)), 0.0)
    st = st_ref[...]
    o = _dot_nt(qd.astype(BF16), st.astype(BF16)) + _dot(att.astype(BF16), v.astype(BF16))
    kl = k * jnp.exp(blast - bcum)
    st_ref[...] = st * jnp.exp(blast) + _dot_tn(v.astype(BF16), kl.astype(BF16))
    return o


def _gla_kernel(qf, kf, vf, lf, qb, kb, vb, lb, wa_ref, ba_ref, of_ref, ob_ref, sf_ref, sb_ref):
    @pl.when(pl.program_id(2) == 0)
    def _():
        sf_ref[...] = jnp.zeros_like(sf_ref)
        sb_ref[...] = jnp.zeros_like(sb_ref)

    of_ref[0] = _gla_dir(qf[0], kf[0], vf[0], lf[0], wa_ref[0], ba_ref[0], sf_ref, False)
    ob_ref[0] = _gla_dir(qb[0], kb[0], vb[0], lb[0], wa_ref[1], ba_ref[1], sb_ref, True)


def _gla(z, w_a2, b_a):
    b, t, _ = z.shape
    n_blk = t // G_CHUNK
    n_ctx_blk = N_CTX // G_CHUNK
    fwd = lambda n: n
    bwd = lambda n: _rwkv_order(1, n, n_ctx_blk, n_blk)
    cq, ck, cv = C_GLA // G_DK, (C_GLA + 512) // G_DK, (C_GLA + 1024) // G_DV
    cl = C_LORA // 128

    def specs(order):
        return [pl.BlockSpec((1, G_CHUNK, G_DK), lambda i, h, n: (i, order(n), cq + h)),
                pl.BlockSpec((1, G_CHUNK, G_DK), lambda i, h, n: (i, order(n), ck + h)),
                pl.BlockSpec((1, G_CHUNK, G_DV), lambda i, h, n: (i, order(n), cv + h)),
                pl.BlockSpec((1, G_CHUNK, 128), lambda i, h, n: (i, order(n), cl))]

    out_sd = jax.ShapeDtypeStruct((b, t, D), F32)
    return pl.pallas_call(
        _gla_kernel,
        grid=(b, G_HEADS, n_blk),
        in_specs=specs(fwd) + specs(bwd) + [
            pl.BlockSpec((2, 128, G_DK), lambda i, h, n: (0, 0, h)),
            pl.BlockSpec((2, 1, G_DK), lambda i, h, n: (0, 0, h))],
        out_specs=[pl.BlockSpec((1, G_CHUNK, G_DV), lambda i, h, n: (i, fwd(n), h)),
                   pl.BlockSpec((1, G_CHUNK, G_DV), lambda i, h, n: (i, bwd(n), h))],
        out_shape=[out_sd, out_sd],
        scratch_shapes=[pltpu.VMEM((G_DV, G_DK), F32), pltpu.VMEM((G_DV, G_DK), F32)],
        compiler_params=_cparams(("parallel", "parallel", "arbitrary"), 24),
        name="gla",
    )(z, z, z, z, z, z, z, z, w_a2, b_a)


def _mix_kernel(ogf_ref, ogb_ref, gout_ref, orf_ref, orb_ref, bonus_ref, gr_ref, gg_ref, grw_ref, x_ref,
                gnw_ref, lnw_ref, lnb_ref, e_ref, et_ref, wout_ref, m2_ref, n2w_ref, m3_ref, m4_ref,
                h1_o, hmb_o):
    og = ogf_ref[0] + ogb_ref[0]
    parts = []
    for h in range(G_HEADS):
        seg = og[:, h * G_DV:(h + 1) * G_DV]
        parts.append(seg * lax.rsqrt(jnp.mean(seg * seg, axis=-1, keepdims=True) + EPS))
    gout = gout_ref[0]
    y_gla = jnp.concatenate(parts, axis=-1) * gnw_ref[...] * (gout * _sigmoid(gout))

    e = e_ref[...]
    et = et_ref[...]
    o_r = orf_ref[0] + orb_ref[0]
    mu = _dot(_dot(o_r, e, HI) * (1.0 / R_N), et, HI)
    dlt = o_r - mu
    var = _dot(dlt * dlt, e, HI) * (1.0 / R_N)
    o_r = dlt * _dot(lax.rsqrt(var + R_LN_EPS), et, HI) * lnw_ref[...] + lnb_ref[...] + bonus_ref[0]
    y_rwkv = o_r * gr_ref[0]

    y = _sigmoid(gg_ref[0]) * y_gla + _sigmoid(grw_ref[0]) * y_rwkv
    h1 = x_ref[0] + m2_ref[0] * _dot(y.astype(BF16), wout_ref[...])
    h1_o[0] = h1
    hn = h1 * lax.rsqrt(jnp.mean(h1 * h1, axis=-1, keepdims=True) + EPS) * n2w_ref[...]
    hmb_o[0] = (hn * (1.0 + m4_ref[0]) + m3_ref[0]).astype(BF16)


def _mix(og_f, og_b, z, or_f, or_b, bonus, gate_r, x, gnw, lnw, lnb, e, et, w_out, m2, n2w, m3, m4):
    b, tl, _ = x.shape
    off = N_CTX // ROWS
    seq = lambda c: pl.BlockSpec((1, ROWS, D), lambda i, j: (i, j + off, c))
    full = lambda *s: pl.BlockSpec(s, lambda i, j: (0,) * len(s))
    per_b = pl.BlockSpec((1, 1, D), lambda i, j: (i, 0, 0))
    lat = pl.BlockSpec((1, ROWS, D), lambda i, j: (i, j, 0))
    return pl.pallas_call(
        _mix_kernel,
        grid=(b, tl // ROWS),
        in_specs=[seq(0), seq(0), seq((C_GLA + 2048) // D), seq(0), seq(0), seq(0), seq(0),
                  seq(C_GATE // D), seq(C_GATE // D + 1), lat,
                  full(1, D), full(1, D), full(1, D), full(D, R_HEADS), full(R_HEADS, D), full(D, D),
                  per_b, full(1, D), per_b, per_b],
        out_specs=[lat, lat],
        out_shape=[jax.ShapeDtypeStruct((b, tl, D), F32), jax.ShapeDtypeStruct((b, tl, D), BF16)],
        compiler_params=_cparams(("parallel", "parallel"), 52),
        name="mix",
    )(og_f, og_b, z, or_f, or_b, bonus, gate_r, z, z, x, gnw, lnw, lnb, e, et, w_out, m2, n2w, m3, m4)


ROUTE_TOK = 256


def _top_values(work, top_ref, n):
    def body(i, w):
        m = jnp.max(w, axis=0, keepdims=True)
        top_ref[pl.ds(i, 1), :] = m
        return jnp.where(w == m, -jnp.inf, w)
    lax.fori_loop(0, n, body, work)


N_CAND = 80


def _dup_bf16(x):
    hi = lax.bitcast_convert_type(x.astype(BF16).astype(F32), jnp.uint32)
    return hi | (hi >> 16)


def _route_kernel(x_ref, wq_ref, keys_ref, n_o, e1_o, rho_o, e2_o, q_scr, t1_scr, t2_scr, cand_scr):
    q_scr[...] = _dot_nt(wq_ref[...], x_ref[...])
    ninf = -jnp.inf
    half = P_TOPK // 2

    def head(h, carry):
        q1 = q_scr[pl.ds(pl.multiple_of(h * 2 * P_KEYS, P_KEYS), P_KEYS), :]
        q2 = q_scr[pl.ds(pl.multiple_of(h * 2 * P_KEYS + P_KEYS, P_KEYS), P_KEYS), :]
        s1 = _dot(keys_ref[2 * h], q1, HI)
        s2 = _dot(keys_ref[2 * h + 1], q2, HI)
        _top_values(s1, t1_scr, P_TOPK)

        def rank(i, st):
            w, rho = st
            m = jnp.max(w, axis=0, keepdims=True)
            t2_scr[pl.ds(i, 1), :] = m
            top = w == m
            return jnp.where(top, ninf, w), jnp.where(top, i.astype(F32), rho)

        _, rho = lax.fori_loop(0, P_TOPK, rank, (s2, jnp.full(s2.shape, float(P_TOPK), F32)))
        t1 = t1_scr[...]
        t2 = t2_scr[...]
        cand_scr[0:P_TOPK] = t1[0:1] + t2
        for i in range(1, half):
            cand_scr[P_TOPK + half * (i - 1):P_TOPK + half * i] = t1[i:i + 1] + t2[0:half]
        cand_scr[N_CAND - half:N_CAND] = t1[half:P_TOPK] + t2[0:1]
        cmax = t1[0:1] + t2[0:1]

        def pick(i, st):
            w, z, _ = st
            m = jnp.max(w, axis=0, keepdims=True)
            return jnp.where(w == m, ninf, w), z + jnp.exp(m - cmax), m

        _, zsum, tau = lax.fori_loop(0, P_TOPK, pick, (cand_scr[...], jnp.zeros_like(cmax), cmax))

        def count(j, acc):
            return acc + jnp.where(s1 + t2_scr[pl.ds(j, 1), :] >= tau, 1.0, 0.0)

        cnt = lax.fori_loop(0, P_TOPK, count, jnp.zeros_like(s1))
        sel1 = s1 >= t1[P_TOPK - 1:P_TOPK]
        sel2 = s2 >= t2[P_TOPK - 1:P_TOPK]
        n_o[h] = _dup_bf16(jnp.where(sel1, cnt, 0.0))
        e1_o[h] = _dup_bf16(jnp.where(sel1, jnp.exp(s1 - t1[0:1]), 0.0))
        rho_o[h] = rho.astype(BF16)
        e2_o[h] = jnp.where(sel2, jnp.exp(s2 - t2[0:1]) / zsum, 0.0).astype(BF16)
        return carry

    lax.fori_loop(0, P_HEADS, head, 0)


def _route(hm_b, wq_t, keys):
    n = hm_b.shape[0]
    tn = ROUTE_TOK
    big = pl.BlockSpec((P_HEADS, P_KEYS, tn), lambda i: (0, 0, i))
    sd = lambda dt: jax.ShapeDtypeStruct((P_HEADS, P_KEYS, n), dt)
    return pl.pallas_call(
        _route_kernel,
        grid=(n // tn,),
        in_specs=[pl.BlockSpec((tn, D), lambda i: (i, 0)),
                  pl.BlockSpec((2 * P_HEADS * P_KEYS, D), lambda i: (0, 0)),
                  pl.BlockSpec((2 * P_HEADS, P_KEYS, P_KEYS), lambda i: (0, 0, 0))],
        out_specs=[big, big, big, big],
        out_shape=[sd(jnp.uint32), sd(jnp.uint32), sd(BF16), sd(BF16)],
        scratch_shapes=[pltpu.VMEM((2 * P_HEADS * P_KEYS, tn), F32),
                        pltpu.VMEM((P_TOPK, tn), F32), pltpu.VMEM((P_TOPK, tn), F32),
                        pltpu.VMEM((N_CAND, tn), F32)],
        compiler_params=_cparams(("parallel",), 48),
        name="route",
    )(hm_b, wq_t, keys)


PEER_TOK = 512
PEER_EXP = 1024


def _gelu(x):
    return 0.5 * x * (1.0 + lax.erf(x * (2.0 ** -0.5)))


BF16_ROWS = 16


def _peer_kernel(x_ref, u_ref, vt_ref, n_ref, e1_ref, rho_ref, e2_ref, o_ref):
    j = pl.program_id(1)

    @pl.when(j == 0)
    def _():
        o_ref[...] = jnp.zeros_like(o_ref)

    tn = x_ref.shape[0]
    x = x_ref[...]
    n_a = PEER_EXP // P_KEYS
    groups = P_KEYS // BF16_ROWS
    halves = 2
    a_per_half = n_a // halves
    p_halves = []
    for half in range(halves):
        rows_h = a_per_half * P_KEYS
        act = _dot_nt(u_ref[half * rows_h:(half + 1) * rows_h, :], x)
        gates = []
        for al in range(half * a_per_half, (half + 1) * a_per_half):
            g = [None] * groups
            for h in range(P_HEADS):
                row = lambda ref: pltpu.bitcast(jnp.broadcast_to(ref[h, al:al + 1, :], (8, tn)), BF16)
                n_row = row(n_ref)
                e1_row = row(e1_ref)
                for q in range(groups):
                    rows = slice(q * BF16_ROWS, (q + 1) * BF16_ROWS)
                    term = jnp.where(rho_ref[h, rows, :] < n_row, e2_ref[h, rows, :], 0.0) * e1_row
                    g[q] = term if g[q] is None else g[q] + term
            gates.extend(g)
        p_halves.append(_gelu(act).astype(BF16) * jnp.concatenate(gates, axis=0))
    o_ref[...] += _dot(vt_ref[...], jnp.concatenate(p_halves, axis=0))


def _peer(hm_b, u_b, vt_b, n_cnt, e1, rho, e2):
    n = hm_b.shape[0]
    n_exp = u_b.shape[0]
    tn, te = PEER_TOK, PEER_EXP
    big = pl.BlockSpec((P_HEADS, P_KEYS, tn), lambda i, j: (0, 0, i))
    a_rows = pl.BlockSpec((P_HEADS, te // P_KEYS, tn), lambda i, j: (0, j, i))
    return pl.pallas_call(
        _peer_kernel,
        grid=(n // tn, n_exp // te),
        in_specs=[pl.BlockSpec((tn, D), lambda i, j: (i, 0)),
                  pl.BlockSpec((te, D), lambda i, j: (j, 0)),
                  pl.BlockSpec((D, te), lambda i, j: (0, j)),
                  a_rows, a_rows, big, big],
        out_specs=pl.BlockSpec((D, tn), lambda i, j: (0, i)),
        out_shape=jax.ShapeDtypeStruct((D, n), F32),
        compiler_params=_cparams(("parallel", "arbitrary"), 52),
        name="peer",
    )(hm_b, u_b, vt_b, n_cnt, e1, rho, e2)


def _final_kernel(h1_ref, f_ref, m5_ref, w_ref, o_ref):
    h = h1_ref[0] + m5_ref[0] * f_ref[0]
    o_ref[0] = h * lax.rsqrt(jnp.mean(h * h, axis=-1, keepdims=True) + EPS) * w_ref[...]


def _final(h1, f, m5, w):
    b, tl, _ = h1.shape
    lat = pl.BlockSpec((1, ROWS, D), lambda i, j: (i, j, 0))
    return pl.pallas_call(
        _final_kernel,
        grid=(b, tl // ROWS),
        in_specs=[lat, lat, pl.BlockSpec((1, 1, D), lambda i, j: (i, 0, 0)),
                  pl.BlockSpec((1, D), lambda i, j: (0, 0))],
        out_specs=lat,
        out_shape=jax.ShapeDtypeStruct((b, tl, D), F32),
        compiler_params=_cparams(("parallel", "parallel"), 24),
        name="final",
    )(h1, f, m5, w)


def _reorder_cols(w):
    gla_main, gla_lora = w[..., 0:3072], w[..., 3072:3104]
    rw_main, rw_lora = w[..., 3104:6176], w[..., 6176:6528]
    gates = w[..., 6528:8576]
    pad = jnp.zeros(w.shape[:-1] + (512 - 32 - 352,), w.dtype)
    return jnp.concatenate([rw_main, gla_main, gates, gla_lora, rw_lora, pad], axis=-1)


def _to_chains(a):
    b, t, _ = a.shape
    return a.reshape(b, t, R_HEADS, R_N).transpose(1, 3, 0, 2).reshape(t, R_N, b * R_HEADS)


def _from_chains(a, b):
    t = a.shape[0]
    return a.reshape(t, R_N, b, R_HEADS).transpose(2, 0, 3, 1).reshape(b, t, D)


def kernel(x, c, ctx, c_ctx, norm1_w, w_mod, b_mod, w_in, gla_w_a2, gla_b_a, gla_norm_w, rwkv_mu, rwkv_w0, rwkv_w2, rwkv_a0, rwkv_a2, rwkv_g2, rwkv_k_k, rwkv_k_a, rwkv_r_k, rwkv_ln_w, rwkv_ln_b, w_out, norm2_w, peer_w_q, peer_sub_keys, peer_u, peer_v, final_norm_w):
    b, seq, _ = x.shape
    assert w_in.shape[0] == 1 and ctx.shape[1] == N_CTX and seq % ROWS == 0
    row = lambda v: v.reshape(1, -1)

    c16 = jnp.zeros((16, D), F32).at[:b].set(c).at[b].set(c_ctx)
    m = _mod(c16, w_mod[0], row(b_mod[0])).reshape(16, N_MOD, D)
    m_lat, m_ctx = m[:b], m[b]
    per_b = lambda i: m_lat[:, i].reshape(b, 1, D)
    shift1 = jnp.stack([jnp.broadcast_to(m_ctx[0], (b, D)), m_lat[:, 0]], axis=1).reshape(b, 2, 1, D)
    scale1 = jnp.stack([jnp.broadcast_to(m_ctx[1], (b, D)), m_lat[:, 1]], axis=1).reshape(b, 2, 1, D)

    h_all = jnp.concatenate([ctx, x], axis=1)
    z = _inproj(h_all, row(norm1_w[0]), scale1, shift1, _reorder_cols(w_in[0]).astype(BF16))

    mu = rwkv_mu[0]
    mu_lora = jnp.concatenate([jnp.zeros((32,), F32), mu[3072:], jnp.zeros((128,), F32)]).reshape(1, 512)
    e = jnp.repeat(jnp.eye(R_HEADS, dtype=F32), R_N, axis=0)
    w2f = jnp.zeros((128, D), F32).at[L_WF:L_WF + R_W_LORA].set(rwkv_w2[0, 0])
    w2b = jnp.zeros((256, D), F32).at[L_WB:L_WB + R_W_LORA].set(rwkv_w2[0, 1])
    a2 = jnp.zeros((128, D), F32).at[L_A - 128:L_A - 128 + R_A_LORA].set(rwkv_a2[0])
    g2 = jnp.zeros((256, D), F32).at[L_G - 128:L_G - 128 + R_G_LORA].set(rwkv_g2[0])
    wa = jnp.zeros((2, 128, G_HEADS * G_DK), F32)
    wa = wa.at[0, 0:G_LORA].set(gla_w_a2[0, 0]).at[1, G_LORA:2 * G_LORA].set(gla_w_a2[0, 1])
    r, k2, vr, kk, kka, d_f, d_b, bonus, gate_r = _rprep(
        z, row(mu[:3072]), mu_lora, rwkv_w0[0].reshape(2, 1, D), w2f, w2b, row(rwkv_a0[0]), a2,
        g2, row(rwkv_k_k[0]), row(rwkv_k_a[0]), row(rwkv_r_k[0]), e, e.T)

    o_r = _rwkv(_to_chains(r), _to_chains(k2), _to_chains(vr), _to_chains(kk), _to_chains(kka),
                jnp.stack([_to_chains(d_f), _to_chains(d_b)]))
    or_f, or_b = _from_chains(o_r[0], b), _from_chains(o_r[1], b)

    og_f, og_b = _gla(z, wa, gla_b_a[0].reshape(2, 1, -1))

    h1, hm_b = _mix(og_f, og_b, z, or_f, or_b, bonus, gate_r, x, row(gla_norm_w[0]), row(rwkv_ln_w[0]),
                        row(rwkv_ln_b[0]), e, e.T, w_out[0].astype(BF16), per_b(2), row(norm2_w[0]),
                        per_b(3), per_b(4))

    n_tok = b * seq
    n_cnt, e1, rho, e2 = _route(hm_b.reshape(n_tok, D), peer_w_q[0].T.astype(BF16),
                                peer_sub_keys[0].reshape(2 * P_HEADS, P_KEYS, -1))
    f_t = _peer(hm_b.reshape(n_tok, D), peer_u[0].astype(BF16), peer_v[0].T.astype(BF16), n_cnt, e1, rho, e2)
    return _final(h1, f_t.T.reshape(b, seq, D), per_b(5), row(final_norm_w))
```

```python
import functools

import numpy as np
import jax
import jax.numpy as jnp
from jax import lax
from jax.experimental import pallas as pl
from jax.experimental.pallas import tpu as pltpu

F32 = jnp.float32
BF16 = jnp.bfloat16
HI = lax.Precision.HIGHEST

D = 1024
GRID_COLS = 64
N_CTX = 256
EPS = 1e-6
N_MOD = 6

G_HEADS = 4
G_DV = D // G_HEADS
G_DK = G_DV // 2
G_LORA = 16
G_GATE_NORM = 16.0
G_CHUNK = 64

R_N = 64
R_HEADS = D // R_N
R_LN_EPS = 64e-5
R_W_LORA = 64
R_A_LORA = 64
R_G_LORA = 160

P_HEADS = 8
P_KEYS = 128
P_TOPK = 16

C_RWKV = 0
C_GLA = 3072
C_GATE = 6144
C_LORA = 8192
N_PROJ = 8704
L_WF, L_WB, L_A, L_G = 32, 96, 160, 224

ROWS = 256
V7X_VMEM_BYTES = 64 * 1024 * 1024


def _cparams(sem, vmem_mb):
    return pltpu.CompilerParams(dimension_semantics=sem,
                                vmem_limit_bytes=min(vmem_mb * 1024 * 1024, V7X_VMEM_BYTES - (8 << 20)))


def _sigmoid(x):
    return 1.0 / (1.0 + jnp.exp(-x))


def _softplus(x):
    return jnp.maximum(x, 0.0) + jnp.log(1.0 + jnp.exp(-jnp.abs(x)))


def _dot(a, b, precision=None):
    return jnp.dot(a, b, preferred_element_type=F32, precision=precision)


def _dot_nt(a, b, precision=None):
    return lax.dot_general(a, b, (((1,), (1,)), ((), ())), preferred_element_type=F32, precision=precision)


def _dot_tn(a, b, precision=None):
    return lax.dot_general(a, b, (((0,), (0,)), ((), ())), preferred_element_type=F32, precision=precision)


def _mod_kernel(c_ref, w_ref, b_ref, o_ref):
    c = c_ref[...]
    s = c * _sigmoid(c)
    o_ref[...] = _dot(s.astype(BF16), w_ref[...].astype(BF16)) + b_ref[...]


def _mod(c16, w_mod, b_mod):
    n = w_mod.shape[1]
    tn = 768
    return pl.pallas_call(
        _mod_kernel,
        grid=(n // tn,),
        in_specs=[pl.BlockSpec((16, D), lambda j: (0, 0)),
                  pl.BlockSpec((D, tn), lambda j: (0, j)),
                  pl.BlockSpec((1, tn), lambda j: (0, j))],
        out_specs=pl.BlockSpec((16, tn), lambda j: (0, j)),
        out_shape=jax.ShapeDtypeStruct((16, n), F32),
        compiler_params=_cparams(("parallel",), 24),
        name="mod",
    )(c16, w_mod, b_mod)


INPROJ_ROWS = 768


def _inproj_kernel(h_ref, nw_ref, sc_ref, sh_ref, w_ref, o_ref, a_scr):
    @pl.when(pl.program_id(2) == 0)
    def _():
        x = h_ref[0]
        y = x * lax.rsqrt(jnp.mean(x * x, axis=-1, keepdims=True) + EPS) * nw_ref[...]
        t = pl.program_id(1) * INPROJ_ROWS + lax.broadcasted_iota(jnp.int32, (INPROJ_ROWS, 1), 0)
        is_ctx = t < N_CTX
        sc = jnp.where(is_ctx, sc_ref[0, 0], sc_ref[0, 1])
        sh = jnp.where(is_ctx, sh_ref[0, 0], sh_ref[0, 1])
        a_scr[...] = (y * (1.0 + sc) + sh).astype(BF16)

    o_ref[0] = _dot(a_scr[...], w_ref[...])


def _inproj(h_all, norm_w, scale, shift, w_p):
    b, t, _ = h_all.shape
    tn = 512
    mod = pl.BlockSpec((1, 2, 1, D), lambda i, j, n: (i, 0, 0, 0))
    return pl.pallas_call(
        _inproj_kernel,
        grid=(b, t // INPROJ_ROWS, N_PROJ // tn),
        in_specs=[pl.BlockSpec((1, INPROJ_ROWS, D), lambda i, j, n: (i, j, 0)),
                  pl.BlockSpec((1, D), lambda i, j, n: (0, 0)),
                  mod, mod,
                  pl.BlockSpec((D, tn), lambda i, j, n: (0, n))],
        out_specs=pl.BlockSpec((1, INPROJ_ROWS, tn), lambda i, j, n: (i, j, n)),
        out_shape=jax.ShapeDtypeStruct((b, t, N_PROJ), F32),
        scratch_shapes=[pltpu.VMEM((INPROJ_ROWS, D), BF16)],
        compiler_params=_cparams(("parallel", "parallel", "arbitrary"), 32),
        name="inproj",
    )(h_all, norm_w, scale, shift, w_p)


def _shift_mix(zc, zp, zn, mu, is_ctx):
    rows, w = zc.shape
    row = lax.broadcasted_iota(jnp.int32, (rows, w), 0)
    lane = lax.broadcasted_iota(jnp.int32, (rows, w), 1)
    pmask = jnp.where(is_ctx, 1, 3)
    rmask = jnp.where(is_ctx, rows - 1, GRID_COLS - 1)
    cls = lane & pmask
    rr = row & rmask
    prev = pltpu.roll(zc, 1, 0)
    nxt = pltpu.roll(zc, rows - 1, 0)
    up = jnp.concatenate([zp, zc[:rows - GRID_COLS]], axis=0)
    down = jnp.concatenate([zc[GRID_COLS:], zn], axis=0)
    zs = jnp.where((cls == 0) & (rr != 0), prev,
                   jnp.where((cls == 1) & (rr != rmask), nxt,
                             jnp.where(cls == 2, up, jnp.where(cls == 3, down, 0.0))))
    return zc + (zs - zc) * mu


def _rprep_kernel(zc_ref, zp_ref, zn_ref, lc_ref, lp_ref, ln_ref,
                  mu_ref, mul_ref, w0_ref, w2f_ref, w2b_ref, a0_ref, a2_ref, g2_ref,
                  kk_w_ref, ka_w_ref, rk_w_ref, e_ref, et_ref,
                  r_o, k_o, v_o, kk_o, kka_o, df_o, db_o, bonus_o, gate_o):
    tb = pl.program_id(1)
    ntb = pl.num_programs(1)
    is_ctx = tb == 0
    has_up = (tb > 1).astype(F32)
    has_down = (tb < ntb - 1).astype(F32)

    zl = _shift_mix(lc_ref[0], lp_ref[0] * has_up, ln_ref[0] * has_down, mul_ref[...], is_ctx)
    def decay(window, w2p_ref, d):
        w = -_softplus(-(w0_ref[d] + _dot(jnp.tanh(window), w2p_ref[...], HI))) - 0.5
        return jnp.exp(-jnp.exp(w))

    df_o[0] = decay(zl[:, 0:128], w2f_ref, 0)
    db_o[0] = decay(zl[:, 0:256], w2b_ref, 1)
    a = _sigmoid(a0_ref[...] + _dot(zl[:, 128:256], a2_ref[...], HI))
    gate_o[0] = _dot(_sigmoid(zl[:, 128:384]), g2_ref[...], HI)

    def piece(i):
        sl = slice(i * D, (i + 1) * D)
        return _shift_mix(zc_ref[0, :, sl], zp_ref[0, :, sl] * has_up, zn_ref[0, :, sl] * has_down,
                          mu_ref[:, sl], is_ctx)

    r = piece(0)
    kr = piece(1)
    vr = piece(2)
    e = e_ref[...]
    et = et_ref[...]
    kkraw = kr * kk_w_ref[...]
    nrm = jnp.sqrt(_dot(kkraw * kkraw, e, HI))
    inv = 1.0 / jnp.maximum(nrm, 1e-12)
    kk = kkraw * _dot(inv, et, HI)
    k2 = kr * (1.0 + (a - 1.0) * ka_w_ref[...])
    rk = _dot(r * k2 * rk_w_ref[...], e, HI)
    r_o[0] = r.astype(BF16)
    k_o[0] = k2.astype(BF16)
    v_o[0] = vr.astype(BF16)
    kk_o[0] = kk.astype(BF16)
    kka_o[0] = (kk * a).astype(BF16)
    bonus_o[0] = _dot(rk, et, HI) * vr


def _rprep(z, mu_main, mu_lora, w0, w2f, w2b, a0, a2, g2, k_k, k_a, r_k, e, et):
    b, t, _ = z.shape
    ntb = t // ROWS
    q = ROWS // GRID_COLS
    last64 = t // GRID_COLS - 1
    wm = 3 * D
    cm = C_RWKV // wm
    cl = C_LORA // 512
    cur = lambda c: (lambda i, j: (i, j, c))
    prv = lambda c: (lambda i, j: (i, jnp.maximum(j * q - 1, 0), c))
    nxt = lambda c: (lambda i, j: (i, jnp.minimum(j * q + q, last64), c))
    full = lambda *s: pl.BlockSpec(s, lambda i, j: (0,) * len(s))
    out_spec = pl.BlockSpec((1, ROWS, D), lambda i, j: (i, j, 0))
    out_sd = jax.ShapeDtypeStruct((b, t, D), F32)
    return pl.pallas_call(
        _rprep_kernel,
        grid=(b, ntb),
        in_specs=[pl.BlockSpec((1, ROWS, wm), cur(cm)),
                  pl.BlockSpec((1, GRID_COLS, wm), prv(cm)),
                  pl.BlockSpec((1, GRID_COLS, wm), nxt(cm)),
                  pl.BlockSpec((1, ROWS, 512), cur(cl)),
                  pl.BlockSpec((1, GRID_COLS, 512), prv(cl)),
                  pl.BlockSpec((1, GRID_COLS, 512), nxt(cl)),
                  full(1, wm), full(1, 512), full(2, 1, D), full(128, D), full(256, D),
                  full(1, D), full(128, D), full(256, D),
                  full(1, D), full(1, D), full(1, D), full(D, R_HEADS), full(R_HEADS, D)],
        out_specs=[out_spec] * 9,
        out_shape=[jax.ShapeDtypeStruct((b, t, D), BF16)] * 5 + [out_sd] * 4,
        compiler_params=_cparams(("parallel", "parallel"), 52),
        name="rprep",
    )(z, z, z, z, z, z, mu_main, mu_lora, w0, w2f, w2b, a0, a2, g2, k_k, k_a, r_k, e, et)


R_STEPS = 64


def _rwkv_kernel(r_ref, k_ref, v_ref, kk_ref, kka_ref, d_ref, o_ref, s_ref):
    dirn = pl.program_id(0)

    @pl.when(pl.program_id(1) == 0)
    def _():
        s_ref[...] = jnp.zeros_like(s_ref)

    def step(i, carry):
        t = jnp.where(dirn == 0, i, R_STEPS - 1 - i)
        kk_t = kk_ref[t].astype(F32)
        sa = s_ref[0] * kk_t[0:1, :]
        for k in range(1, R_N):
            sa = sa + s_ref[k] * kk_t[k:k + 1, :]
        d_t = d_ref[0, t]
        kka_t = kka_ref[t].astype(F32)
        k_t = k_ref[t].astype(F32)
        r_t = r_ref[t].astype(F32)
        vv = v_ref[t].astype(F32)
        out = None
        for k in range(R_N):
            sk = s_ref[k] * d_t[k:k + 1, :] - sa * kka_t[k:k + 1, :] + vv * k_t[k:k + 1, :]
            s_ref[k] = sk
            term = sk * r_t[k:k + 1, :]
            out = term if out is None else out + term
        o_ref[0, t] = out.astype(BF16)
        return carry

    lax.fori_loop(0, R_STEPS, step, 0)


def _rwkv_order(dirn, nb, n_ctx_blk, n_blk):
    back = jnp.where(nb < n_ctx_blk, n_ctx_blk - 1 - nb, n_blk + n_ctx_blk - 1 - nb)
    return jnp.where(dirn == 0, nb, back)


def _rwkv(r_t, k_t, v_t, kk_t, kka_t, d_t):
    t, _, nch = r_t.shape
    n_blk = t // R_STEPS
    n_ctx_blk = N_CTX // R_STEPS
    blk = lambda d, n: (_rwkv_order(d, n, n_ctx_blk, n_blk), 0, 0)
    blk4 = lambda d, n: (d, _rwkv_order(d, n, n_ctx_blk, n_blk), 0, 0)
    spec = pl.BlockSpec((R_STEPS, R_N, nch), blk)
    spec4 = pl.BlockSpec((1, R_STEPS, R_N, nch), blk4)
    return pl.pallas_call(
        _rwkv_kernel,
        grid=(2, n_blk),
        in_specs=[spec, spec, spec, spec, spec, spec4],
        out_specs=spec4,
        out_shape=jax.ShapeDtypeStruct((2, t, R_N, nch), BF16),
        scratch_shapes=[pltpu.VMEM((R_N, R_N, nch), F32)],
        compiler_params=_cparams(("arbitrary", "arbitrary"), 48),
        name="rwkv",
    )(r_t, k_t, v_t, kk_t, kka_t, d_t)


def _log_sigmoid(x):
    return jnp.minimum(x, 0.0) - jnp.log(1.0 + jnp.exp(-jnp.abs(x)))


def _gla_dir(q, k, v, alo, wa, ba, st_ref, reverse):
    c = q.shape[0]
    g = _log_sigmoid(_dot(alo, wa, HI) + ba) * (1.0 / G_GATE_NORM)
    row = lax.broadcasted_iota(jnp.int32, (c, c), 0)
    col = lax.broadcasted_iota(jnp.int32, (c, c), 1)
    tri = (row <= col) if reverse else (row >= col)
    bcum = _dot(tri.astype(F32), g, HI)
    blast = bcum[0:1] if reverse else bcum[c - 1:c]
    qd = (q * (G_DK ** -0.5) * jnp.exp(bcum)).astype(BF16)
    kd = (k * jnp.exp(-bcum)).astype(BF16)
    kl = (k * jnp.exp(blast - bcum)).astype(BF16)
    eb = jnp.exp(blast)
    vb = v.astype(BF16)
    outs = []
    for h in range(G_HEADS):
        sk = slice(h * G_DK, (h + 1) * G_DK)
        sv = slice(h * G_DV, (h + 1) * G_DV)
        att = jnp.where(tri, _dot_nt(qd[:, sk], kd[:, sk]), 0.0)
        st = st_ref[h]
        outs.append(_dot_nt(qd[:, sk], st.astype(BF16)) + _dot(att.astype(BF16), vb[:, sv]))
        st_ref[h] = st * eb[:, sk] + _dot_tn(vb[:, sv], kl[:, sk])
    return jnp.concatenate(outs, axis=-1)


def _gla_kernel(qf, kf, vf, lf, qb, kb, vb, lb, wa_ref, ba_ref, of_ref, ob_ref, sf_ref, sb_ref):
    @pl.when(pl.program_id(1) == 0)
    def _():
        sf_ref[...] = jnp.zeros_like(sf_ref)
        sb_ref[...] = jnp.zeros_like(sb_ref)

    of_ref[0] = _gla_dir(qf[0], kf[0], vf[0], lf[0], wa_ref[0], ba_ref[0], sf_ref, False)
    ob_ref[0] = _gla_dir(qb[0], kb[0], vb[0], lb[0], wa_ref[1], ba_ref[1], sb_ref, True)


def _gla(z, w_a2, b_a):
    b, t, _ = z.shape
    n_blk = t // G_CHUNK
    n_ctx_blk = N_CTX // G_CHUNK
    fwd = lambda n: n
    bwd = lambda n: _rwkv_order(1, n, n_ctx_blk, n_blk)
    wk = G_HEADS * G_DK
    cq, ck, cv = C_GLA // wk, (C_GLA + wk) // wk, (C_GLA + 2 * wk) // D
    cl = C_LORA // 128

    def specs(order):
        return [pl.BlockSpec((1, G_CHUNK, wk), lambda i, n: (i, order(n), cq)),
                pl.BlockSpec((1, G_CHUNK, wk), lambda i, n: (i, order(n), ck)),
                pl.BlockSpec((1, G_CHUNK, D), lambda i, n: (i, order(n), cv)),
                pl.BlockSpec((1, G_CHUNK, 128), lambda i, n: (i, order(n), cl))]

    out_sd = jax.ShapeDtypeStruct((b, t, D), F32)
    state = pltpu.VMEM((G_HEADS, G_DV, G_DK), F32)
    return pl.pallas_call(
        _gla_kernel,
        grid=(b, n_blk),
        in_specs=specs(fwd) + specs(bwd) + [
            pl.BlockSpec((2, 128, wk), lambda i, n: (0, 0, 0)),
            pl.BlockSpec((2, 1, wk), lambda i, n: (0, 0, 0))],
        out_specs=[pl.BlockSpec((1, G_CHUNK, D), lambda i, n: (i, fwd(n), 0)),
                   pl.BlockSpec((1, G_CHUNK, D), lambda i, n: (i, bwd(n), 0))],
        out_shape=[out_sd, out_sd],
        scratch_shapes=[state, state],
        compiler_params=_cparams(("parallel", "arbitrary"), 24),
        name="gla",
    )(z, z, z, z, z, z, z, z, w_a2, b_a)


def _mix_kernel(ogf_ref, ogb_ref, gout_ref, orf_ref, orb_ref, bonus_ref, gr_ref, gg_ref, grw_ref, x_ref,
                gnw_ref, lnw_ref, lnb_ref, e_ref, et_ref, wout_ref, m2_ref, n2w_ref, m3_ref, m4_ref,
                h1_o, hmb_o):
    og = ogf_ref[0] + ogb_ref[0]
    parts = []
    for h in range(G_HEADS):
        seg = og[:, h * G_DV:(h + 1) * G_DV]
        parts.append(seg * lax.rsqrt(jnp.mean(seg * seg, axis=-1, keepdims=True) + EPS))
    gout = gout_ref[0]
    y_gla = jnp.concatenate(parts, axis=-1) * gnw_ref[...] * (gout * _sigmoid(gout))

    e = e_ref[...]
    et = et_ref[...]
    o_r = orf_ref[0].astype(F32) + orb_ref[0].astype(F32)
    mu =_dot(_dot(o_r, e, HI) * (1.0 / R_N), et, HI)
    dlt = o_r - mu
    var = _dot(dlt * dlt, e, HI) * (1.0 / R_N)
    o_r = dlt * _dot(lax.rsqrt(var + R_LN_EPS), et, HI) * lnw_ref[...] + lnb_ref[...] + bonus_ref[0]
    y_rwkv = o_r * gr_ref[0]

    y = _sigmoid(gg_ref[0]) * y_gla + _sigmoid(grw_ref[0]) * y_rwkv
    h1 = x_ref[0] + m2_ref[0] * _dot(y.astype(BF16), wout_ref[...])
    h1_o[0] = h1
    hn = h1 * lax.rsqrt(jnp.mean(h1 * h1, axis=-1, keepdims=True) + EPS) * n2w_ref[...]
    hmb_o[0] = (hn * (1.0 + m4_ref[0]) + m3_ref[0]).astype(BF16)


def _mix(og_f, og_b, z, or_f, or_b, bonus, gate_r, x, gnw, lnw, lnb, e, et, w_out, m2, n2w, m3, m4):
    b, tl, _ = x.shape
    off = N_CTX // ROWS
    seq = lambda c: pl.BlockSpec((1, ROWS, D), lambda i, j: (i, j + off, c))
    full = lambda *s: pl.BlockSpec(s, lambda i, j: (0,) * len(s))
    per_b = pl.BlockSpec((1, 1, D), lambda i, j: (i, 0, 0))
    lat = pl.BlockSpec((1, ROWS, D), lambda i, j: (i, j, 0))
    return pl.pallas_call(
        _mix_kernel,
        grid=(b, tl // ROWS),
        in_specs=[seq(0), seq(0), seq((C_GLA + 2048) // D), seq(0), seq(0), seq(0), seq(0),
                  seq(C_GATE // D), seq(C_GATE // D + 1), lat,
                  full(1, D), full(1, D), full(1, D), full(D, R_HEADS), full(R_HEADS, D), full(D, D),
                  per_b, full(1, D), per_b, per_b],
        out_specs=[lat, lat],
        out_shape=[jax.ShapeDtypeStruct((b, tl, D), F32), jax.ShapeDtypeStruct((b, tl, D), BF16)],
        compiler_params=_cparams(("parallel", "parallel"), 52),
        name="mix",
    )(og_f, og_b, z, or_f, or_b, bonus, gate_r, z, z, x, gnw, lnw, lnb, e, et, w_out, m2, n2w, m3, m4)


ROUTE_TOK = 256


def _top_values(work, top_ref, n):
    def body(i, w):
        m = jnp.max(w, axis=0, keepdims=True)
        top_ref[pl.ds(i, 1), :] = m
        return jnp.where(w == m, -jnp.inf, w)
    lax.fori_loop(0, n, body, work)


N_CAND = 80


def _dup_bf16(x):
    hi = lax.bitcast_convert_type(x.astype(BF16).astype(F32), jnp.uint32)
    return hi | (hi >> 16)


def _route_kernel(x_ref, wq_ref, keys_ref, n_o, e1_o, rho_o, e2_o, q_scr, t1_scr, t2_scr, cand_scr):
    q_scr[...] = _dot_nt(wq_ref[...], x_ref[...])
    ninf = -jnp.inf
    half = P_TOPK // 2

    def head(h, carry):
        q1 = q_scr[pl.ds(pl.multiple_of(h * 2 * P_KEYS, P_KEYS), P_KEYS), :]
        q2 = q_scr[pl.ds(pl.multiple_of(h * 2 * P_KEYS + P_KEYS, P_KEYS), P_KEYS), :]
        s1 = _dot(keys_ref[2 * h], q1, HI)
        s2 = _dot(keys_ref[2 * h + 1], q2, HI)
        _top_values(s1, t1_scr, P_TOPK)

        def rank(i, st):
            w, rho = st
            m = jnp.max(w, axis=0, keepdims=True)
            t2_scr[pl.ds(i, 1), :] = m
            top = w == m
            return jnp.where(top, ninf, w), jnp.where(top, lax.convert_element_type(i, F32), rho)

        _, rho = lax.fori_loop(0, P_TOPK, rank, (s2, jnp.full(s2.shape, float(P_TOPK), F32)))
        t1 = t1_scr[...]
        t2 = t2_scr[...]
        cand_scr[0:P_TOPK] = t1[0:1] + t2
        for i in range(1, half):
            cand_scr[P_TOPK + half * (i - 1):P_TOPK + half * i] = t1[i:i + 1] + t2[0:half]
        cand_scr[N_CAND - half:N_CAND] = t1[half:P_TOPK] + t2[0:1]
        cmax = t1[0:1] + t2[0:1]

        def pick(i, st):
            w, z, _ = st
            m = jnp.max(w, axis=0, keepdims=True)
            return jnp.where(w == m, ninf, w), z + jnp.exp(m - cmax), m

        _, zsum, tau = lax.fori_loop(0, P_TOPK, pick, (cand_scr[...], jnp.zeros_like(cmax), cmax))

        def count(j, acc):
            return acc + jnp.where(s1 + t2_scr[pl.ds(j, 1), :] >= tau, 1.0, 0.0)

        cnt = lax.fori_loop(0, P_TOPK, count, jnp.zeros_like(s1))
        sel1 = s1 >= t1[P_TOPK - 1:P_TOPK]
        sel2 = s2 >= t2[P_TOPK - 1:P_TOPK]
        n_o[h] = _dup_bf16(jnp.where(sel1, cnt, 0.0))
        e1_o[h] = _dup_bf16(jnp.where(sel1, jnp.exp(s1 - t1[0:1]), 0.0))
        rho_o[h] = rho.astype(BF16)
        e2_o[h] = jnp.where(sel2, jnp.exp(s2 - t2[0:1]) / zsum, 0.0).astype(BF16)
        return carry

    lax.fori_loop(0, P_HEADS, head, 0)


def _route(hm_b, wq_t, keys):
    n = hm_b.shape[0]
    tn = ROUTE_TOK
    big = pl.BlockSpec((P_HEADS, P_KEYS, tn), lambda i: (0, 0, i))
    sd = lambda dt: jax.ShapeDtypeStruct((P_HEADS, P_KEYS, n), dt)
    return pl.pallas_call(
        _route_kernel,
        grid=(n // tn,),
        in_specs=[pl.BlockSpec((tn, D), lambda i: (i, 0)),
                  pl.BlockSpec((2 * P_HEADS * P_KEYS, D), lambda i: (0, 0)),
                  pl.BlockSpec((2 * P_HEADS, P_KEYS, P_KEYS), lambda i: (0, 0, 0))],
        out_specs=[big, big, big, big],
        out_shape=[sd(jnp.uint32), sd(jnp.uint32), sd(BF16), sd(BF16)],
        scratch_shapes=[pltpu.VMEM((2 * P_HEADS * P_KEYS, tn), F32),
                        pltpu.VMEM((P_TOPK, tn), F32), pltpu.VMEM((P_TOPK, tn), F32),
                        pltpu.VMEM((N_CAND, tn), F32)],
        compiler_params=_cparams(("parallel",), 48),
        name="route",
    )(hm_b, wq_t, keys)


PEER_TOK = 512
PEER_EXP = 1024


def _gelu(x):
    return 0.5 * x * (1.0 + lax.erf(x * (2.0 ** -0.5)))


BF16_ROWS = 16


def _peer_kernel(x_ref, u_ref, vt_ref, n_ref, e1_ref, rho_ref, e2_ref, o_ref):
    j = pl.program_id(1)

    @pl.when(j == 0)
    def _():
        o_ref[...] = jnp.zeros_like(o_ref)

    tn = x_ref.shape[0]
    x = x_ref[...]
    n_a = PEER_EXP // P_KEYS
    groups = P_KEYS // BF16_ROWS
    halves = 2
    a_per_half = n_a // halves
    p_halves = []
    for half in range(halves):
        rows_h = a_per_half * P_KEYS
        act = _dot_nt(u_ref[half * rows_h:(half + 1) * rows_h, :], x)
        gates = []
        for al in range(half * a_per_half, (half + 1) * a_per_half):
            g = [None] * groups
            for h in range(P_HEADS):
                row = lambda ref: pltpu.bitcast(jnp.broadcast_to(ref[h, al:al + 1, :], (8, tn)), BF16)
                n_row = row(n_ref)
                e1_row = row(e1_ref)
                for q in range(groups):
                    rows = slice(q * BF16_ROWS, (q + 1) * BF16_ROWS)
                    term = jnp.where(rho_ref[h, rows, :] < n_row, e2_ref[h, rows, :], 0.0) * e1_row
                    g[q] = term if g[q] is None else g[q] + term
            gates.extend(g)
        p_halves.append(_gelu(act).astype(BF16) * jnp.concatenate(gates, axis=0))
    o_ref[...] += _dot(vt_ref[...], jnp.concatenate(p_halves, axis=0))


def _peer(hm_b, u_b, vt_b, n_cnt, e1, rho, e2):
    n = hm_b.shape[0]
    n_exp = u_b.shape[0]
    tn, te = PEER_TOK, PEER_EXP
    big = pl.BlockSpec((P_HEADS, P_KEYS, tn), lambda i, j: (0, 0, i))
    a_rows = pl.BlockSpec((P_HEADS, te // P_KEYS, tn), lambda i, j: (0, j, i))
    return pl.pallas_call(
        _peer_kernel,
        grid=(n // tn, n_exp // te),
        in_specs=[pl.BlockSpec((tn, D), lambda i, j: (i, 0)),
                  pl.BlockSpec((te, D), lambda i, j: (j, 0)),
                  pl.BlockSpec((D, te), lambda i, j: (0, j)),
                  a_rows, a_rows, big, big],
        out_specs=pl.BlockSpec((D, tn), lambda i, j: (0, i)),
        out_shape=jax.ShapeDtypeStruct((D, n), F32),
        compiler_params=_cparams(("parallel", "arbitrary"), 52),
        name="peer",
    )(hm_b, u_b, vt_b, n_cnt, e1, rho, e2)


def _final_kernel(h1_ref, f_ref, m5_ref, w_ref, o_ref):
    h = h1_ref[0] + m5_ref[0] * f_ref[0]
    o_ref[0] = h * lax.rsqrt(jnp.mean(h * h, axis=-1, keepdims=True) + EPS) * w_ref[...]


def _final(h1, f, m5, w):
    b, tl, _ = h1.shape
    lat = pl.BlockSpec((1, ROWS, D), lambda i, j: (i, j, 0))
    return pl.pallas_call(
        _final_kernel,
        grid=(b, tl // ROWS),
        in_specs=[lat, lat, pl.BlockSpec((1, 1, D), lambda i, j: (i, 0, 0)),
                  pl.BlockSpec((1, D), lambda i, j: (0, 0))],
        out_specs=lat,
        out_shape=jax.ShapeDtypeStruct((b, tl, D), F32),
        compiler_params=_cparams(("parallel", "parallel"), 24),
        name="final",
    )(h1, f, m5, w)


def _reorder_cols(w):
    gla_main, gla_lora = w[..., 0:3072], w[..., 3072:3104]
    rw_main, rw_lora = w[..., 3104:6176], w[..., 6176:6528]
    gates = w[..., 6528:8576]
    pad = jnp.zeros(w.shape[:-1] + (512 - 32 - 352,), w.dtype)
    return jnp.concatenate([rw_main, gla_main, gates, gla_lora, rw_lora, pad], axis=-1)


def _to_chains(a):
    b, t, _ = a.shape
    return a.reshape(b, t, R_HEADS, R_N).transpose(1, 3, 0, 2).reshape(t, R_N, b * R_HEADS)


def _from_chains(a, b):
    t = a.shape[0]
    return a.reshape(t, R_N, b, R_HEADS).transpose(2, 0, 3, 1).reshape(b, t, D)


def kernel(x, c, ctx, c_ctx, norm1_w, w_mod, b_mod, w_in, gla_w_a2, gla_b_a, gla_norm_w, rwkv_mu, rwkv_w0, rwkv_w2, rwkv_a0, rwkv_a2, rwkv_g2, rwkv_k_k, rwkv_k_a, rwkv_r_k, rwkv_ln_w, rwkv_ln_b, w_out, norm2_w, peer_w_q, peer_sub_keys, peer_u, peer_v, final_norm_w):
    b, seq, _ = x.shape
    assert w_in.shape[0] == 1 and ctx.shape[1] == N_CTX and seq % ROWS == 0
    row = lambda v: v.reshape(1, -1)

    c16 = jnp.zeros((16, D), F32).at[:b].set(c).at[b].set(c_ctx)
    m = _mod(c16, w_mod[0], row(b_mod[0])).reshape(16, N_MOD, D)
    m_lat, m_ctx = m[:b], m[b]
    per_b = lambda i: m_lat[:, i].reshape(b, 1, D)
    shift1 = jnp.stack([jnp.broadcast_to(m_ctx[0], (b, D)), m_lat[:, 0]], axis=1).reshape(b, 2, 1, D)
    scale1 = jnp.stack([jnp.broadcast_to(m_ctx[1], (b, D)), m_lat[:, 1]], axis=1).reshape(b, 2, 1, D)

    h_all = jnp.concatenate([ctx, x], axis=1)
    z = _inproj(h_all, row(norm1_w[0]), scale1, shift1, _reorder_cols(w_in[0]).astype(BF16))

    mu = rwkv_mu[0]
    mu_lora = jnp.concatenate([jnp.zeros((32,), F32), mu[3072:], jnp.zeros((128,), F32)]).reshape(1, 512)
    e = jnp.repeat(jnp.eye(R_HEADS, dtype=F32), R_N, axis=0)
    w2f = jnp.zeros((128, D), F32).at[L_WF:L_WF + R_W_LORA].set(rwkv_w2[0, 0])
    w2b = jnp.zeros((256, D), F32).at[L_WB:L_WB + R_W_LORA].set(rwkv_w2[0, 1])
    a2 = jnp.zeros((128, D), F32).at[L_A - 128:L_A - 128 + R_A_LORA].set(rwkv_a2[0])
    g2 = jnp.zeros((256, D), F32).at[L_G - 128:L_G - 128 + R_G_LORA].set(rwkv_g2[0])
    wa = jnp.zeros((2, 128, G_HEADS * G_DK), F32)
    wa = wa.at[0, 0:G_LORA].set(gla_w_a2[0, 0]).at[1, G_LORA:2 * G_LORA].set(gla_w_a2[0, 1])
    r, k2, vr, kk, kka, d_f, d_b, bonus, gate_r = _rprep(
        z, row(mu[:3072]), mu_lora, rwkv_w0[0].reshape(2, 1, D), w2f, w2b, row(rwkv_a0[0]), a2,
        g2, row(rwkv_k_k[0]), row(rwkv_k_a[0]), row(rwkv_r_k[0]), e, e.T)

    o_r = _rwkv(_to_chains(r), _to_chains(k2), _to_chains(vr), _to_chains(kk), _to_chains(kka),
                jnp.stack([_to_chains(d_f), _to_chains(d_b)]))
    or_f, or_b = _from_chains(o_r[0], b), _from_chains(o_r[1], b)

    og_f, og_b = _gla(z, wa, gla_b_a[0].reshape(2, 1, -1))

    h1, hm_b = _mix(og_f, og_b, z, or_f, or_b, bonus, gate_r, x, row(gla_norm_w[0]), row(rwkv_ln_w[0]),
                        row(rwkv_ln_b[0]), e, e.T, w_out[0].astype(BF16), per_b(2), row(norm2_w[0]),
                        per_b(3), per_b(4))

    n_tok = b * seq
    n_cnt, e1, rho, e2 = _route(hm_b.reshape(n_tok, D), peer_w_q[0].T.astype(BF16),
                                peer_sub_keys[0].reshape(2 * P_HEADS, P_KEYS, -1))
    f_t = _peer(hm_b.reshape(n_tok, D), peer_u[0].astype(BF16), peer_v[0].T.astype(BF16), n_cnt, e1, rho, e2)
    return _final(h1, f_t.T.reshape(b, seq, D), per_b(5), row(final_norm_w))
```

```python
import functools

import numpy as np
import jax
import jax.numpy as jnp
from jax import lax
from jax.experimental import pallas as pl
from jax.experimental.pallas import tpu as pltpu

F32 = jnp.float32
BF16 = jnp.bfloat16
HI = lax.Precision.HIGHEST

D = 1024
GRID_COLS = 64
N_CTX = 256
EPS = 1e-6
N_MOD = 6

G_HEADS = 4
G_DV = D // G_HEADS
G_DK = G_DV // 2
G_LORA = 16
G_GATE_NORM = 16.0
G_CHUNK = 64

R_N = 64
R_HEADS = D // R_N
R_LN_EPS = 64e-5
R_W_LORA = 64
R_A_LORA = 64
R_G_LORA = 160

P_HEADS = 8
P_KEYS = 128
P_TOPK = 16

C_RWKV = 0
C_GLA = 3072
C_GATE = 6144
C_LORA = 8192
N_PROJ = 8704
L_WF, L_WB, L_A, L_G = 32, 96, 160, 224

ROWS = 256
V7X_VMEM_BYTES = 64 * 1024 * 1024


def _cparams(sem, vmem_mb):
    return pltpu.CompilerParams(dimension_semantics=sem,
                                vmem_limit_bytes=min(vmem_mb * 1024 * 1024, V7X_VMEM_BYTES - (8 << 20)))


def _sigmoid(x):
    return 1.0 / (1.0 + jnp.exp(-x))


def _softplus(x):
    return jnp.maximum(x, 0.0) + jnp.log(1.0 + jnp.exp(-jnp.abs(x)))


def _dot(a, b, precision=None):
    return jnp.dot(a, b, preferred_element_type=F32, precision=precision)


def _dot_nt(a, b, precision=None):
    return lax.dot_general(a, b, (((1,), (1,)), ((), ())), preferred_element_type=F32, precision=precision)


def _dot_tn(a, b, precision=None):
    return lax.dot_general(a, b, (((0,), (0,)), ((), ())), preferred_element_type=F32, precision=precision)


def _mod_kernel(c_ref, w_ref, b_ref, o_ref):
    c = c_ref[...]
    s = c * _sigmoid(c)
    o_ref[...] = _dot(s.astype(BF16), w_ref[...].astype(BF16)) + b_ref[...]


def _mod(c16, w_mod, b_mod):
    n = w_mod.shape[1]
    tn = 768
    return pl.pallas_call(
        _mod_kernel,
        grid=(n // tn,),
        in_specs=[pl.BlockSpec((16, D), lambda j: (0, 0)),
                  pl.BlockSpec((D, tn), lambda j: (0, j)),
                  pl.BlockSpec((1, tn), lambda j: (0, j))],
        out_specs=pl.BlockSpec((16, tn), lambda j: (0, j)),
        out_shape=jax.ShapeDtypeStruct((16, n), F32),
        compiler_params=_cparams(("parallel",), 24),
        name="mod",
    )(c16, w_mod, b_mod)


INPROJ_ROWS = 768


def _inproj_kernel(h_ref, nw_ref, sc_ref, sh_ref, w_ref, o_ref, a_scr):
    @pl.when(pl.program_id(2) == 0)
    def _():
        x = h_ref[0]
        y = x * lax.rsqrt(jnp.mean(x * x, axis=-1, keepdims=True) + EPS) * nw_ref[...]
        t = pl.program_id(1) * INPROJ_ROWS + lax.broadcasted_iota(jnp.int32, (INPROJ_ROWS, 1), 0)
        is_ctx = t < N_CTX
        sc = jnp.where(is_ctx, sc_ref[0, 0], sc_ref[0, 1])
        sh = jnp.where(is_ctx, sh_ref[0, 0], sh_ref[0, 1])
        a_scr[...] = (y * (1.0 + sc) + sh).astype(BF16)

    o_ref[0] = _dot(a_scr[...], w_ref[...])


def _inproj(h_all, norm_w, scale, shift, w_p):
    b, t, _ = h_all.shape
    tn = 512
    mod = pl.BlockSpec((1, 2, 1, D), lambda i, j, n: (i, 0, 0, 0))
    return pl.pallas_call(
        _inproj_kernel,
        grid=(b, t // INPROJ_ROWS, N_PROJ // tn),
        in_specs=[pl.BlockSpec((1, INPROJ_ROWS, D), lambda i, j, n: (i, j, 0)),
                  pl.BlockSpec((1, D), lambda i, j, n: (0, 0)),
                  mod, mod,
                  pl.BlockSpec((D, tn), lambda i, j, n: (0, n))],
        out_specs=pl.BlockSpec((1, INPROJ_ROWS, tn), lambda i, j, n: (i, j, n)),
        out_shape=jax.ShapeDtypeStruct((b, t, N_PROJ), F32),
        scratch_shapes=[pltpu.VMEM((INPROJ_ROWS, D), BF16)],
        compiler_params=_cparams(("parallel", "parallel", "arbitrary"), 32),
        name="inproj",
    )(h_all, norm_w, scale, shift, w_p)


def _shift_mix(zc, zp, zn, mu, is_ctx):
    rows, w = zc.shape
    row = lax.broadcasted_iota(jnp.int32, (rows, w), 0)
    lane = lax.broadcasted_iota(jnp.int32, (rows, w), 1)
    pmask = jnp.where(is_ctx, 1, 3)
    rmask = jnp.where(is_ctx, rows - 1, GRID_COLS - 1)
    cls = lane & pmask
    rr = row & rmask
    prev = pltpu.roll(zc, 1, 0)
    nxt = pltpu.roll(zc, rows - 1, 0)
    up = jnp.concatenate([zp, zc[:rows - GRID_COLS]], axis=0)
    down = jnp.concatenate([zc[GRID_COLS:], zn], axis=0)
    zs = jnp.where((cls == 0) & (rr != 0), prev,
                   jnp.where((cls == 1) & (rr != rmask), nxt,
                             jnp.where(cls == 2, up, jnp.where(cls == 3, down, 0.0))))
    return zc + (zs - zc) * mu


def _rprep_kernel(zc_ref, zp_ref, zn_ref, lc_ref, lp_ref, ln_ref,
                  mu_ref, mul_ref, w0_ref, w2f_ref, w2b_ref, a0_ref, a2_ref, g2_ref,
                  kk_w_ref, ka_w_ref, rk_w_ref, e_ref, et_ref,
                  r_o, k_o, v_o, kk_o, kka_o, df_o, db_o, bonus_o, gate_o):
    tb = pl.program_id(1)
    ntb = pl.num_programs(1)
    is_ctx = tb == 0
    has_up = (tb > 1).astype(F32)
    has_down = (tb < ntb - 1).astype(F32)

    zl = _shift_mix(lc_ref[0], lp_ref[0] * has_up, ln_ref[0] * has_down, mul_ref[...], is_ctx)
    def decay(window, w2p_ref, d):
        w = -_softplus(-(w0_ref[d] + _dot(jnp.tanh(window), w2p_ref[...], HI))) - 0.5
        return -jnp.exp(w)

    df_o[0] = decay(zl[:, 0:128], w2f_ref, 0)
    db_o[0] = decay(zl[:, 0:256], w2b_ref, 1)
    a = _sigmoid(a0_ref[...] + _dot(zl[:, 128:256], a2_ref[...], HI))
    gate_o[0] = _dot(_sigmoid(zl[:, 128:384]), g2_ref[...], HI)

    def piece(i):
        sl = slice(i * D, (i + 1) * D)
        return _shift_mix(zc_ref[0, :, sl], zp_ref[0, :, sl] * has_up, zn_ref[0, :, sl] * has_down,
                          mu_ref[:, sl], is_ctx)

    r = piece(0)
    kr = piece(1)
    vr = piece(2)
    e = e_ref[...]
    et = et_ref[...]
    kkraw = kr * kk_w_ref[...]
    nrm = jnp.sqrt(_dot(kkraw * kkraw, e, HI))
    inv = 1.0 / jnp.maximum(nrm, 1e-12)
    kk = kkraw * _dot(inv, et, HI)
    k2 = kr * (1.0 + (a - 1.0) * ka_w_ref[...])
    rk = _dot(r * k2 * rk_w_ref[...], e, HI)
    r_o[0] = r.astype(BF16)
    k_o[0] = k2.astype(BF16)
    v_o[0] = vr.astype(BF16)
    kk_o[0] = kk.astype(BF16)
    kka_o[0] = (kk * a).astype(BF16)
    bonus_o[0] = _dot(rk, et, HI) * vr


def _rprep(z, mu_main, mu_lora, w0, w2f, w2b, a0, a2, g2, k_k, k_a, r_k, e, et):
    b, t, _ = z.shape
    ntb = t // ROWS
    q = ROWS // GRID_COLS
    last64 = t // GRID_COLS - 1
    wm = 3 * D
    cm = C_RWKV // wm
    cl = C_LORA // 512
    cur = lambda c: (lambda i, j: (i, j, c))
    prv = lambda c: (lambda i, j: (i, jnp.maximum(j * q - 1, 0), c))
    nxt = lambda c: (lambda i, j: (i, jnp.minimum(j * q + q, last64), c))
    full = lambda *s: pl.BlockSpec(s, lambda i, j: (0,) * len(s))
    out_spec = pl.BlockSpec((1, ROWS, D), lambda i, j: (i, j, 0))
    out_sd = jax.ShapeDtypeStruct((b, t, D), F32)
    return pl.pallas_call(
        _rprep_kernel,
        grid=(b, ntb),
        in_specs=[pl.BlockSpec((1, ROWS, wm), cur(cm)),
                  pl.BlockSpec((1, GRID_COLS, wm), prv(cm)),
                  pl.BlockSpec((1, GRID_COLS, wm), nxt(cm)),
                  pl.BlockSpec((1, ROWS, 512), cur(cl)),
                  pl.BlockSpec((1, GRID_COLS, 512), prv(cl)),
                  pl.BlockSpec((1, GRID_COLS, 512), nxt(cl)),
                  full(1, wm), full(1, 512), full(2, 1, D), full(128, D), full(256, D),
                  full(1, D), full(128, D), full(256, D),
                  full(1, D), full(1, D), full(1, D), full(D, R_HEADS), full(R_HEADS, D)],
        out_specs=[out_spec] * 9,
        out_shape=[jax.ShapeDtypeStruct((b, t, D), BF16)] * 5 + [out_sd] * 4,
        compiler_params=_cparams(("parallel", "parallel"), 52),
        name="rprep",
    )(z, z, z, z, z, z, mu_main, mu_lora, w0, w2f, w2b, a0, a2, g2, k_k, k_a, r_k, e, et)


R_STEPS = 64


def _rwkv_kernel(r_ref, k_ref, v_ref, kk_ref, kka_ref, lw_ref, o_ref, s_ref, rows_ref):
    dirn = pl.program_id(0)
    tix = lambda i: jnp.where(dirn == 0, i, R_STEPS - 1 - i)

    @pl.when(pl.program_id(1) == 0)
    def _():
        s_ref[...] = jnp.zeros_like(s_ref)

    def weighted_sum(rows):
        acc = s_ref[0] * rows_ref[rows, 0:1, :]
        for k in range(1, R_N):
            acc = acc + s_ref[k] * rows_ref[rows, k:k + 1, :]
        return acc

    rows_ref[0] = kk_ref[tix(0)].astype(F32)
    sa0 = weighted_sum(0)

    def step(i, carry):
        sa, lcum = carry
        t = tix(i)
        t_next = tix(jnp.minimum(i + 1, R_STEPS - 1))
        lcum = lcum + lw_ref[0, t]
        e_pos = jnp.exp(lcum)
        e_neg = jnp.exp(-lcum)
        rows_ref[0] = kk_ref[t_next].astype(F32) * e_pos
        rows_ref[1] = kka_ref[t].astype(F32) * e_neg
        rows_ref[2] = k_ref[t].astype(F32) * e_neg
        rows_ref[3] = r_ref[t].astype(F32) * e_pos
        vv = v_ref[t].astype(F32)
        out = None
        sa_next = None
        for k in range(R_N):
            sk = s_ref[k] - sa * rows_ref[1, k:k + 1, :] + vv * rows_ref[2, k:k + 1, :]
            s_ref[k] = sk
            o_term = sk * rows_ref[3, k:k + 1, :]
            s_term = sk * rows_ref[0, k:k + 1, :]
            out = o_term if out is None else out + o_term
            sa_next = s_term if sa_next is None else sa_next + s_term
        o_ref[0, t] = out.astype(BF16)
        return sa_next, lcum

    _, lcum = lax.fori_loop(0, R_STEPS, step, (sa0, jnp.zeros((R_N, s_ref.shape[2]), F32)))
    rows_ref[0] = jnp.exp(lcum)
    for k in range(R_N):
        s_ref[k] = s_ref[k] * rows_ref[0, k:k + 1, :]


def _rwkv_order(dirn, nb, n_ctx_blk, n_blk):
    back = jnp.where(nb < n_ctx_blk, n_ctx_blk - 1 - nb, n_blk + n_ctx_blk - 1 - nb)
    return jnp.where(dirn == 0, nb, back)


def _rwkv(r_t, k_t, v_t, kk_t, kka_t, lw_t):
    t, _, nch = r_t.shape
    n_blk = t // R_STEPS
    n_ctx_blk = N_CTX // R_STEPS
    blk = lambda d, n: (_rwkv_order(d, n, n_ctx_blk, n_blk), 0, 0)
    blk4 = lambda d, n: (d, _rwkv_order(d, n, n_ctx_blk, n_blk), 0, 0)
    spec = pl.BlockSpec((R_STEPS, R_N, nch), blk)
    spec4 = pl.BlockSpec((1, R_STEPS, R_N, nch), blk4)
    return pl.pallas_call(
        _rwkv_kernel,
        grid=(2, n_blk),
        in_specs=[spec, spec, spec, spec, spec, spec4],
        out_specs=spec4,
        out_shape=jax.ShapeDtypeStruct((2, t, R_N, nch), BF16),
        scratch_shapes=[pltpu.VMEM((R_N, R_N, nch), F32), pltpu.VMEM((4, R_N, nch), F32)],
        compiler_params=_cparams(("arbitrary", "arbitrary"), 48),
        name="rwkv",
    )(r_t, k_t, v_t, kk_t, kka_t, lw_t)


def _log_sigmoid(x):
    return jnp.minimum(x, 0.0) - jnp.log(1.0 + jnp.exp(-jnp.abs(x)))


def _gla_dir(q, k, v, alo, wa, ba, st_ref, reverse):
    c = q.shape[0]
    g = _log_sigmoid(_dot(alo, wa, HI) + ba) * (1.0 / G_GATE_NORM)
    row = lax.broadcasted_iota(jnp.int32, (c, c), 0)
    col = lax.broadcasted_iota(jnp.int32, (c, c), 1)
    tri = (row <= col) if reverse else (row >= col)
    bcum = _dot(tri.astype(F32), g, HI)
    blast = bcum[0:1] if reverse else bcum[c - 1:c]
    qd = (q * (G_DK ** -0.5) * jnp.exp(bcum)).astype(BF16)
    kd = (k * jnp.exp(-bcum)).astype(BF16)
    kl = (k * jnp.exp(blast - bcum)).astype(BF16)
    eb = jnp.exp(blast)
    vb = v.astype(BF16)
    outs = []
    for h in range(G_HEADS):
        sk = slice(h * G_DK, (h + 1) * G_DK)
        sv = slice(h * G_DV, (h + 1) * G_DV)
        att = jnp.where(tri, _dot_nt(qd[:, sk], kd[:, sk]), 0.0)
        st = st_ref[h]
        outs.append(_dot_nt(qd[:, sk], st.astype(BF16)) + _dot(att.astype(BF16), vb[:, sv]))
        st_ref[h] = st * eb[:, sk] + _dot_tn(vb[:, sv], kl[:, sk])
    return jnp.concatenate(outs, axis=-1)


def _gla_kernel(qf, kf, vf, lf, qb, kb, vb, lb, wa_ref, ba_ref, of_ref, ob_ref, sf_ref, sb_ref):
    @pl.when(pl.program_id(1) == 0)
    def _():
        sf_ref[...] = jnp.zeros_like(sf_ref)
        sb_ref[...] = jnp.zeros_like(sb_ref)

    of_ref[0] = _gla_dir(qf[0], kf[0], vf[0], lf[0], wa_ref[0], ba_ref[0], sf_ref, False)
    ob_ref[0] = _gla_dir(qb[0], kb[0], vb[0], lb[0], wa_ref[1], ba_ref[1], sb_ref, True)


def _gla(z, w_a2, b_a):
    b, t, _ = z.shape
    n_blk = t // G_CHUNK
    n_ctx_blk = N_CTX // G_CHUNK
    fwd = lambda n: n
    bwd = lambda n: _rwkv_order(1, n, n_ctx_blk, n_blk)
    wk = G_HEADS * G_DK
    cq, ck, cv = C_GLA // wk, (C_GLA + wk) // wk, (C_GLA + 2 * wk) // D
    cl = C_LORA // 128

    def specs(order):
        return [pl.BlockSpec((1, G_CHUNK, wk), lambda i, n: (i, order(n), cq)),
                pl.BlockSpec((1, G_CHUNK, wk), lambda i, n: (i, order(n), ck)),
                pl.BlockSpec((1, G_CHUNK, D), lambda i, n: (i, order(n), cv)),
                pl.BlockSpec((1, G_CHUNK, 128), lambda i, n: (i, order(n), cl))]

    out_sd = jax.ShapeDtypeStruct((b, t, D), F32)
    state = pltpu.VMEM((G_HEADS, G_DV, G_DK), F32)
    return pl.pallas_call(
        _gla_kernel,
        grid=(b, n_blk),
        in_specs=specs(fwd) + specs(bwd) + [
            pl.BlockSpec((2, 128, wk), lambda i, n: (0, 0, 0)),
            pl.BlockSpec((2, 1, wk), lambda i, n: (0, 0, 0))],
        out_specs=[pl.BlockSpec((1, G_CHUNK, D), lambda i, n: (i, fwd(n), 0)),
                   pl.BlockSpec((1, G_CHUNK, D), lambda i, n: (i, bwd(n), 0))],
        out_shape=[out_sd, out_sd],
        scratch_shapes=[state, state],
        compiler_params=_cparams(("parallel", "arbitrary"), 24),
        name="gla",
    )(z, z, z, z, z, z, z, z, w_a2, b_a)


def _mix_kernel(ogf_ref, ogb_ref, gout_ref, orf_ref, orb_ref, bonus_ref, gr_ref, gg_ref, grw_ref, x_ref,
                gnw_ref, lnw_ref, lnb_ref, e_ref, et_ref, wout_ref, m2_ref, n2w_ref, m3_ref, m4_ref,
                h1_o, hmb_o):
    og = ogf_ref[0] + ogb_ref[0]
    parts = []
    for h in range(G_HEADS):
        seg = og[:, h * G_DV:(h + 1) * G_DV]
        parts.append(seg * lax.rsqrt(jnp.mean(seg * seg, axis=-1, keepdims=True) + EPS))
    gout = gout_ref[0]
    y_gla = jnp.concatenate(parts, axis=-1) * gnw_ref[...] * (gout * _sigmoid(gout))

    e = e_ref[...]
    et = et_ref[...]
    o_r = orf_ref[0].astype(F32) + orb_ref[0].astype(F32)
    mu =_dot(_dot(o_r, e, HI) * (1.0 / R_N), et, HI)
    dlt = o_r - mu
    var = _dot(dlt * dlt, e, HI) * (1.0 / R_N)
    o_r = dlt * _dot(lax.rsqrt(var + R_LN_EPS), et, HI) * lnw_ref[...] + lnb_ref[...] + bonus_ref[0]
    y_rwkv = o_r * gr_ref[0]

    y = _sigmoid(gg_ref[0]) * y_gla + _sigmoid(grw_ref[0]) * y_rwkv
    h1 = x_ref[0] + m2_ref[0] * _dot(y.astype(BF16), wout_ref[...])
    h1_o[0] = h1
    hn = h1 * lax.rsqrt(jnp.mean(h1 * h1, axis=-1, keepdims=True) + EPS) * n2w_ref[...]
    hmb_o[0] = (hn * (1.0 + m4_ref[0]) + m3_ref[0]).astype(BF16)


def _mix(og_f, og_b, z, or_f, or_b, bonus, gate_r, x, gnw, lnw, lnb, e, et, w_out, m2, n2w, m3, m4):
    b, tl, _ = x.shape
    off = N_CTX // ROWS
    seq = lambda c: pl.BlockSpec((1, ROWS, D), lambda i, j: (i, j + off, c))
    full = lambda *s: pl.BlockSpec(s, lambda i, j: (0,) * len(s))
    per_b = pl.BlockSpec((1, 1, D), lambda i, j: (i, 0, 0))
    lat = pl.BlockSpec((1, ROWS, D), lambda i, j: (i, j, 0))
    return pl.pallas_call(
        _mix_kernel,
        grid=(b, tl // ROWS),
        in_specs=[seq(0), seq(0), seq((C_GLA + 2048) // D), seq(0), seq(0), seq(0), seq(0),
                  seq(C_GATE // D), seq(C_GATE // D + 1), lat,
                  full(1, D), full(1, D), full(1, D), full(D, R_HEADS), full(R_HEADS, D), full(D, D),
                  per_b, full(1, D), per_b, per_b],
        out_specs=[lat, lat],
        out_shape=[jax.ShapeDtypeStruct((b, tl, D), F32), jax.ShapeDtypeStruct((b, tl, D), BF16)],
        compiler_params=_cparams(("parallel", "parallel"), 52),
        name="mix",
    )(og_f, og_b, z, or_f, or_b, bonus, gate_r, z, z, x, gnw, lnw, lnb, e, et, w_out, m2, n2w, m3, m4)


ROUTE_TOK = 256


def _top_values(work, top_ref, n):
    def body(i, w):
        m = jnp.max(w, axis=0, keepdims=True)
        top_ref[pl.ds(i, 1), :] = m
        return jnp.where(w == m, -jnp.inf, w)
    lax.fori_loop(0, n, body, work)


N_CAND = 80


def _dup_bf16(x):
    hi = lax.bitcast_convert_type(x.astype(BF16).astype(F32), jnp.uint32)
    return hi | (hi >> 16)


def _route_kernel(x_ref, wq_ref, keys_ref, n_o, e1_o, rho_o, e2_o, q_scr, t1_scr, t2_scr, cand_scr):
    q_scr[...] = _dot_nt(wq_ref[...], x_ref[...])
    ninf = -jnp.inf
    half = P_TOPK // 2

    def head(h, carry):
        q1 = q_scr[pl.ds(pl.multiple_of(h * 2 * P_KEYS, P_KEYS), P_KEYS), :]
        q2 = q_scr[pl.ds(pl.multiple_of(h * 2 * P_KEYS + P_KEYS, P_KEYS), P_KEYS), :]
        s1 = _dot(keys_ref[2 * h], q1, HI)
        s2 = _dot(keys_ref[2 * h + 1], q2, HI)
        _top_values(s1, t1_scr, P_TOPK)

        def rank(i, st):
            w, rho = st
            m = jnp.max(w, axis=0, keepdims=True)
            t2_scr[pl.ds(i, 1), :] = m
            top = w == m
            return jnp.where(top, ninf, w), jnp.where(top, lax.convert_element_type(i, F32), rho)

        _, rho = lax.fori_loop(0, P_TOPK, rank, (s2, jnp.full(s2.shape, float(P_TOPK), F32)))
        t1 = t1_scr[...]
        t2 = t2_scr[...]
        cand_scr[0:P_TOPK] = t1[0:1] + t2
        for i in range(1, half):
            cand_scr[P_TOPK + half * (i - 1):P_TOPK + half * i] = t1[i:i + 1] + t2[0:half]
        cand_scr[N_CAND - half:N_CAND] = t1[half:P_TOPK] + t2[0:1]
        cmax = t1[0:1] + t2[0:1]

        def pick(i, st):
            w, z, _ = st
            m = jnp.max(w, axis=0, keepdims=True)
            return jnp.where(w == m, ninf, w), z + jnp.exp(m - cmax), m

        _, zsum, tau = lax.fori_loop(0, P_TOPK, pick, (cand_scr[...], jnp.zeros_like(cmax), cmax))

        def count(j, acc):
            return acc + jnp.where(s1 + t2_scr[pl.ds(j, 1), :] >= tau, 1.0, 0.0)

        cnt = lax.fori_loop(0, P_TOPK, count, jnp.zeros_like(s1))
        sel1 = s1 >= t1[P_TOPK - 1:P_TOPK]
        sel2 = s2 >= t2[P_TOPK - 1:P_TOPK]
        n_o[h] = _dup_bf16(jnp.where(sel1, cnt, 0.0))
        e1_o[h] = _dup_bf16(jnp.where(sel1, jnp.exp(s1 - t1[0:1]), 0.0))
        rho_o[h] = rho.astype(BF16)
        e2_o[h] = jnp.where(sel2, jnp.exp(s2 - t2[0:1]) / zsum, 0.0).astype(BF16)
        return carry

    lax.fori_loop(0, P_HEADS, head, 0)


def _route(hm_b, wq_t, keys):
    n = hm_b.shape[0]
    tn = ROUTE_TOK
    big = pl.BlockSpec((P_HEADS, P_KEYS, tn), lambda i: (0, 0, i))
    sd = lambda dt: jax.ShapeDtypeStruct((P_HEADS, P_KEYS, n), dt)
    return pl.pallas_call(
        _route_kernel,
        grid=(n // tn,),
        in_specs=[pl.BlockSpec((tn, D), lambda i: (i, 0)),
                  pl.BlockSpec((2 * P_HEADS * P_KEYS, D), lambda i: (0, 0)),
                  pl.BlockSpec((2 * P_HEADS, P_KEYS, P_KEYS), lambda i: (0, 0, 0))],
        out_specs=[big, big, big, big],
        out_shape=[sd(jnp.uint32), sd(jnp.uint32), sd(BF16), sd(BF16)],
        scratch_shapes=[pltpu.VMEM((2 * P_HEADS * P_KEYS, tn), F32),
                        pltpu.VMEM((P_TOPK, tn), F32), pltpu.VMEM((P_TOPK, tn), F32),
                        pltpu.VMEM((N_CAND, tn), F32)],
        compiler_params=_cparams(("parallel",), 48),
        name="route",
    )(hm_b, wq_t, keys)


PEER_TOK = 512
PEER_EXP = 1024


def _gelu(x):
    return 0.5 * x * (1.0 + lax.erf(x * (2.0 ** -0.5)))


BF16_ROWS = 16


def _peer_kernel(x_ref, u_ref, vt_ref, n_ref, e1_ref, rho_ref, e2_ref, o_ref):
    j = pl.program_id(1)

    @pl.when(j == 0)
    def _():
        o_ref[...] = jnp.zeros_like(o_ref)

    tn = x_ref.shape[0]
    x = x_ref[...]
    n_a = PEER_EXP // P_KEYS
    groups = P_KEYS // BF16_ROWS
    halves = 2
    a_per_half = n_a // halves
    p_halves = []
    for half in range(halves):
        rows_h = a_per_half * P_KEYS
        act = _dot_nt(u_ref[half * rows_h:(half + 1) * rows_h, :], x)
        gates = []
        for al in range(half * a_per_half, (half + 1) * a_per_half):
            g = [None] * groups
            for h in range(P_HEADS):
                row = lambda ref: pltpu.bitcast(jnp.broadcast_to(ref[h, al:al + 1, :], (8, tn)), BF16)
                n_row = row(n_ref)
                e1_row = row(e1_ref)
                for q in range(groups):
                    rows = slice(q * BF16_ROWS, (q + 1) * BF16_ROWS)
                    term = jnp.where(rho_ref[h, rows, :] < n_row, e2_ref[h, rows, :], 0.0) * e1_row
                    g[q] = term if g[q] is None else g[q] + term
            gates.extend(g)
        p_halves.append(_gelu(act).astype(BF16) * jnp.concatenate(gates, axis=0))
    o_ref[...] += _dot(vt_ref[...], jnp.concatenate(p_halves, axis=0))


def _peer(hm_b, u_b, vt_b, n_cnt, e1, rho, e2):
    n = hm_b.shape[0]
    n_exp = u_b.shape[0]
    tn, te = PEER_TOK, PEER_EXP
    big = pl.BlockSpec((P_HEADS, P_KEYS, tn), lambda i, j: (0, 0, i))
    a_rows = pl.BlockSpec((P_HEADS, te // P_KEYS, tn), lambda i, j: (0, j, i))
    return pl.pallas_call(
        _peer_kernel,
        grid=(n // tn, n_exp // te),
        in_specs=[pl.BlockSpec((tn, D), lambda i, j: (i, 0)),
                  pl.BlockSpec((te, D), lambda i, j: (j, 0)),
                  pl.BlockSpec((D, te), lambda i, j: (0, j)),
                  a_rows, a_rows, big, big],
        out_specs=pl.BlockSpec((D, tn), lambda i, j: (0, i)),
        out_shape=jax.ShapeDtypeStruct((D, n), F32),
        compiler_params=_cparams(("parallel", "arbitrary"), 52),
        name="peer",
    )(hm_b, u_b, vt_b, n_cnt, e1, rho, e2)


def _final_kernel(h1_ref, f_ref, m5_ref, w_ref, o_ref):
    h = h1_ref[0] + m5_ref[0] * f_ref[0]
    o_ref[0] = h * lax.rsqrt(jnp.mean(h * h, axis=-1, keepdims=True) + EPS) * w_ref[...]


def _final(h1, f, m5, w):
    b, tl, _ = h1.shape
    lat = pl.BlockSpec((1, ROWS, D), lambda i, j: (i, j, 0))
    return pl.pallas_call(
        _final_kernel,
        grid=(b, tl // ROWS),
        in_specs=[lat, lat, pl.BlockSpec((1, 1, D), lambda i, j: (i, 0, 0)),
                  pl.BlockSpec((1, D), lambda i, j: (0, 0))],
        out_specs=lat,
        out_shape=jax.ShapeDtypeStruct((b, tl, D), F32),
        compiler_params=_cparams(("parallel", "parallel"), 24),
        name="final",
    )(h1, f, m5, w)


def _reorder_cols(w):
    gla_main, gla_lora = w[..., 0:3072], w[..., 3072:3104]
    rw_main, rw_lora = w[..., 3104:6176], w[..., 6176:6528]
    gates = w[..., 6528:8576]
    pad = jnp.zeros(w.shape[:-1] + (512 - 32 - 352,), w.dtype)
    return jnp.concatenate([rw_main, gla_main, gates, gla_lora, rw_lora, pad], axis=-1)


def _to_chains(a):
    b, t, _ = a.shape
    return a.reshape(b, t, R_HEADS, R_N).transpose(1, 3, 0, 2).reshape(t, R_N, b * R_HEADS)


def _from_chains(a, b):
    t = a.shape[0]
    return a.reshape(t, R_N, b, R_HEADS).transpose(2, 0, 3, 1).reshape(b, t, D)


def kernel(x, c, ctx, c_ctx, norm1_w, w_mod, b_mod, w_in, gla_w_a2, gla_b_a, gla_norm_w, rwkv_mu, rwkv_w0, rwkv_w2, rwkv_a0, rwkv_a2, rwkv_g2, rwkv_k_k, rwkv_k_a, rwkv_r_k, rwkv_ln_w, rwkv_ln_b, w_out, norm2_w, peer_w_q, peer_sub_keys, peer_u, peer_v, final_norm_w):
    b, seq, _ = x.shape
    assert w_in.shape[0] == 1 and ctx.shape[1] == N_CTX and seq % ROWS == 0
    row = lambda v: v.reshape(1, -1)

    c16 = jnp.zeros((16, D), F32).at[:b].set(c).at[b].set(c_ctx)
    m = _mod(c16, w_mod[0], row(b_mod[0])).reshape(16, N_MOD, D)
    m_lat, m_ctx = m[:b], m[b]
    per_b = lambda i: m_lat[:, i].reshape(b, 1, D)
    shift1 = jnp.stack([jnp.broadcast_to(m_ctx[0], (b, D)), m_lat[:, 0]], axis=1).reshape(b, 2, 1, D)
    scale1 = jnp.stack([jnp.broadcast_to(m_ctx[1], (b, D)), m_lat[:, 1]], axis=1).reshape(b, 2, 1, D)

    h_all = jnp.concatenate([ctx, x], axis=1)
    z = _inproj(h_all, row(norm1_w[0]), scale1, shift1, _reorder_cols(w_in[0]).astype(BF16))

    mu = rwkv_mu[0]
    mu_lora = jnp.concatenate([jnp.zeros((32,), F32), mu[3072:], jnp.zeros((128,), F32)]).reshape(1, 512)
    e = jnp.repeat(jnp.eye(R_HEADS, dtype=F32), R_N, axis=0)
    w2f = jnp.zeros((128, D), F32).at[L_WF:L_WF + R_W_LORA].set(rwkv_w2[0, 0])
    w2b = jnp.zeros((256, D), F32).at[L_WB:L_WB + R_W_LORA].set(rwkv_w2[0, 1])
    a2 = jnp.zeros((128, D), F32).at[L_A - 128:L_A - 128 + R_A_LORA].set(rwkv_a2[0])
    g2 = jnp.zeros((256, D), F32).at[L_G - 128:L_G - 128 + R_G_LORA].set(rwkv_g2[0])
    wa = jnp.zeros((2, 128, G_HEADS * G_DK), F32)
    wa = wa.at[0, 0:G_LORA].set(gla_w_a2[0, 0]).at[1, G_LORA:2 * G_LORA].set(gla_w_a2[0, 1])
    r, k2, vr, kk, kka, d_f, d_b, bonus, gate_r = _rprep(
        z, row(mu[:3072]), mu_lora, rwkv_w0[0].reshape(2, 1, D), w2f, w2b, row(rwkv_a0[0]), a2,
        g2, row(rwkv_k_k[0]), row(rwkv_k_a[0]), row(rwkv_r_k[0]), e, e.T)

    o_r = _rwkv(_to_chains(r), _to_chains(k2), _to_chains(vr), _to_chains(kk), _to_chains(kka),
                jnp.stack([_to_chains(d_f), _to_chains(d_b)]))
    or_f, or_b = _from_chains(o_r[0], b), _from_chains(o_r[1], b)

    og_f, og_b = _gla(z, wa, gla_b_a[0].reshape(2, 1, -1))

    h1, hm_b = _mix(og_f, og_b, z, or_f, or_b, bonus, gate_r, x, row(gla_norm_w[0]), row(rwkv_ln_w[0]),
                        row(rwkv_ln_b[0]), e, e.T, w_out[0].astype(BF16), per_b(2), row(norm2_w[0]),
                        per_b(3), per_b(4))

    n_tok = b * seq
    n_cnt, e1, rho, e2 = _route(hm_b.reshape(n_tok, D), peer_w_q[0].T.astype(BF16),
                                peer_sub_keys[0].reshape(2 * P_HEADS, P_KEYS, -1))
    f_t = _peer(hm_b.reshape(n_tok, D), peer_u[0].astype(BF16), peer_v[0].T.astype(BF16), n_cnt, e1, rho, e2)
    return _final(h1, f_t.T.reshape(b, seq, D), per_b(5), row(final_norm_w))
```

```python
import functools

import numpy as np
import jax
import jax.numpy as jnp
from jax import lax
from jax.experimental import pallas as pl
from jax.experimental.pallas import tpu as pltpu

F32 = jnp.float32
BF16 = jnp.bfloat16
HI = lax.Precision.HIGHEST

D = 1024
GRID_COLS = 64
N_CTX = 256
EPS = 1e-6
N_MOD = 6

G_HEADS = 4
G_DV = D // G_HEADS
G_DK = G_DV // 2
G_LORA = 16
G_GATE_NORM = 16.0
G_CHUNK = 64

R_N = 64
R_HEADS = D // R_N
R_LN_EPS = 64e-5
R_W_LORA = 64
R_A_LORA = 64
R_G_LORA = 160

P_HEADS = 8
P_KEYS = 128
P_TOPK = 16

C_RWKV = 0
C_GLA = 3072
C_GATE = 6144
C_LORA = 8192
N_PROJ = 8704
L_WF, L_WB, L_A, L_G = 32, 96, 160, 224

ROWS = 256
V7X_VMEM_BYTES = 64 * 1024 * 1024


def _cparams(sem, vmem_mb):
    return pltpu.CompilerParams(dimension_semantics=sem,
                                vmem_limit_bytes=min(vmem_mb * 1024 * 1024, V7X_VMEM_BYTES - (8 << 20)))


def _sigmoid(x):
    return 1.0 / (1.0 + jnp.exp(-x))


def _softplus(x):
    return jnp.maximum(x, 0.0) + jnp.log(1.0 + jnp.exp(-jnp.abs(x)))


def _dot(a, b, precision=None):
    return jnp.dot(a, b, preferred_element_type=F32, precision=precision)


def _dot_nt(a, b, precision=None):
    return lax.dot_general(a, b, (((1,), (1,)), ((), ())), preferred_element_type=F32, precision=precision)


def _dot_tn(a, b, precision=None):
    return lax.dot_general(a, b, (((0,), (0,)), ((), ())), preferred_element_type=F32, precision=precision)


def _mod_kernel(c_ref, w_ref, b_ref, o_ref):
    c = c_ref[...]
    s = c * _sigmoid(c)
    o_ref[...] = _dot(s.astype(BF16), w_ref[...].astype(BF16)) + b_ref[...]


def _mod(c16, w_mod, b_mod):
    n = w_mod.shape[1]
    tn = 768
    return pl.pallas_call(
        _mod_kernel,
        grid=(n // tn,),
        in_specs=[pl.BlockSpec((16, D), lambda j: (0, 0)),
                  pl.BlockSpec((D, tn), lambda j: (0, j)),
                  pl.BlockSpec((1, tn), lambda j: (0, j))],
        out_specs=pl.BlockSpec((16, tn), lambda j: (0, j)),
        out_shape=jax.ShapeDtypeStruct((16, n), F32),
        compiler_params=_cparams(("parallel",), 24),
        name="mod",
    )(c16, w_mod, b_mod)


INPROJ_ROWS = 768


def _inproj_kernel(h_ref, nw_ref, sc_ref, sh_ref, w_ref, o_ref, a_scr):
    @pl.when(pl.program_id(2) == 0)
    def _():
        x = h_ref[0]
        y = x * lax.rsqrt(jnp.mean(x * x, axis=-1, keepdims=True) + EPS) * nw_ref[...]
        t = pl.program_id(1) * INPROJ_ROWS + lax.broadcasted_iota(jnp.int32, (INPROJ_ROWS, 1), 0)
        is_ctx = t < N_CTX
        sc = jnp.where(is_ctx, sc_ref[0, 0], sc_ref[0, 1])
        sh = jnp.where(is_ctx, sh_ref[0, 0], sh_ref[0, 1])
        a_scr[...] = (y * (1.0 + sc) + sh).astype(BF16)

    o_ref[0] = _dot(a_scr[...], w_ref[...])


def _inproj(h_all, norm_w, scale, shift, w_p):
    b, t, _ = h_all.shape
    tn = 512
    mod = pl.BlockSpec((1, 2, 1, D), lambda i, j, n: (i, 0, 0, 0))
    return pl.pallas_call(
        _inproj_kernel,
        grid=(b, t // INPROJ_ROWS, N_PROJ // tn),
        in_specs=[pl.BlockSpec((1, INPROJ_ROWS, D), lambda i, j, n: (i, j, 0)),
                  pl.BlockSpec((1, D), lambda i, j, n: (0, 0)),
                  mod, mod,
                  pl.BlockSpec((D, tn), lambda i, j, n: (0, n))],
        out_specs=pl.BlockSpec((1, INPROJ_ROWS, tn), lambda i, j, n: (i, j, n)),
        out_shape=jax.ShapeDtypeStruct((b, t, N_PROJ), F32),
        scratch_shapes=[pltpu.VMEM((INPROJ_ROWS, D), BF16)],
        compiler_params=_cparams(("parallel", "parallel", "arbitrary"), 32),
        name="inproj",
    )(h_all, norm_w, scale, shift, w_p)


def _shift_mix(zc, zp, zn, mu, is_ctx):
    rows, w = zc.shape
    row = lax.broadcasted_iota(jnp.int32, (rows, w), 0)
    lane = lax.broadcasted_iota(jnp.int32, (rows, w), 1)
    pmask = jnp.where(is_ctx, 1, 3)
    rmask = jnp.where(is_ctx, rows - 1, GRID_COLS - 1)
    cls = lane & pmask
    rr = row & rmask
    prev = pltpu.roll(zc, 1, 0)
    nxt = pltpu.roll(zc, rows - 1, 0)
    up = jnp.concatenate([zp, zc[:rows - GRID_COLS]], axis=0)
    down = jnp.concatenate([zc[GRID_COLS:], zn], axis=0)
    zs = jnp.where((cls == 0) & (rr != 0), prev,
                   jnp.where((cls == 1) & (rr != rmask), nxt,
                             jnp.where(cls == 2, up, jnp.where(cls == 3, down, 0.0))))
    return zc + (zs - zc) * mu


def _rprep_kernel(zc_ref, zp_ref, zn_ref, lc_ref, lp_ref, ln_ref,
                  mu_ref, mul_ref, w0_ref, w2f_ref, w2b_ref, a0_ref, a2_ref, g2_ref,
                  kk_w_ref, ka_w_ref, rk_w_ref, e_ref, et_ref,
                  r_o, k_o, v_o, kk_o, kka_o, df_o, db_o, bonus_o, gate_o):
    tb = pl.program_id(1)
    ntb = pl.num_programs(1)
    is_ctx = tb == 0
    has_up = (tb > 1).astype(F32)
    has_down = (tb < ntb - 1).astype(F32)

    zl = _shift_mix(lc_ref[0], lp_ref[0] * has_up, ln_ref[0] * has_down, mul_ref[...], is_ctx)
    def decay(window, w2p_ref, d):
        w = -_softplus(-(w0_ref[d] + _dot(jnp.tanh(window), w2p_ref[...], HI))) - 0.5
        return -jnp.exp(w)

    df_o[0] = decay(zl[:, 0:128], w2f_ref, 0)
    db_o[0] = decay(zl[:, 0:256], w2b_ref, 1)
    a = _sigmoid(a0_ref[...] + _dot(zl[:, 128:256], a2_ref[...], HI))
    gate_o[0] = _dot(_sigmoid(zl[:, 128:384]), g2_ref[...], HI)

    def piece(i):
        sl = slice(i * D, (i + 1) * D)
        return _shift_mix(zc_ref[0, :, sl], zp_ref[0, :, sl] * has_up, zn_ref[0, :, sl] * has_down,
                          mu_ref[:, sl], is_ctx)

    r = piece(0)
    kr = piece(1)
    vr = piece(2)
    e = e_ref[...]
    et = et_ref[...]
    kkraw = kr * kk_w_ref[...]
    nrm = jnp.sqrt(_dot(kkraw * kkraw, e, HI))
    inv = 1.0 / jnp.maximum(nrm, 1e-12)
    kk = kkraw * _dot(inv, et, HI)
    k2 = kr * (1.0 + (a - 1.0) * ka_w_ref[...])
    rk = _dot(r * k2 * rk_w_ref[...], e, HI)
    r_o[0] = r.astype(BF16)
    k_o[0] = k2.astype(BF16)
    v_o[0] = vr.astype(BF16)
    kk_o[0] = kk.astype(BF16)
    kka_o[0] = (kk * a).astype(BF16)
    bonus_o[0] = _dot(rk, et, HI) * vr


def _rprep(z, mu_main, mu_lora, w0, w2f, w2b, a0, a2, g2, k_k, k_a, r_k, e, et):
    b, t, _ = z.shape
    ntb = t // ROWS
    q = ROWS // GRID_COLS
    last64 = t // GRID_COLS - 1
    wm = 3 * D
    cm = C_RWKV // wm
    cl = C_LORA // 512
    cur = lambda c: (lambda i, j: (i, j, c))
    prv = lambda c: (lambda i, j: (i, jnp.maximum(j * q - 1, 0), c))
    nxt = lambda c: (lambda i, j: (i, jnp.minimum(j * q + q, last64), c))
    full = lambda *s: pl.BlockSpec(s, lambda i, j: (0,) * len(s))
    out_spec = pl.BlockSpec((1, ROWS, D), lambda i, j: (i, j, 0))
    out_sd = jax.ShapeDtypeStruct((b, t, D), F32)
    return pl.pallas_call(
        _rprep_kernel,
        grid=(b, ntb),
        in_specs=[pl.BlockSpec((1, ROWS, wm), cur(cm)),
                  pl.BlockSpec((1, GRID_COLS, wm), prv(cm)),
                  pl.BlockSpec((1, GRID_COLS, wm), nxt(cm)),
                  pl.BlockSpec((1, ROWS, 512), cur(cl)),
                  pl.BlockSpec((1, GRID_COLS, 512), prv(cl)),
                  pl.BlockSpec((1, GRID_COLS, 512), nxt(cl)),
                  full(1, wm), full(1, 512), full(2, 1, D), full(128, D), full(256, D),
                  full(1, D), full(128, D), full(256, D),
                  full(1, D), full(1, D), full(1, D), full(D, R_HEADS), full(R_HEADS, D)],
        out_specs=[out_spec] * 9,
        out_shape=[jax.ShapeDtypeStruct((b, t, D), BF16)] * 5 + [out_sd] * 4,
        compiler_params=_cparams(("parallel", "parallel"), 52),
        name="rprep",
    )(z, z, z, z, z, z, mu_main, mu_lora, w0, w2f, w2b, a0, a2, g2, k_k, k_a, r_k, e, et)


R_STEPS = 64


def _rwkv_kernel(r_ref, k_ref, v_ref, kk_ref, kka_ref, lw_ref, o_ref, s_ref, rows_ref):
    dirn = pl.program_id(0)
    tix = lambda i: jnp.where(dirn == 0, i, R_STEPS - 1 - i)

    @pl.when(pl.program_id(1) == 0)
    def _():
        s_ref[...] = jnp.zeros_like(s_ref)

    def weighted_sum(rows):
        acc = s_ref[0] * rows_ref[rows, 0:1, :]
        for k in range(1, R_N):
            acc = acc + s_ref[k] * rows_ref[rows, k:k + 1, :]
        return acc

    rows_ref[0] = kk_ref[tix(0)].astype(F32)
    sa0 = weighted_sum(0)

    def step(i, carry):
        sa, lcum = carry
        t = tix(i)
        t_next = tix(jnp.minimum(i + 1, R_STEPS - 1))
        lcum = lcum + lw_ref[0, t]
        e_pos = jnp.exp(lcum)
        e_neg = jnp.exp(-lcum)
        rows_ref[0] = kk_ref[t_next].astype(F32) * e_pos
        rows_ref[1] = kka_ref[t].astype(F32) * e_neg
        rows_ref[2] = k_ref[t].astype(F32) * e_neg
        rows_ref[3] = r_ref[t].astype(F32) * e_pos
        vv = v_ref[t].astype(F32)
        out = None
        sa_next = None
        for k in range(R_N):
            sk = s_ref[k] - sa * rows_ref[1, k:k + 1, :] + vv * rows_ref[2, k:k + 1, :]
            s_ref[k] = sk
            o_term = sk * rows_ref[3, k:k + 1, :]
            s_term = sk * rows_ref[0, k:k + 1, :]
            out = o_term if out is None else out + o_term
            sa_next = s_term if sa_next is None else sa_next + s_term
        o_ref[0, t] = out.astype(BF16)
        return sa_next, lcum

    _, lcum = lax.fori_loop(0, R_STEPS, step, (sa0, jnp.zeros((R_N, s_ref.shape[2]), F32)))
    rows_ref[0] = jnp.exp(lcum)
    for k in range(R_N):
        s_ref[k] = s_ref[k] * rows_ref[0, k:k + 1, :]


def _rwkv_order(dirn, nb, n_ctx_blk, n_blk):
    back = jnp.where(nb < n_ctx_blk, n_ctx_blk - 1 - nb, n_blk + n_ctx_blk - 1 - nb)
    return jnp.where(dirn == 0, nb, back)


def _rwkv(r_t, k_t, v_t, kk_t, kka_t, lw_t):
    t, _, nch = r_t.shape
    n_blk = t // R_STEPS
    n_ctx_blk = N_CTX // R_STEPS
    blk = lambda d, n: (_rwkv_order(d, n, n_ctx_blk, n_blk), 0, 0)
    blk4 = lambda d, n: (d, _rwkv_order(d, n, n_ctx_blk, n_blk), 0, 0)
    spec = pl.BlockSpec((R_STEPS, R_N, nch), blk)
    spec4 = pl.BlockSpec((1, R_STEPS, R_N, nch), blk4)
    return pl.pallas_call(
        _rwkv_kernel,
        grid=(2, n_blk),
        in_specs=[spec, spec, spec, spec, spec, spec4],
        out_specs=spec4,
        out_shape=jax.ShapeDtypeStruct((2, t, R_N, nch), BF16),
        scratch_shapes=[pltpu.VMEM((R_N, R_N, nch), F32), pltpu.VMEM((4, R_N, nch), F32)],
        compiler_params=_cparams(("arbitrary", "arbitrary"), 48),
        name="rwkv",
    )(r_t, k_t, v_t, kk_t, kka_t, lw_t)


def _log_sigmoid(x):
    return jnp.minimum(x, 0.0) - jnp.log(1.0 + jnp.exp(-jnp.abs(x)))


def _gla_dir(q, k, v, alo, wa, ba, st_ref, reverse):
    c = q.shape[0]
    g = _log_sigmoid(_dot(alo, wa, HI) + ba) * (1.0 / G_GATE_NORM)
    row = lax.broadcasted_iota(jnp.int32, (c, c), 0)
    col = lax.broadcasted_iota(jnp.int32, (c, c), 1)
    tri = (row <= col) if reverse else (row >= col)
    bcum = _dot(tri.astype(F32), g, HI)
    blast = bcum[0:1] if reverse else bcum[c - 1:c]
    qd = (q * (G_DK ** -0.5) * jnp.exp(bcum)).astype(BF16)
    kd = (k * jnp.exp(-bcum)).astype(BF16)
    kl = (k * jnp.exp(blast - bcum)).astype(BF16)
    eb = jnp.exp(blast)
    vb = v.astype(BF16)
    outs = []
    for h in range(G_HEADS):
        sk = slice(h * G_DK, (h + 1) * G_DK)
        sv = slice(h * G_DV, (h + 1) * G_DV)
        att = jnp.where(tri, _dot_nt(qd[:, sk], kd[:, sk]), 0.0)
        st = st_ref[h]
        outs.append(_dot_nt(qd[:, sk], st.astype(BF16)) + _dot(att.astype(BF16), vb[:, sv]))
        st_ref[h] = st * eb[:, sk] + _dot_tn(vb[:, sv], kl[:, sk])
    return jnp.concatenate(outs, axis=-1)


def _gla_kernel(qf, kf, vf, lf, qb, kb, vb, lb, wa_ref, ba_ref, of_ref, ob_ref, sf_ref, sb_ref):
    @pl.when(pl.program_id(1) == 0)
    def _():
        sf_ref[...] = jnp.zeros_like(sf_ref)
        sb_ref[...] = jnp.zeros_like(sb_ref)

    of_ref[0] = _gla_dir(qf[0], kf[0], vf[0], lf[0], wa_ref[0], ba_ref[0], sf_ref, False)
    ob_ref[0] = _gla_dir(qb[0], kb[0], vb[0], lb[0], wa_ref[1], ba_ref[1], sb_ref, True)


def _gla(z, w_a2, b_a):
    b, t, _ = z.shape
    n_blk = t // G_CHUNK
    n_ctx_blk = N_CTX // G_CHUNK
    fwd = lambda n: n
    bwd = lambda n: _rwkv_order(1, n, n_ctx_blk, n_blk)
    wk = G_HEADS * G_DK
    cq, ck, cv = C_GLA // wk, (C_GLA + wk) // wk, (C_GLA + 2 * wk) // D
    cl = C_LORA // 128

    def specs(order):
        return [pl.BlockSpec((1, G_CHUNK, wk), lambda i, n: (i, order(n), cq)),
                pl.BlockSpec((1, G_CHUNK, wk), lambda i, n: (i, order(n), ck)),
                pl.BlockSpec((1, G_CHUNK, D), lambda i, n: (i, order(n), cv)),
                pl.BlockSpec((1, G_CHUNK, 128), lambda i, n: (i, order(n), cl))]

    out_sd = jax.ShapeDtypeStruct((b, t, D), F32)
    state = pltpu.VMEM((G_HEADS, G_DV, G_DK), F32)
    return pl.pallas_call(
        _gla_kernel,
        grid=(b, n_blk),
        in_specs=specs(fwd) + specs(bwd) + [
            pl.BlockSpec((2, 128, wk), lambda i, n: (0, 0, 0)),
            pl.BlockSpec((2, 1, wk), lambda i, n: (0, 0, 0))],
        out_specs=[pl.BlockSpec((1, G_CHUNK, D), lambda i, n: (i, fwd(n), 0)),
                   pl.BlockSpec((1, G_CHUNK, D), lambda i, n: (i, bwd(n), 0))],
        out_shape=[out_sd, out_sd],
        scratch_shapes=[state, state],
        compiler_params=_cparams(("parallel", "arbitrary"), 24),
        name="gla",
    )(z, z, z, z, z, z, z, z, w_a2, b_a)


def _mix_kernel(ogf_ref, ogb_ref, gout_ref, orf_ref, orb_ref, bonus_ref, gr_ref, gg_ref, grw_ref, x_ref,
                gnw_ref, lnw_ref, lnb_ref, e_ref, et_ref, wout_ref, m2_ref, n2w_ref, m3_ref, m4_ref,
                h1_o, hmb_o):
    og = ogf_ref[0] + ogb_ref[0]
    parts = []
    for h in range(G_HEADS):
        seg = og[:, h * G_DV:(h + 1) * G_DV]
        parts.append(seg * lax.rsqrt(jnp.mean(seg * seg, axis=-1, keepdims=True) + EPS))
    gout = gout_ref[0]
    y_gla = jnp.concatenate(parts, axis=-1) * gnw_ref[...] * (gout * _sigmoid(gout))

    e = e_ref[...]
    et = et_ref[...]
    o_r = orf_ref[0].astype(F32) + orb_ref[0].astype(F32)
    mu =_dot(_dot(o_r, e, HI) * (1.0 / R_N), et, HI)
    dlt = o_r - mu
    var = _dot(dlt * dlt, e, HI) * (1.0 / R_N)
    o_r = dlt * _dot(lax.rsqrt(var + R_LN_EPS), et, HI) * lnw_ref[...] + lnb_ref[...] + bonus_ref[0]
    y_rwkv = o_r * gr_ref[0]

    y = _sigmoid(gg_ref[0]) * y_gla + _sigmoid(grw_ref[0]) * y_rwkv
    h1 = x_ref[0] + m2_ref[0] * _dot(y.astype(BF16), wout_ref[...])
    h1_o[0] = h1
    hn = h1 * lax.rsqrt(jnp.mean(h1 * h1, axis=-1, keepdims=True) + EPS) * n2w_ref[...]
    hmb_o[0] = (hn * (1.0 + m4_ref[0]) + m3_ref[0]).astype(BF16)


def _mix(og_f, og_b, z, or_f, or_b, bonus, gate_r, x, gnw, lnw, lnb, e, et, w_out, m2, n2w, m3, m4):
    b, tl, _ = x.shape
    off = N_CTX // ROWS
    seq = lambda c: pl.BlockSpec((1, ROWS, D), lambda i, j: (i, j + off, c))
    full = lambda *s: pl.BlockSpec(s, lambda i, j: (0,) * len(s))
    per_b = pl.BlockSpec((1, 1, D), lambda i, j: (i, 0, 0))
    lat = pl.BlockSpec((1, ROWS, D), lambda i, j: (i, j, 0))
    return pl.pallas_call(
        _mix_kernel,
        grid=(b, tl // ROWS),
        in_specs=[seq(0), seq(0), seq((C_GLA + 2048) // D), seq(0), seq(0), seq(0), seq(0),
                  seq(C_GATE // D), seq(C_GATE // D + 1), lat,
                  full(1, D), full(1, D), full(1, D), full(D, R_HEADS), full(R_HEADS, D), full(D, D),
                  per_b, full(1, D), per_b, per_b],
        out_specs=[lat, lat],
        out_shape=[jax.ShapeDtypeStruct((b, tl, D), F32), jax.ShapeDtypeStruct((b, tl, D), BF16)],
        compiler_params=_cparams(("parallel", "parallel"), 52),
        name="mix",
    )(og_f, og_b, z, or_f, or_b, bonus, gate_r, z, z, x, gnw, lnw, lnb, e, et, w_out, m2, n2w, m3, m4)


ROUTE_TOK = 512


N_CAND = 80


def _dup_bf16(x):
    hi = lax.bitcast_convert_type(x.astype(BF16).astype(F32), jnp.uint32)
    return hi | (hi >> 16)


def _route_kernel(x_ref, wq_ref, keys_ref, n_o, e1_o, rho_o, e2_o,
                  q_scr, s_scr, t1_scr, t2_scr, cand_scr, rho_scr, cnt_scr):
    q_scr[...] = _dot(wq_ref[...], x_ref[...]).astype(BF16)
    ninf = -jnp.inf
    half = P_TOPK // 2
    tn = x_ref.shape[1]
    top_row = jnp.full((1, tn), jnp.inf, F32)

    def next_below(vals, bound):
        below = vals < bound
        return below, jnp.max(jnp.where(below, vals, ninf), axis=0, keepdims=True)

    def head(h, carry):
        q1 = q_scr[pl.ds(pl.multiple_of(h * 2 * P_KEYS, P_KEYS), P_KEYS), :]
        q2 = q_scr[pl.ds(pl.multiple_of(h * 2 * P_KEYS + P_KEYS, P_KEYS), P_KEYS), :]
        s_scr[0] = _dot(keys_ref[2 * h], q1)
        s_scr[1] = _dot(keys_ref[2 * h + 1], q2)
        rho_scr[...] = jnp.full(rho_scr.shape, -1.0, F32)

        def extract(i, bounds):
            m1, m2 = bounds
            _, m1 = next_below(s_scr[0], m1)
            below2, m2 = next_below(s_scr[1], m2)
            rho_scr[...] += jnp.where(below2, 1.0, 0.0)
            t1_scr[pl.ds(i, 1), :] = m1
            t2_scr[pl.ds(i, 1), :] = m2
            return m1, m2

        _, last2 = lax.fori_loop(0, P_TOPK, extract, (top_row, top_row))
        s1 = s_scr[0]
        s2 = s_scr[1]
        t1 = t1_scr[...]
        t2 = t2_scr[...]
        cand_scr[0:P_TOPK] = t1[0:1] + t2
        for i in range(1, half):
            cand_scr[P_TOPK + half * (i - 1):P_TOPK + half * i] = t1[i:i + 1] + t2[0:half]
        cand_scr[N_CAND - half:N_CAND] = t1[half:P_TOPK] + t2[0:1]
        cmax = t1[0:1] + t2[0:1]

        def pick(i, st):
            m, z = st
            _, m = next_below(cand_scr[...], m)
            return m, z + jnp.exp(m - cmax)

        tau, zsum = lax.fori_loop(0, P_TOPK, pick, (top_row, jnp.zeros_like(cmax)))
        cnt_scr[...] = jnp.zeros_like(cnt_scr)

        def count(j, c):
            cnt_scr[...] += jnp.where(s_scr[0] + t2_scr[pl.ds(j, 1), :] >= tau, 1.0, 0.0)
            return c

        lax.fori_loop(0, P_TOPK, count, 0)
        sel1 = s1 >= t1[P_TOPK - 1:P_TOPK]
        sel2 = s2 >= last2
        rho = rho_scr[...] + jnp.where(s2 < last2, 1.0, 0.0)
        n_o[h] = _dup_bf16(jnp.where(sel1, cnt_scr[...], 0.0))
        e1_o[h] = _dup_bf16(jnp.where(sel1, jnp.exp(s1 - t1[0:1]), 0.0))
        rho_o[h] = rho.astype(BF16)
        e2_o[h] = jnp.where(sel2, jnp.exp(s2 - t2[0:1]) / zsum, 0.0).astype(BF16)
        return carry

    lax.fori_loop(0, P_HEADS, head, 0)


def _route(hm_t, wq_t, keys):
    n = hm_t.shape[1]
    tn = ROUTE_TOK
    big = pl.BlockSpec((P_HEADS, P_KEYS, tn), lambda i: (0, 0, i))
    sd = lambda dt: jax.ShapeDtypeStruct((P_HEADS, P_KEYS, n), dt)
    return pl.pallas_call(
        _route_kernel,
        grid=(n // tn,),
        in_specs=[pl.BlockSpec((D, tn), lambda i: (0, i)),
                  pl.BlockSpec((2 * P_HEADS * P_KEYS, D), lambda i: (0, 0)),
                  pl.BlockSpec((2 * P_HEADS, P_KEYS, P_KEYS), lambda i: (0, 0, 0))],
        out_specs=[big, big, big, big],
        out_shape=[sd(jnp.uint32), sd(jnp.uint32), sd(BF16), sd(BF16)],
        scratch_shapes=[pltpu.VMEM((2 * P_HEADS * P_KEYS, tn), BF16), pltpu.VMEM((2, P_KEYS, tn), F32),
                        pltpu.VMEM((P_TOPK, tn), F32), pltpu.VMEM((P_TOPK, tn), F32),
                        pltpu.VMEM((N_CAND, tn), F32),
                        pltpu.VMEM((P_KEYS, tn), F32), pltpu.VMEM((P_KEYS, tn), F32)],
        compiler_params=_cparams(("parallel",), 48),
        name="route",
    )(hm_t, wq_t, keys)


PEER_TOK = 512
PEER_EXP = 1024


def _gelu(x):
    return 0.5 * x * (1.0 + lax.erf(x * (2.0 ** -0.5)))


BF16_ROWS = 16


def _peer_kernel(x_ref, u_ref, vt_ref, n_ref, e1_ref, rho_ref, e2_ref, o_ref):
    j = pl.program_id(1)

    @pl.when(j == 0)
    def _():
        o_ref[...] = jnp.zeros_like(o_ref)

    tn = x_ref.shape[1]
    x = x_ref[...]
    n_a = PEER_EXP // P_KEYS
    groups = P_KEYS // BF16_ROWS
    halves = 2
    a_per_half = n_a // halves
    p_halves = []
    for half in range(halves):
        rows_h = a_per_half * P_KEYS
        act = _dot(u_ref[half * rows_h:(half + 1) * rows_h, :], x)
        gates = []
        for al in range(half * a_per_half, (half + 1) * a_per_half):
            g = [None] * groups
            for h in range(P_HEADS):
                row = lambda ref: pltpu.bitcast(jnp.broadcast_to(ref[h, al:al + 1, :], (8, tn)), BF16)
                n_row = row(n_ref)
                e1_row = row(e1_ref)
                for q in range(groups):
                    rows = slice(q * BF16_ROWS, (q + 1) * BF16_ROWS)
                    term = jnp.where(rho_ref[h, rows, :] < n_row, e2_ref[h, rows, :], 0.0) * e1_row
                    g[q] = term if g[q] is None else g[q] + term
            gates.extend(g)
        p_halves.append(_gelu(act).astype(BF16) * jnp.concatenate(gates, axis=0))
    o_ref[...] += _dot(vt_ref[...], jnp.concatenate(p_halves, axis=0))


def _peer(hm_t, u_b, vt_b, n_cnt, e1, rho, e2):
    n = hm_t.shape[1]
    n_exp = u_b.shape[0]
    tn, te = PEER_TOK, PEER_EXP
    big = pl.BlockSpec((P_HEADS, P_KEYS, tn), lambda i, j: (0, 0, i))
    a_rows = pl.BlockSpec((P_HEADS, te // P_KEYS, tn), lambda i, j: (0, j, i))
    return pl.pallas_call(
        _peer_kernel,
        grid=(n // tn, n_exp // te),
        in_specs=[pl.BlockSpec((D, tn), lambda i, j: (0, i)),
                  pl.BlockSpec((te, D), lambda i, j: (j, 0)),
                  pl.BlockSpec((D, te), lambda i, j: (0, j)),
                  a_rows, a_rows, big, big],
        out_specs=pl.BlockSpec((D, tn), lambda i, j: (0, i)),
        out_shape=jax.ShapeDtypeStruct((D, n), F32),
        compiler_params=_cparams(("parallel", "arbitrary"), 52),
        name="peer",
    )(hm_t, u_b, vt_b, n_cnt, e1, rho, e2)


def _final_kernel(h1_ref, f_ref, m5_ref, w_ref, o_ref):
    h = h1_ref[0] + m5_ref[0] * f_ref[0]
    o_ref[0] = h * lax.rsqrt(jnp.mean(h * h, axis=-1, keepdims=True) + EPS) * w_ref[...]


def _final(h1, f, m5, w):
    b, tl, _ = h1.shape
    lat = pl.BlockSpec((1, ROWS, D), lambda i, j: (i, j, 0))
    return pl.pallas_call(
        _final_kernel,
        grid=(b, tl // ROWS),
        in_specs=[lat, lat, pl.BlockSpec((1, 1, D), lambda i, j: (i, 0, 0)),
                  pl.BlockSpec((1, D), lambda i, j: (0, 0))],
        out_specs=lat,
        out_shape=jax.ShapeDtypeStruct((b, tl, D), F32),
        compiler_params=_cparams(("parallel", "parallel"), 24),
        name="final",
    )(h1, f, m5, w)


def _reorder_cols(w):
    gla_main, gla_lora = w[..., 0:3072], w[..., 3072:3104]
    rw_main, rw_lora = w[..., 3104:6176], w[..., 6176:6528]
    gates = w[..., 6528:8576]
    pad = jnp.zeros(w.shape[:-1] + (512 - 32 - 352,), w.dtype)
    return jnp.concatenate([rw_main, gla_main, gates, gla_lora, rw_lora, pad], axis=-1)


def _to_chains(a):
    b, t, _ = a.shape
    return a.reshape(b, t, R_HEADS, R_N).transpose(1, 3, 0, 2).reshape(t, R_N, b * R_HEADS)


def _from_chains(a, b):
    t = a.shape[0]
    return a.reshape(t, R_N, b, R_HEADS).transpose(2, 0, 3, 1).reshape(b, t, D)


def kernel(x, c, ctx, c_ctx, norm1_w, w_mod, b_mod, w_in, gla_w_a2, gla_b_a, gla_norm_w, rwkv_mu, rwkv_w0, rwkv_w2, rwkv_a0, rwkv_a2, rwkv_g2, rwkv_k_k, rwkv_k_a, rwkv_r_k, rwkv_ln_w, rwkv_ln_b, w_out, norm2_w, peer_w_q, peer_sub_keys, peer_u, peer_v, final_norm_w):
    b, seq, _ = x.shape
    assert w_in.shape[0] == 1 and ctx.shape[1] == N_CTX and seq % ROWS == 0
    row = lambda v: v.reshape(1, -1)

    c16 = jnp.zeros((16, D), F32).at[:b].set(c).at[b].set(c_ctx)
    m = _mod(c16, w_mod[0], row(b_mod[0])).reshape(16, N_MOD, D)
    m_lat, m_ctx = m[:b], m[b]
    per_b = lambda i: m_lat[:, i].reshape(b, 1, D)
    shift1 = jnp.stack([jnp.broadcast_to(m_ctx[0], (b, D)), m_lat[:, 0]], axis=1).reshape(b, 2, 1, D)
    scale1 = jnp.stack([jnp.broadcast_to(m_ctx[1], (b, D)), m_lat[:, 1]], axis=1).reshape(b, 2, 1, D)

    h_all = jnp.concatenate([ctx, x], axis=1)
    z = _inproj(h_all, row(norm1_w[0]), scale1, shift1, _reorder_cols(w_in[0]).astype(BF16))

    mu = rwkv_mu[0]
    mu_lora = jnp.concatenate([jnp.zeros((32,), F32), mu[3072:], jnp.zeros((128,), F32)]).reshape(1, 512)
    e = jnp.repeat(jnp.eye(R_HEADS, dtype=F32), R_N, axis=0)
    w2f = jnp.zeros((128, D), F32).at[L_WF:L_WF + R_W_LORA].set(rwkv_w2[0, 0])
    w2b = jnp.zeros((256, D), F32).at[L_WB:L_WB + R_W_LORA].set(rwkv_w2[0, 1])
    a2 = jnp.zeros((128, D), F32).at[L_A - 128:L_A - 128 + R_A_LORA].set(rwkv_a2[0])
    g2 = jnp.zeros((256, D), F32).at[L_G - 128:L_G - 128 + R_G_LORA].set(rwkv_g2[0])
    wa = jnp.zeros((2, 128, G_HEADS * G_DK), F32)
    wa = wa.at[0, 0:G_LORA].set(gla_w_a2[0, 0]).at[1, G_LORA:2 * G_LORA].set(gla_w_a2[0, 1])
    r, k2, vr, kk, kka, d_f, d_b, bonus, gate_r = _rprep(
        z, row(mu[:3072]), mu_lora, rwkv_w0[0].reshape(2, 1, D), w2f, w2b, row(rwkv_a0[0]), a2,
        g2, row(rwkv_k_k[0]), row(rwkv_k_a[0]), row(rwkv_r_k[0]), e, e.T)

    o_r = _rwkv(_to_chains(r), _to_chains(k2), _to_chains(vr), _to_chains(kk), _to_chains(kka),
                jnp.stack([_to_chains(d_f), _to_chains(d_b)]))
    or_f, or_b = _from_chains(o_r[0], b), _from_chains(o_r[1], b)

    og_f, og_b = _gla(z, wa, gla_b_a[0].reshape(2, 1, -1))

    h1, hm_b = _mix(og_f, og_b, z, or_f, or_b, bonus, gate_r, x, row(gla_norm_w[0]), row(rwkv_ln_w[0]),
                        row(rwkv_ln_b[0]), e, e.T, w_out[0].astype(BF16), per_b(2), row(norm2_w[0]),
                        per_b(3), per_b(4))

    n_tok = b * seq
    hm_t = hm_b.reshape(n_tok, D).T
    n_cnt, e1, rho, e2 = _route(hm_t, peer_w_q[0].T.astype(BF16),
                                peer_sub_keys[0].reshape(2 * P_HEADS, P_KEYS, -1).astype(BF16))
    f_t = _peer(hm_t, peer_u[0].astype(BF16), peer_v[0].T.astype(BF16), n_cnt, e1, rho, e2)
    return _final(h1, f_t.T.reshape(b, seq, D), per_b(5), row(final_norm_w))
```

```python
import functools

import numpy as np
import jax
import jax.numpy as jnp
from jax import lax
from jax.experimental import pallas as pl
from jax.experimental.pallas import tpu as pltpu

F32 = jnp.float32
BF16 = jnp.bfloat16
HI = lax.Precision.HIGHEST

D = 1024
GRID_COLS = 64
N_CTX = 256
EPS = 1e-6
N_MOD = 6

G_HEADS = 4
G_DV = D // G_HEADS
G_DK = G_DV // 2
G_LORA = 16
G_GATE_NORM = 16.0
G_CHUNK = 64

R_N = 64
R_HEADS = D // R_N
R_LN_EPS = 64e-5
R_W_LORA = 64
R_A_LORA = 64
R_G_LORA = 160

P_HEADS = 8
P_KEYS = 128
P_TOPK = 16

C_RWKV = 0
C_GLA = 3072
C_GATE = 6144
C_LORA = 8192
N_PROJ = 8704
L_WF, L_WB, L_A, L_G = 32, 96, 160, 224

ROWS = 256
V7X_VMEM_BYTES = 64 * 1024 * 1024


def _cparams(sem, vmem_mb):
    return pltpu.CompilerParams(dimension_semantics=sem,
                                vmem_limit_bytes=min(vmem_mb * 1024 * 1024, V7X_VMEM_BYTES - (8 << 20)))


def _sigmoid(x):
    return 1.0 / (1.0 + jnp.exp(-x))


def _softplus(x):
    return jnp.maximum(x, 0.0) + jnp.log(1.0 + jnp.exp(-jnp.abs(x)))


def _dot(a, b, precision=None):
    return jnp.dot(a, b, preferred_element_type=F32, precision=precision)


def _dot_nt(a, b, precision=None):
    return lax.dot_general(a, b, (((1,), (1,)), ((), ())), preferred_element_type=F32, precision=precision)


def _dot_tn(a, b, precision=None):
    return lax.dot_general(a, b, (((0,), (0,)), ((), ())), preferred_element_type=F32, precision=precision)


def _split2(x):
    hi = x.astype(BF16)
    return hi, (x - hi.astype(F32)).astype(BF16)


def _dot_x2(x, w):
    hi, lo = _split2(x)
    return _dot(hi, w) + _dot(lo, w)


def _dot_2x(w, x):
    hi, lo = _split2(x)
    return _dot(w, hi) + _dot(w, lo)


def _dot_x3(x, w_ref):
    hi, lo = _split2(x)
    return _dot(hi, w_ref[0]) + (_dot(lo, w_ref[0]) + _dot(hi, w_ref[1]))


def _mod_kernel(c_ref, w_ref, b_ref, o_ref):
    c = c_ref[...]
    s = c * _sigmoid(c)
    o_ref[...] = _dot(s.astype(BF16), w_ref[...].astype(BF16)) + b_ref[...]


def _mod(c16, w_mod, b_mod):
    n = w_mod.shape[1]
    tn = 768
    return pl.pallas_call(
        _mod_kernel,
        grid=(n // tn,),
        in_specs=[pl.BlockSpec((16, D), lambda j: (0, 0)),
                  pl.BlockSpec((D, tn), lambda j: (0, j)),
                  pl.BlockSpec((1, tn), lambda j: (0, j))],
        out_specs=pl.BlockSpec((16, tn), lambda j: (0, j)),
        out_shape=jax.ShapeDtypeStruct((16, n), F32),
        compiler_params=_cparams(("parallel",), 24),
        name="mod",
    )(c16, w_mod, b_mod)


INPROJ_ROWS = 768


def _inproj_kernel(h_ref, nw_ref, sc_ref, sh_ref, w_ref, o_ref, a_scr):
    @pl.when(pl.program_id(2) == 0)
    def _():
        x = h_ref[0]
        y = x * lax.rsqrt(jnp.mean(x * x, axis=-1, keepdims=True) + EPS) * nw_ref[...]
        t = pl.program_id(1) * INPROJ_ROWS + lax.broadcasted_iota(jnp.int32, (INPROJ_ROWS, 1), 0)
        is_ctx = t < N_CTX
        sc = jnp.where(is_ctx, sc_ref[0, 0], sc_ref[0, 1])
        sh = jnp.where(is_ctx, sh_ref[0, 0], sh_ref[0, 1])
        a_scr[...] = (y * (1.0 + sc) + sh).astype(BF16)

    o_ref[0] = _dot(a_scr[...], w_ref[...])


def _inproj(h_all, norm_w, scale, shift, w_p):
    b, t, _ = h_all.shape
    tn = 512
    mod = pl.BlockSpec((1, 2, 1, D), lambda i, j, n: (i, 0, 0, 0))
    return pl.pallas_call(
        _inproj_kernel,
        grid=(b, t // INPROJ_ROWS, N_PROJ // tn),
        in_specs=[pl.BlockSpec((1, INPROJ_ROWS, D), lambda i, j, n: (i, j, 0)),
                  pl.BlockSpec((1, D), lambda i, j, n: (0, 0)),
                  mod, mod,
                  pl.BlockSpec((D, tn), lambda i, j, n: (0, n))],
        out_specs=pl.BlockSpec((1, INPROJ_ROWS, tn), lambda i, j, n: (i, j, n)),
        out_shape=jax.ShapeDtypeStruct((b, t, N_PROJ), F32),
        scratch_shapes=[pltpu.VMEM((INPROJ_ROWS, D), BF16)],
        compiler_params=_cparams(("parallel", "parallel", "arbitrary"), 32),
        name="inproj",
    )(h_all, norm_w, scale, shift, w_p)


def _shift_mix(zc, zp, zn, mu, is_ctx):
    rows, w = zc.shape
    row = lax.broadcasted_iota(jnp.int32, (rows, w), 0)
    lane = lax.broadcasted_iota(jnp.int32, (rows, w), 1)
    pmask = jnp.where(is_ctx, 1, 3)
    rmask = jnp.where(is_ctx, rows - 1, GRID_COLS - 1)
    cls = lane & pmask
    rr = row & rmask
    prev = pltpu.roll(zc, 1, 0)
    nxt = pltpu.roll(zc, rows - 1, 0)
    up = jnp.concatenate([zp, zc[:rows - GRID_COLS]], axis=0)
    down = jnp.concatenate([zc[GRID_COLS:], zn], axis=0)
    zs = jnp.where((cls == 0) & (rr != 0), prev,
                   jnp.where((cls == 1) & (rr != rmask), nxt,
                             jnp.where(cls == 2, up, jnp.where(cls == 3, down, 0.0))))
    return zc + (zs - zc) * mu


def _rprep_kernel(zc_ref, zp_ref, zn_ref, lc_ref, lp_ref, ln_ref,
                  mu_ref, mul_ref, w0_ref, w2f_ref, w2b_ref, a0_ref, a2_ref, g2_ref,
                  kk_w_ref, ka_w_ref, rk_w_ref, e_ref, et_ref,
                  r_o, k_o, v_o, kk_o, kka_o, df_o, db_o, bonus_o, gate_o):
    tb = pl.program_id(1)
    ntb = pl.num_programs(1)
    is_ctx = tb == 0
    has_up = (tb > 1).astype(F32)
    has_down = (tb < ntb - 1).astype(F32)

    zl = _shift_mix(lc_ref[0], lp_ref[0] * has_up, ln_ref[0] * has_down, mul_ref[...], is_ctx)
    def decay(window, w2p_ref, d):
        w = -_softplus(-(w0_ref[d] + _dot_x3(jnp.tanh(window), w2p_ref))) - 0.5
        return -jnp.exp(w)

    df_o[0] = decay(zl[:, 0:128], w2f_ref, 0)
    db_o[0] = decay(zl[:, 0:256], w2b_ref, 1)
    a = _sigmoid(a0_ref[...] + _dot_x3(zl[:, 128:256], a2_ref))
    gate_o[0] = _dot_x3(_sigmoid(zl[:, 128:384]), g2_ref)

    def piece(i):
        sl = slice(i * D, (i + 1) * D)
        return _shift_mix(zc_ref[0, :, sl], zp_ref[0, :, sl] * has_up, zn_ref[0, :, sl] * has_down,
                          mu_ref[:, sl], is_ctx)

    r = piece(0)
    kr = piece(1)
    vr = piece(2)
    e = e_ref[...]
    et = et_ref[...]
    kkraw = kr * kk_w_ref[...]
    nrm = jnp.sqrt(_dot_x2(kkraw * kkraw, e))
    inv = 1.0 / jnp.maximum(nrm, 1e-12)
    kk = kkraw * _dot_x2(inv, et)
    k2 = kr * (1.0 + (a - 1.0) * ka_w_ref[...])
    rk = _dot_x2(r * k2 * rk_w_ref[...], e)
    r_o[0] = r.astype(BF16)
    k_o[0] = k2.astype(BF16)
    v_o[0] = vr.astype(BF16)
    kk_o[0] = kk.astype(BF16)
    kka_o[0] = (kk * a).astype(BF16)
    bonus_o[0] = _dot_x2(rk, et) * vr


def _rprep(z, mu_main, mu_lora, w0, w2f, w2b, a0, a2, g2, k_k, k_a, r_k, e, et):
    b, t, _ = z.shape
    ntb = t // ROWS
    q = ROWS // GRID_COLS
    last64 = t // GRID_COLS - 1
    wm = 3 * D
    cm = C_RWKV // wm
    cl = C_LORA // 512
    cur = lambda c: (lambda i, j: (i, j, c))
    prv = lambda c: (lambda i, j: (i, jnp.maximum(j * q - 1, 0), c))
    nxt = lambda c: (lambda i, j: (i, jnp.minimum(j * q + q, last64), c))
    full = lambda *s: pl.BlockSpec(s, lambda i, j: (0,) * len(s))
    out_spec = pl.BlockSpec((1, ROWS, D), lambda i, j: (i, j, 0))
    out_sd = jax.ShapeDtypeStruct((b, t, D), F32)
    return pl.pallas_call(
        _rprep_kernel,
        grid=(b, ntb),
        in_specs=[pl.BlockSpec((1, ROWS, wm), cur(cm)),
                  pl.BlockSpec((1, GRID_COLS, wm), prv(cm)),
                  pl.BlockSpec((1, GRID_COLS, wm), nxt(cm)),
                  pl.BlockSpec((1, ROWS, 512), cur(cl)),
                  pl.BlockSpec((1, GRID_COLS, 512), prv(cl)),
                  pl.BlockSpec((1, GRID_COLS, 512), nxt(cl)),
                  full(1, wm), full(1, 512), full(2, 1, D), full(2, 128, D), full(2, 256, D),
                  full(1, D), full(2, 128, D), full(2, 256, D),
                  full(1, D), full(1, D), full(1, D), full(D, R_HEADS), full(R_HEADS, D)],
        out_specs=[out_spec] * 9,
        out_shape=[jax.ShapeDtypeStruct((b, t, D), BF16)] * 5 + [out_sd] * 4,
        compiler_params=_cparams(("parallel", "parallel"), 52),
        name="rprep",
    )(z, z, z, z, z, z, mu_main, mu_lora, w0, w2f, w2b, a0, a2, g2, k_k, k_a, r_k, e, et)


R_STEPS = 64


def _rwkv_kernel(r_ref, k_ref, v_ref, kk_ref, kka_ref, lwf_ref, lwb_ref, o_ref, s_ref, rows_ref):
    dirn = pl.program_id(0)
    tix = lambda i: jnp.where(dirn == 0, i, R_STEPS - 1 - i)

    @pl.when(pl.program_id(1) == 0)
    def _():
        s_ref[...] = jnp.zeros_like(s_ref)

    def weighted_sum(rows):
        acc = s_ref[0] * rows_ref[rows, 0:1, :]
        for k in range(1, R_N):
            acc = acc + s_ref[k] * rows_ref[rows, k:k + 1, :]
        return acc

    rows_ref[0] = kk_ref[tix(0)].astype(F32)
    sa0 = weighted_sum(0)

    def step(i, carry):
        sa, lcum = carry
        t = tix(i)
        t_next = tix(jnp.minimum(i + 1, R_STEPS - 1))
        lcum = lcum + jnp.where(dirn == 0, lwf_ref[t], lwb_ref[t])
        e_pos = jnp.exp(lcum)
        e_neg = jnp.exp(-lcum)
        rows_ref[0] = kk_ref[t_next].astype(F32) * e_pos
        rows_ref[1] = kka_ref[t].astype(F32) * e_neg
        rows_ref[2] = k_ref[t].astype(F32) * e_neg
        rows_ref[3] = r_ref[t].astype(F32) * e_pos
        vv = v_ref[t].astype(F32)
        out = None
        sa_next = None
        for k in range(R_N):
            sk = s_ref[k] - sa * rows_ref[1, k:k + 1, :] + vv * rows_ref[2, k:k + 1, :]
            s_ref[k] = sk
            o_term = sk * rows_ref[3, k:k + 1, :]
            s_term = sk * rows_ref[0, k:k + 1, :]
            out = o_term if out is None else out + o_term
            sa_next = s_term if sa_next is None else sa_next + s_term
        o_ref[0, t] = out.astype(BF16)
        return sa_next, lcum

    _, lcum = lax.fori_loop(0, R_STEPS, step, (sa0, jnp.zeros((R_N, s_ref.shape[2]), F32)))
    rows_ref[0] = jnp.exp(lcum)
    for k in range(R_N):
        s_ref[k] = s_ref[k] * rows_ref[0, k:k + 1, :]


def _rwkv_order(dirn, nb, n_ctx_blk, n_blk):
    back = jnp.where(nb < n_ctx_blk, n_ctx_blk - 1 - nb, n_blk + n_ctx_blk - 1 - nb)
    return jnp.where(dirn == 0, nb, back)


def _rwkv(r_t, k_t, v_t, kk_t, kka_t, lwf_t, lwb_t):
    t, _, nch = r_t.shape
    n_blk = t // R_STEPS
    n_ctx_blk = N_CTX // R_STEPS
    blk = lambda d, n: (_rwkv_order(d, n, n_ctx_blk, n_blk), 0, 0)
    blk4 = lambda d, n: (d, _rwkv_order(d, n, n_ctx_blk, n_blk), 0, 0)
    only = lambda own: (lambda d, n: (jnp.where(d == own, _rwkv_order(d, n, n_ctx_blk, n_blk), 0), 0, 0))
    spec = pl.BlockSpec((R_STEPS, R_N, nch), blk)
    spec4 = pl.BlockSpec((1, R_STEPS, R_N, nch), blk4)
    return pl.pallas_call(
        _rwkv_kernel,
        grid=(2, n_blk),
        in_specs=[spec, spec, spec, spec, spec,
                  pl.BlockSpec((R_STEPS, R_N, nch), only(0)), pl.BlockSpec((R_STEPS, R_N, nch), only(1))],
        out_specs=spec4,
        out_shape=jax.ShapeDtypeStruct((2, t, R_N, nch), BF16),
        scratch_shapes=[pltpu.VMEM((R_N, R_N, nch), F32), pltpu.VMEM((4, R_N, nch), F32)],
        compiler_params=_cparams(("arbitrary", "arbitrary"), 48),
        name="rwkv",
    )(r_t, k_t, v_t, kk_t, kka_t, lwf_t, lwb_t)


def _log_sigmoid(x):
    return jnp.minimum(x, 0.0) - jnp.log(1.0 + jnp.exp(-jnp.abs(x)))


def _gla_dir(q, k, v, alo, wa, ba, st_ref, reverse):
    c = q.shape[0]
    g = _log_sigmoid(_dot_x3(alo, wa) + ba) * (1.0 / G_GATE_NORM)
    row = lax.broadcasted_iota(jnp.int32, (c, c), 0)
    col = lax.broadcasted_iota(jnp.int32, (c, c), 1)
    tri = (row <= col) if reverse else (row >= col)
    bcum = _dot_2x(tri.astype(BF16), g)
    blast = bcum[0:1] if reverse else bcum[c - 1:c]
    qd = (q * (G_DK ** -0.5) * jnp.exp(bcum)).astype(BF16)
    kd = (k * jnp.exp(-bcum)).astype(BF16)
    kl = (k * jnp.exp(blast - bcum)).astype(BF16)
    eb = jnp.exp(blast)
    vb = v.astype(BF16)
    outs = []
    for h in range(G_HEADS):
        sk = slice(h * G_DK, (h + 1) * G_DK)
        sv = slice(h * G_DV, (h + 1) * G_DV)
        att = jnp.where(tri, _dot_nt(qd[:, sk], kd[:, sk]), 0.0)
        st = st_ref[h]
        outs.append(_dot_nt(qd[:, sk], st.astype(BF16)) + _dot(att.astype(BF16), vb[:, sv]))
        st_ref[h] = st * eb[:, sk] + _dot_tn(vb[:, sv], kl[:, sk])
    return jnp.concatenate(outs, axis=-1)


def _gla_kernel(qf, kf, vf, lf, qb, kb, vb, lb, wa_ref, ba_ref, of_ref, ob_ref, sf_ref, sb_ref):
    @pl.when(pl.program_id(1) == 0)
    def _():
        sf_ref[...] = jnp.zeros_like(sf_ref)
        sb_ref[...] = jnp.zeros_like(sb_ref)

    of_ref[0] = _gla_dir(qf[0], kf[0], vf[0], lf[0], wa_ref.at[0], ba_ref[0], sf_ref, False)
    ob_ref[0] = _gla_dir(qb[0], kb[0], vb[0], lb[0], wa_ref.at[1], ba_ref[1], sb_ref, True)


def _gla(z, w_a2, b_a):
    b, t, _ = z.shape
    n_blk = t // G_CHUNK
    n_ctx_blk = N_CTX // G_CHUNK
    fwd = lambda n: n
    bwd = lambda n: _rwkv_order(1, n, n_ctx_blk, n_blk)
    wk = G_HEADS * G_DK
    cq, ck, cv = C_GLA // wk, (C_GLA + wk) // wk, (C_GLA + 2 * wk) // D
    cl = C_LORA // 128

    def specs(order):
        return [pl.BlockSpec((1, G_CHUNK, wk), lambda i, n: (i, order(n), cq)),
                pl.BlockSpec((1, G_CHUNK, wk), lambda i, n: (i, order(n), ck)),
                pl.BlockSpec((1, G_CHUNK, D), lambda i, n: (i, order(n), cv)),
                pl.BlockSpec((1, G_CHUNK, 128), lambda i, n: (i, order(n), cl))]

    out_sd = jax.ShapeDtypeStruct((b, t, D), F32)
    state = pltpu.VMEM((G_HEADS, G_DV, G_DK), F32)
    return pl.pallas_call(
        _gla_kernel,
        grid=(b, n_blk),
        in_specs=specs(fwd) + specs(bwd) + [
            pl.BlockSpec((2, 2, 128, wk), lambda i, n: (0, 0, 0, 0)),
            pl.BlockSpec((2, 1, wk), lambda i, n: (0, 0, 0))],
        out_specs=[pl.BlockSpec((1, G_CHUNK, D), lambda i, n: (i, fwd(n), 0)),
                   pl.BlockSpec((1, G_CHUNK, D), lambda i, n: (i, bwd(n), 0))],
        out_shape=[out_sd, out_sd],
        scratch_shapes=[state, state],
        compiler_params=_cparams(("parallel", "arbitrary"), 24),
        name="gla",
    )(z, z, z, z, z, z, z, z, w_a2, b_a)


def _mix_kernel(ogf_ref, ogb_ref, gout_ref, orf_ref, orb_ref, bonus_ref, gr_ref, gg_ref, grw_ref, x_ref,
                gnw_ref, lnw_ref, lnb_ref, e_ref, et_ref, wout_ref, m2_ref, n2w_ref, m3_ref, m4_ref,
                h1_o, hmb_o):
    og = ogf_ref[0] + ogb_ref[0]
    parts = []
    for h in range(G_HEADS):
        seg = og[:, h * G_DV:(h + 1) * G_DV]
        parts.append(seg * lax.rsqrt(jnp.mean(seg * seg, axis=-1, keepdims=True) + EPS))
    gout = gout_ref[0]
    y_gla = jnp.concatenate(parts, axis=-1) * gnw_ref[...] * (gout * _sigmoid(gout))

    e = e_ref[...]
    et = et_ref[...]
    o_r = orf_ref[0].astype(F32) + orb_ref[0].astype(F32)
    mu = _dot_x2(_dot_x2(o_r, e) * (1.0 / R_N), et)
    dlt = o_r - mu
    var = _dot_x2(dlt * dlt, e) * (1.0 / R_N)
    o_r = dlt * _dot_x2(lax.rsqrt(var + R_LN_EPS), et) * lnw_ref[...] + lnb_ref[...] + bonus_ref[0]
    y_rwkv = o_r * gr_ref[0]

    y = _sigmoid(gg_ref[0]) * y_gla + _sigmoid(grw_ref[0]) * y_rwkv
    h1 = x_ref[0] + m2_ref[0] * _dot(y.astype(BF16), wout_ref[...])
    h1_o[0] = h1
    hn = h1 * lax.rsqrt(jnp.mean(h1 * h1, axis=-1, keepdims=True) + EPS) * n2w_ref[...]
    hmb_o[0] = (hn * (1.0 + m4_ref[0]) + m3_ref[0]).astype(BF16)


def _mix(og_f, og_b, z, or_f, or_b, bonus, gate_r, x, gnw, lnw, lnb, e, et, w_out, m2, n2w, m3, m4):
    b, tl, _ = x.shape
    off = N_CTX // ROWS
    seq = lambda c: pl.BlockSpec((1, ROWS, D), lambda i, j: (i, j + off, c))
    full = lambda *s: pl.BlockSpec(s, lambda i, j: (0,) * len(s))
    per_b = pl.BlockSpec((1, 1, D), lambda i, j: (i, 0, 0))
    lat = pl.BlockSpec((1, ROWS, D), lambda i, j: (i, j, 0))
    return pl.pallas_call(
        _mix_kernel,
        grid=(b, tl // ROWS),
        in_specs=[seq(0), seq(0), seq((C_GLA + 2048) // D), seq(0), seq(0), seq(0), seq(0),
                  seq(C_GATE // D), seq(C_GATE // D + 1), lat,
                  full(1, D), full(1, D), full(1, D), full(D, R_HEADS), full(R_HEADS, D), full(D, D),
                  per_b, full(1, D), per_b, per_b],
        out_specs=[lat, lat],
        out_shape=[jax.ShapeDtypeStruct((b, tl, D), F32), jax.ShapeDtypeStruct((b, tl, D), BF16)],
        compiler_params=_cparams(("parallel", "parallel"), 52),
        name="mix",
    )(og_f, og_b, z, or_f, or_b, bonus, gate_r, z, z, x, gnw, lnw, lnb, e, et, w_out, m2, n2w, m3, m4)


ROUTE_TOK = 512


N_CAND = 80


def _dup_bf16(x):
    hi = lax.bitcast_convert_type(x.astype(BF16).astype(F32), jnp.uint32)
    return hi | (hi >> 16)


def _route_kernel(x_ref, wq_ref, keys_ref, n_o, e1_o, rho_o, e2_o,
                  q_scr, s_scr, t1_scr, t2_scr, cand_scr, rho_scr, cnt_scr):
    q_scr[...] = _dot(wq_ref[...], x_ref[...]).astype(BF16)
    ninf = -jnp.inf
    half = P_TOPK // 2
    tn = x_ref.shape[1]
    top_row = jnp.full((1, tn), jnp.inf, F32)

    def next_below(vals, bound):
        below = vals < bound
        return below, jnp.max(jnp.where(below, vals, ninf), axis=0, keepdims=True)

    def head(h, carry):
        q1 = q_scr[pl.ds(pl.multiple_of(h * 2 * P_KEYS, P_KEYS), P_KEYS), :]
        q2 = q_scr[pl.ds(pl.multiple_of(h * 2 * P_KEYS + P_KEYS, P_KEYS), P_KEYS), :]
        s_scr[0] = _dot(keys_ref[2 * h], q1)
        s_scr[1] = _dot(keys_ref[2 * h + 1], q2)
        rho_scr[...] = jnp.full(rho_scr.shape, -1.0, F32)

        def extract(i, bounds):
            m1, m2 = bounds
            _, m1 = next_below(s_scr[0], m1)
            below2, m2 = next_below(s_scr[1], m2)
            rho_scr[...] += jnp.where(below2, 1.0, 0.0)
            t1_scr[pl.ds(i, 1), :] = m1
            t2_scr[pl.ds(i, 1), :] = m2
            return m1, m2

        _, last2 = lax.fori_loop(0, P_TOPK, extract, (top_row, top_row))
        s1 = s_scr[0]
        s2 = s_scr[1]
        t1 = t1_scr[...]
        t2 = t2_scr[...]
        cand_scr[0:P_TOPK] = t1[0:1] + t2
        for i in range(1, half):
            cand_scr[P_TOPK + half * (i - 1):P_TOPK + half * i] = t1[i:i + 1] + t2[0:half]
        cand_scr[N_CAND - half:N_CAND] = t1[half:P_TOPK] + t2[0:1]
        cmax = t1[0:1] + t2[0:1]

        def pick(i, st):
            m, z = st
            _, m = next_below(cand_scr[...], m)
            return m, z + jnp.exp(m - cmax)

        tau, zsum = lax.fori_loop(0, P_TOPK, pick, (top_row, jnp.zeros_like(cmax)))
        cnt_scr[...] = jnp.zeros_like(cnt_scr)

        def count(j, c):
            cnt_scr[...] += jnp.where(s_scr[0] + t2_scr[pl.ds(j, 1), :] >= tau, 1.0, 0.0)
            return c

        lax.fori_loop(0, P_TOPK, count, 0)
        sel1 = s1 >= t1[P_TOPK - 1:P_TOPK]
        sel2 = s2 >= last2
        rho = rho_scr[...] + jnp.where(s2 < last2, 1.0, 0.0)
        n_o[h] = _dup_bf16(jnp.where(sel1, cnt_scr[...], 0.0))
        e1_o[h] = _dup_bf16(jnp.where(sel1, jnp.exp(s1 - t1[0:1]), 0.0))
        rho_o[h] = rho.astype(BF16)
        e2_o[h] = jnp.where(sel2, jnp.exp(s2 - t2[0:1]) / zsum, 0.0).astype(BF16)
        return carry

    lax.fori_loop(0, P_HEADS, head, 0)


def _route(hm_t, wq_t, keys):
    n = hm_t.shape[1]
    tn = ROUTE_TOK
    big = pl.BlockSpec((P_HEADS, P_KEYS, tn), lambda i: (0, 0, i))
    sd = lambda dt: jax.ShapeDtypeStruct((P_HEADS, P_KEYS, n), dt)
    return pl.pallas_call(
        _route_kernel,
        grid=(n // tn,),
        in_specs=[pl.BlockSpec((D, tn), lambda i: (0, i)),
                  pl.BlockSpec((2 * P_HEADS * P_KEYS, D), lambda i: (0, 0)),
                  pl.BlockSpec((2 * P_HEADS, P_KEYS, P_KEYS), lambda i: (0, 0, 0))],
        out_specs=[big, big, big, big],
        out_shape=[sd(jnp.uint32), sd(jnp.uint32), sd(BF16), sd(BF16)],
        scratch_shapes=[pltpu.VMEM((2 * P_HEADS * P_KEYS, tn), BF16), pltpu.VMEM((2, P_KEYS, tn), F32),
                        pltpu.VMEM((P_TOPK, tn), F32), pltpu.VMEM((P_TOPK, tn), F32),
                        pltpu.VMEM((N_CAND, tn), F32),
                        pltpu.VMEM((P_KEYS, tn), F32), pltpu.VMEM((P_KEYS, tn), F32)],
        compiler_params=_cparams(("parallel",), 48),
        name="route",
    )(hm_t, wq_t, keys)


PEER_TOK = 512
PEER_EXP = 1024


def _gelu(x):
    return 0.5 * x * (1.0 + lax.erf(x * (2.0 ** -0.5)))


BF16_ROWS = 16


def _peer_kernel(x_ref, u_ref, vt_ref, n_ref, e1_ref, rho_ref, e2_ref, o_ref):
    j = pl.program_id(1)

    @pl.when(j == 0)
    def _():
        o_ref[...] = jnp.zeros_like(o_ref)

    tn = x_ref.shape[1]
    x = x_ref[...]
    n_a = PEER_EXP // P_KEYS
    groups = P_KEYS // BF16_ROWS
    halves = 2
    a_per_half = n_a // halves
    p_halves = []
    for half in range(halves):
        rows_h = a_per_half * P_KEYS
        act = _dot(u_ref[half * rows_h:(half + 1) * rows_h, :], x)
        gates = []
        for al in range(half * a_per_half, (half + 1) * a_per_half):
            g = [None] * groups
            for h in range(P_HEADS):
                row = lambda ref: pltpu.bitcast(jnp.broadcast_to(ref[h, al:al + 1, :], (8, tn)), BF16)
                n_row = row(n_ref)
                e1_row = row(e1_ref)
                for q in range(groups):
                    rows = slice(q * BF16_ROWS, (q + 1) * BF16_ROWS)
                    term = jnp.where(rho_ref[h, rows, :] < n_row, e2_ref[h, rows, :], 0.0) * e1_row
                    g[q] = term if g[q] is None else g[q] + term
            gates.extend(g)
        p_halves.append(_gelu(act).astype(BF16) * jnp.concatenate(gates, axis=0))
    o_ref[...] += _dot(vt_ref[...], jnp.concatenate(p_halves, axis=0))


def _peer(hm_t, u_b, vt_b, n_cnt, e1, rho, e2):
    n = hm_t.shape[1]
    n_exp = u_b.shape[0]
    tn, te = PEER_TOK, PEER_EXP
    big = pl.BlockSpec((P_HEADS, P_KEYS, tn), lambda i, j: (0, 0, i))
    a_rows = pl.BlockSpec((P_HEADS, te // P_KEYS, tn), lambda i, j: (0, j, i))
    return pl.pallas_call(
        _peer_kernel,
        grid=(n // tn, n_exp // te),
        in_specs=[pl.BlockSpec((D, tn), lambda i, j: (0, i)),
                  pl.BlockSpec((te, D), lambda i, j: (j, 0)),
                  pl.BlockSpec((D, te), lambda i, j: (0, j)),
                  a_rows, a_rows, big, big],
        out_specs=pl.BlockSpec((D, tn), lambda i, j: (0, i)),
        out_shape=jax.ShapeDtypeStruct((D, n), F32),
        compiler_params=_cparams(("parallel", "arbitrary"), 52),
        name="peer",
    )(hm_t, u_b, vt_b, n_cnt, e1, rho, e2)


def _final_kernel(h1_ref, f_ref, m5_ref, w_ref, o_ref):
    h = h1_ref[0] + m5_ref[0] * f_ref[0]
    o_ref[0] = h * lax.rsqrt(jnp.mean(h * h, axis=-1, keepdims=True) + EPS) * w_ref[...]


def _final(h1, f, m5, w):
    b, tl, _ = h1.shape
    lat = pl.BlockSpec((1, ROWS, D), lambda i, j: (i, j, 0))
    return pl.pallas_call(
        _final_kernel,
        grid=(b, tl // ROWS),
        in_specs=[lat, lat, pl.BlockSpec((1, 1, D), lambda i, j: (i, 0, 0)),
                  pl.BlockSpec((1, D), lambda i, j: (0, 0))],
        out_specs=lat,
        out_shape=jax.ShapeDtypeStruct((b, tl, D), F32),
        compiler_params=_cparams(("parallel", "parallel"), 24),
        name="final",
    )(h1, f, m5, w)


def _reorder_cols(w):
    gla_main, gla_lora = w[..., 0:3072], w[..., 3072:3104]
    rw_main, rw_lora = w[..., 3104:6176], w[..., 6176:6528]
    gates = w[..., 6528:8576]
    pad = jnp.zeros(w.shape[:-1] + (512 - 32 - 352,), w.dtype)
    return jnp.concatenate([rw_main, gla_main, gates, gla_lora, rw_lora, pad], axis=-1)


def _hi_lo(w):
    hi = w.astype(BF16)
    return jnp.stack([hi, (w - hi.astype(F32)).astype(BF16)])


def _to_chains(a):
    b, t, _ = a.shape
    return a.reshape(b, t, R_HEADS, R_N).transpose(1, 3, 0, 2).reshape(t, R_N, b * R_HEADS)


def _from_chains(a, b):
    t = a.shape[0]
    return a.reshape(t, R_N, b, R_HEADS).transpose(2, 0, 3, 1).reshape(b, t, D)


def kernel(x, c, ctx, c_ctx, norm1_w, w_mod, b_mod, w_in, gla_w_a2, gla_b_a, gla_norm_w, rwkv_mu, rwkv_w0, rwkv_w2, rwkv_a0, rwkv_a2, rwkv_g2, rwkv_k_k, rwkv_k_a, rwkv_r_k, rwkv_ln_w, rwkv_ln_b, w_out, norm2_w, peer_w_q, peer_sub_keys, peer_u, peer_v, final_norm_w):
    b, seq, _ = x.shape
    assert w_in.shape[0] == 1 and ctx.shape[1] == N_CTX and seq % ROWS == 0
    row = lambda v: v.reshape(1, -1)

    c16 = jnp.zeros((16, D), F32).at[:b].set(c).at[b].set(c_ctx)
    m = _mod(c16, w_mod[0], row(b_mod[0])).reshape(16, N_MOD, D)
    m_lat, m_ctx = m[:b], m[b]
    per_b = lambda i: m_lat[:, i].reshape(b, 1, D)
    shift1 = jnp.stack([jnp.broadcast_to(m_ctx[0], (b, D)), m_lat[:, 0]], axis=1).reshape(b, 2, 1, D)
    scale1 = jnp.stack([jnp.broadcast_to(m_ctx[1], (b, D)), m_lat[:, 1]], axis=1).reshape(b, 2, 1, D)

    h_all = jnp.concatenate([ctx, x], axis=1)
    z = _inproj(h_all, row(norm1_w[0]), scale1, shift1, _reorder_cols(w_in[0]).astype(BF16))

    mu = rwkv_mu[0]
    mu_lora = jnp.concatenate([jnp.zeros((32,), F32), mu[3072:], jnp.zeros((128,), F32)]).reshape(1, 512)
    e = jnp.repeat(jnp.eye(R_HEADS, dtype=BF16), R_N, axis=0)
    w2f = _hi_lo(jnp.zeros((128, D), F32).at[L_WF:L_WF + R_W_LORA].set(rwkv_w2[0, 0]))
    w2b = _hi_lo(jnp.zeros((256, D), F32).at[L_WB:L_WB + R_W_LORA].set(rwkv_w2[0, 1]))
    a2 = _hi_lo(jnp.zeros((128, D), F32).at[L_A - 128:L_A - 128 + R_A_LORA].set(rwkv_a2[0]))
    g2 = _hi_lo(jnp.zeros((256, D), F32).at[L_G - 128:L_G - 128 + R_G_LORA].set(rwkv_g2[0]))
    wa = jnp.zeros((2, 128, G_HEADS * G_DK), F32)
    wa = wa.at[0, 0:G_LORA].set(gla_w_a2[0, 0]).at[1, G_LORA:2 * G_LORA].set(gla_w_a2[0, 1])
    wa = jnp.stack([_hi_lo(wa[0]), _hi_lo(wa[1])])
    r, k2, vr, kk, kka, d_f, d_b, bonus, gate_r = _rprep(
        z, row(mu[:3072]), mu_lora, rwkv_w0[0].reshape(2, 1, D), w2f, w2b, row(rwkv_a0[0]), a2,
        g2, row(rwkv_k_k[0]), row(rwkv_k_a[0]), row(rwkv_r_k[0]), e, e.T)

    o_r = _rwkv(_to_chains(r), _to_chains(k2), _to_chains(vr), _to_chains(kk), _to_chains(kka),
                _to_chains(d_f), _to_chains(d_b))
    or_f, or_b = _from_chains(o_r[0], b), _from_chains(o_r[1], b)

    og_f, og_b = _gla(z, wa, gla_b_a[0].reshape(2, 1, -1))

    h1, hm_b = _mix(og_f, og_b, z, or_f, or_b, bonus, gate_r, x, row(gla_norm_w[0]), row(rwkv_ln_w[0]),
                        row(rwkv_ln_b[0]), e, e.T, w_out[0].astype(BF16), per_b(2), row(norm2_w[0]),
                        per_b(3), per_b(4))

    n_tok = b * seq
    hm_t = hm_b.reshape(n_tok, D).T
    n_cnt, e1, rho, e2 = _route(hm_t, peer_w_q[0].T.astype(BF16),
                                peer_sub_keys[0].reshape(2 * P_HEADS, P_KEYS, -1).astype(BF16))
    f_t = _peer(hm_t, peer_u[0].astype(BF16), peer_v[0].T.astype(BF16), n_cnt, e1, rho, e2)
    return _final(h1, f_t.T.reshape(b, seq, D), per_b(5), row(final_norm_w))
```

```python
import functools

import numpy as np
import jax
import jax.numpy as jnp
from jax import lax
from jax.experimental import pallas as pl
from jax.experimental.pallas import tpu as pltpu

F32 = jnp.float32
BF16 = jnp.bfloat16
HI = lax.Precision.HIGHEST

D = 1024
GRID_COLS = 64
N_CTX = 256
EPS = 1e-6
N_MOD = 6

G_HEADS = 4
G_DV = D // G_HEADS
G_DK = G_DV // 2
G_LORA = 16
G_GATE_NORM = 16.0
G_CHUNK = 64

R_N = 64
R_HEADS = D // R_N
R_LN_EPS = 64e-5
R_W_LORA = 64
R_A_LORA = 64
R_G_LORA = 160

P_HEADS = 8
P_KEYS = 128
P_TOPK = 16

C_RWKV = 0
C_GLA = 3072
C_GATE = 6144
C_LORA = 8192
LORA_W = 512
N_PROJ = C_LORA + LORA_W
L_WF, L_WB, L_A, L_G = 32, 96, 160, 224

ROWS = 256
V7X_VMEM_BYTES = 64 * 1024 * 1024


def _cparams(sem, vmem_mb):
    return pltpu.CompilerParams(dimension_semantics=sem,
                                vmem_limit_bytes=min(vmem_mb * 1024 * 1024, V7X_VMEM_BYTES - (8 << 20)))


def _sigmoid(x):
    return 1.0 / (1.0 + jnp.exp(-x))


def _softplus(x):
    return jnp.maximum(x, 0.0) + jnp.log(1.0 + jnp.exp(-jnp.abs(x)))


def _dot(a, b, precision=None):
    return jnp.dot(a, b, preferred_element_type=F32, precision=precision)


def _dot_nt(a, b, precision=None):
    return lax.dot_general(a, b, (((1,), (1,)), ((), ())), preferred_element_type=F32, precision=precision)


def _dot_tn(a, b, precision=None):
    return lax.dot_general(a, b, (((0,), (0,)), ((), ())), preferred_element_type=F32, precision=precision)


def _split2(x):
    hi = x.astype(BF16)
    return hi, (x - hi.astype(F32)).astype(BF16)


def _dot_x2(x, w):
    hi, lo = _split2(x)
    return _dot(hi, w) + _dot(lo, w)


def _dot_2x(w, x):
    hi, lo = _split2(x)
    return _dot(w, hi) + _dot(w, lo)


def _dot_x3(x, w_ref):
    hi, lo = _split2(x)
    return _dot(hi, w_ref[0]) + (_dot(lo, w_ref[0]) + _dot(hi, w_ref[1]))


def _mod_kernel(c_ref, w_ref, b_ref, o_ref):
    c = c_ref[...]
    s = c * _sigmoid(c)
    o_ref[...] = _dot(s.astype(BF16), w_ref[...].astype(BF16)) + b_ref[...]


def _mod(c16, w_mod, b_mod):
    n = w_mod.shape[1]
    tn = 768
    return pl.pallas_call(
        _mod_kernel,
        grid=(n // tn,),
        in_specs=[pl.BlockSpec((16, D), lambda j: (0, 0)),
                  pl.BlockSpec((D, tn), lambda j: (0, j)),
                  pl.BlockSpec((1, tn), lambda j: (0, j))],
        out_specs=pl.BlockSpec((16, tn), lambda j: (0, j)),
        out_shape=jax.ShapeDtypeStruct((16, n), F32),
        compiler_params=_cparams(("parallel",), 24),
        name="mod",
    )(c16, w_mod, b_mod)


INPROJ_ROWS = 768


def _inproj_kernel(h_ref, nw_ref, sc_ref, sh_ref, w_ref, o_ref, ol_ref, a_scr):
    n = pl.program_id(2)

    @pl.when(n == 0)
    def _():
        x = h_ref[0]
        y = x * lax.rsqrt(jnp.mean(x * x, axis=-1, keepdims=True) + EPS) * nw_ref[...]
        t = pl.program_id(1) * INPROJ_ROWS + lax.broadcasted_iota(jnp.int32, (INPROJ_ROWS, 1), 0)
        is_ctx = t < N_CTX
        sc = jnp.where(is_ctx, sc_ref[0, 0], sc_ref[0, 1])
        sh = jnp.where(is_ctx, sh_ref[0, 0], sh_ref[0, 1])
        a_scr[...] = (y * (1.0 + sc) + sh).astype(BF16)

    acc = _dot(a_scr[...], w_ref[...])
    n_main = C_LORA // LORA_W

    @pl.when(n < n_main)
    def _():
        o_ref[0] = acc.astype(BF16)

    @pl.when(n == n_main)
    def _():
        ol_ref[0] = acc


def _inproj(h_all, norm_w, scale, shift, w_p):
    b, t, _ = h_all.shape
    tn = LORA_W
    n_main = C_LORA // tn
    mod = pl.BlockSpec((1, 2, 1, D), lambda i, j, n: (i, 0, 0, 0))
    return pl.pallas_call(
        _inproj_kernel,
        grid=(b, t // INPROJ_ROWS, N_PROJ // tn),
        in_specs=[pl.BlockSpec((1, INPROJ_ROWS, D), lambda i, j, n: (i, j, 0)),
                  pl.BlockSpec((1, D), lambda i, j, n: (0, 0)),
                  mod, mod,
                  pl.BlockSpec((D, tn), lambda i, j, n: (0, n))],
        out_specs=[pl.BlockSpec((1, INPROJ_ROWS, tn), lambda i, j, n: (i, j, jnp.minimum(n, n_main - 1))),
                   pl.BlockSpec((1, INPROJ_ROWS, tn), lambda i, j, n: (i, j, 0))],
        out_shape=[jax.ShapeDtypeStruct((b, t, C_LORA), BF16), jax.ShapeDtypeStruct((b, t, tn), F32)],
        scratch_shapes=[pltpu.VMEM((INPROJ_ROWS, D), BF16)],
        compiler_params=_cparams(("parallel", "parallel", "arbitrary"), 32),
        name="inproj",
    )(h_all, norm_w, scale, shift, w_p)


def _shift_mix(zc, zp, zn, mu, is_ctx):
    rows, w = zc.shape
    row = lax.broadcasted_iota(jnp.int32, (rows, w), 0)
    lane = lax.broadcasted_iota(jnp.int32, (rows, w), 1)
    pmask = jnp.where(is_ctx, 1, 3)
    rmask = jnp.where(is_ctx, rows - 1, GRID_COLS - 1)
    cls = lane & pmask
    rr = row & rmask
    prev = pltpu.roll(zc, 1, 0)
    nxt = pltpu.roll(zc, rows - 1, 0)
    up = jnp.concatenate([zp, zc[:rows - GRID_COLS]], axis=0)
    down = jnp.concatenate([zc[GRID_COLS:], zn], axis=0)
    zs = jnp.where((cls == 0) & (rr != 0), prev,
                   jnp.where((cls == 1) & (rr != rmask), nxt,
                             jnp.where(cls == 2, up, jnp.where(cls == 3, down, 0.0))))
    return zc + (zs - zc) * mu


def _rprep_kernel(zc_ref, zp_ref, zn_ref, lc_ref, lp_ref, ln_ref,
                  mu_ref, mul_ref, w0_ref, w2f_ref, w2b_ref, a0_ref, a2_ref, g2_ref,
                  kk_w_ref, ka_w_ref, rk_w_ref, e_ref, et_ref,
                  r_o, k_o, v_o, kk_o, kka_o, df_o, db_o, bonus_o, gate_o):
    tb = pl.program_id(1)
    ntb = pl.num_programs(1)
    is_ctx = tb == 0
    has_up = (tb > 1).astype(F32)
    has_down = (tb < ntb - 1).astype(F32)

    zl = _shift_mix(lc_ref[0], lp_ref[0] * has_up, ln_ref[0] * has_down, mul_ref[...], is_ctx)
    def decay(window, w2p_ref, d):
        w = -_softplus(-(w0_ref[d] + _dot_x3(jnp.tanh(window), w2p_ref))) - 0.5
        return -jnp.exp(w)

    df_o[0] = decay(zl[:, 0:128], w2f_ref, 0).astype(BF16)
    db_o[0] = decay(zl[:, 0:256], w2b_ref, 1).astype(BF16)
    a = _sigmoid(a0_ref[...] + _dot_x3(zl[:, 128:256], a2_ref))
    gate_o[0] = _dot_x3(_sigmoid(zl[:, 128:384]), g2_ref).astype(BF16)

    def piece(i):
        sl = slice(i * D, (i + 1) * D)
        return _shift_mix(zc_ref[0, :, sl].astype(F32), zp_ref[0, :, sl].astype(F32) * has_up,
                          zn_ref[0, :, sl].astype(F32) * has_down, mu_ref[:, sl], is_ctx)

    r = piece(0)
    kr = piece(1)
    vr = piece(2)
    e = e_ref[...]
    et = et_ref[...]
    kkraw = kr * kk_w_ref[...]
    nrm = jnp.sqrt(_dot_x2(kkraw * kkraw, e))
    inv = 1.0 / jnp.maximum(nrm, 1e-12)
    kk = kkraw * _dot_x2(inv, et)
    k2 = kr * (1.0 + (a - 1.0) * ka_w_ref[...])
    rk = _dot_x2(r * k2 * rk_w_ref[...], e)
    r_o[0] = r.astype(BF16)
    k_o[0] = k2.astype(BF16)
    v_o[0] = vr.astype(BF16)
    kk_o[0] = kk.astype(BF16)
    kka_o[0] = (kk * a).astype(BF16)
    bonus_o[0] = (_dot_x2(rk, et) * vr).astype(BF16)


def _rprep(z, zl, mu_main, mu_lora, w0, w2f, w2b, a0, a2, g2, k_k, k_a, r_k, e, et):
    b, t, _ = z.shape
    ntb = t // ROWS
    q = ROWS // GRID_COLS
    last64 = t // GRID_COLS - 1
    wm = 3 * D
    cm = C_RWKV // wm
    cl = 0
    cur = lambda c: (lambda i, j: (i, j, c))
    prv = lambda c: (lambda i, j: (i, jnp.maximum(j * q - 1, 0), c))
    nxt = lambda c: (lambda i, j: (i, jnp.minimum(j * q + q, last64), c))
    full = lambda *s: pl.BlockSpec(s, lambda i, j: (0,) * len(s))
    out_spec = pl.BlockSpec((1, ROWS, D), lambda i, j: (i, j, 0))
    return pl.pallas_call(
        _rprep_kernel,
        grid=(b, ntb),
        in_specs=[pl.BlockSpec((1, ROWS, wm), cur(cm)),
                  pl.BlockSpec((1, GRID_COLS, wm), prv(cm)),
                  pl.BlockSpec((1, GRID_COLS, wm), nxt(cm)),
                  pl.BlockSpec((1, ROWS, 512), cur(cl)),
                  pl.BlockSpec((1, GRID_COLS, 512), prv(cl)),
                  pl.BlockSpec((1, GRID_COLS, 512), nxt(cl)),
                  full(1, wm), full(1, 512), full(2, 1, D), full(2, 128, D), full(2, 256, D),
                  full(1, D), full(2, 128, D), full(2, 256, D),
                  full(1, D), full(1, D), full(1, D), full(D, R_HEADS), full(R_HEADS, D)],
        out_specs=[out_spec] * 9,
        out_shape=[jax.ShapeDtypeStruct((b, t, D), BF16)] * 9,
        compiler_params=_cparams(("parallel", "parallel"), 52),
        name="rprep",
    )(z, z, z, zl, zl, zl, mu_main, mu_lora, w0, w2f, w2b, a0, a2, g2, k_k, k_a, r_k, e, et)


R_STEPS = 64


def _rwkv_kernel(r_ref, k_ref, v_ref, kk_ref, kka_ref, lwf_ref, lwb_ref, o_ref, s_ref, rows_ref):
    dirn = pl.program_id(0)
    tix = lambda i: jnp.where(dirn == 0, i, R_STEPS - 1 - i)

    @pl.when(pl.program_id(1) == 0)
    def _():
        s_ref[...] = jnp.zeros_like(s_ref)

    def weighted_sum(rows):
        acc = s_ref[0] * rows_ref[rows, 0:1, :]
        for k in range(1, R_N):
            acc = acc + s_ref[k] * rows_ref[rows, k:k + 1, :]
        return acc

    rows_ref[0] = kk_ref[tix(0)].astype(F32)
    sa0 = weighted_sum(0)

    def step(i, carry):
        sa, lcum = carry
        t = tix(i)
        t_next = tix(jnp.minimum(i + 1, R_STEPS - 1))
        lcum = lcum + jnp.where(dirn == 0, lwf_ref[t], lwb_ref[t]).astype(F32)
        e_pos = jnp.exp(lcum)
        e_neg = jnp.exp(-lcum)
        rows_ref[0] = kk_ref[t_next].astype(F32) * e_pos
        rows_ref[1] = kka_ref[t].astype(F32) * e_neg
        rows_ref[2] = k_ref[t].astype(F32) * e_neg
        rows_ref[3] = r_ref[t].astype(F32) * e_pos
        vv = v_ref[t].astype(F32)
        out = None
        sa_next = None
        for k in range(R_N):
            sk = s_ref[k] - sa * rows_ref[1, k:k + 1, :] + vv * rows_ref[2, k:k + 1, :]
            s_ref[k] = sk
            o_term = sk * rows_ref[3, k:k + 1, :]
            s_term = sk * rows_ref[0, k:k + 1, :]
            out = o_term if out is None else out + o_term
            sa_next = s_term if sa_next is None else sa_next + s_term
        o_ref[0, t] = out.astype(BF16)
        return sa_next, lcum

    _, lcum = lax.fori_loop(0, R_STEPS, step, (sa0, jnp.zeros((R_N, s_ref.shape[2]), F32)))
    rows_ref[0] = jnp.exp(lcum)
    for k in range(R_N):
        s_ref[k] = s_ref[k] * rows_ref[0, k:k + 1, :]


def _rwkv_order(dirn, nb, n_ctx_blk, n_blk):
    back = jnp.where(nb < n_ctx_blk, n_ctx_blk - 1 - nb, n_blk + n_ctx_blk - 1 - nb)
    return jnp.where(dirn == 0, nb, back)


def _rwkv(r_t, k_t, v_t, kk_t, kka_t, lwf_t, lwb_t):
    t, _, nch = r_t.shape
    n_blk = t // R_STEPS
    n_ctx_blk = N_CTX // R_STEPS
    blk = lambda d, n: (_rwkv_order(d, n, n_ctx_blk, n_blk), 0, 0)
    blk4 = lambda d, n: (d, _rwkv_order(d, n, n_ctx_blk, n_blk), 0, 0)
    only = lambda own: (lambda d, n: (jnp.where(d == own, _rwkv_order(d, n, n_ctx_blk, n_blk), 0), 0, 0))
    spec = pl.BlockSpec((R_STEPS, R_N, nch), blk)
    spec4 = pl.BlockSpec((1, R_STEPS, R_N, nch), blk4)
    return pl.pallas_call(
        _rwkv_kernel,
        grid=(2, n_blk),
        in_specs=[spec, spec, spec, spec, spec,
                  pl.BlockSpec((R_STEPS, R_N, nch), only(0)), pl.BlockSpec((R_STEPS, R_N, nch), only(1))],
        out_specs=spec4,
        out_shape=jax.ShapeDtypeStruct((2, t, R_N, nch), BF16),
        scratch_shapes=[pltpu.VMEM((R_N, R_N, nch), F32), pltpu.VMEM((4, R_N, nch), F32)],
        compiler_params=_cparams(("arbitrary", "arbitrary"), 48),
        name="rwkv",
    )(r_t, k_t, v_t, kk_t, kka_t, lwf_t, lwb_t)


def _log_sigmoid(x):
    return jnp.minimum(x, 0.0) - jnp.log(1.0 + jnp.exp(-jnp.abs(x)))


def _gla_dir(q, k, v, alo, wa, ba, st_ref, reverse):
    c = q.shape[0]
    q = q.astype(F32)
    k = k.astype(F32)
    g = _log_sigmoid(_dot_x3(alo, wa) + ba) * (1.0 / G_GATE_NORM)
    row = lax.broadcasted_iota(jnp.int32, (c, c), 0)
    col = lax.broadcasted_iota(jnp.int32, (c, c), 1)
    tri = (row <= col) if reverse else (row >= col)
    bcum = _dot_2x(tri.astype(BF16), g)
    blast = bcum[0:1] if reverse else bcum[c - 1:c]
    qd = (q * (G_DK ** -0.5) * jnp.exp(bcum)).astype(BF16)
    kd = (k * jnp.exp(-bcum)).astype(BF16)
    kl = (k * jnp.exp(blast - bcum)).astype(BF16)
    eb = jnp.exp(blast)
    vb = v.astype(BF16)
    outs = []
    for h in range(G_HEADS):
        sk = slice(h * G_DK, (h + 1) * G_DK)
        sv = slice(h * G_DV, (h + 1) * G_DV)
        att = jnp.where(tri, _dot_nt(qd[:, sk], kd[:, sk]), 0.0)
        st = st_ref[h]
        outs.append(_dot_nt(qd[:, sk], st.astype(BF16)) + _dot(att.astype(BF16), vb[:, sv]))
        st_ref[h] = st * eb[:, sk] + _dot_tn(vb[:, sv], kl[:, sk])
    return jnp.concatenate(outs, axis=-1)


def _gla_kernel(qf, kf, vf, lf, qb, kb, vb, lb, wa_ref, ba_ref, of_ref, ob_ref, sf_ref, sb_ref):
    @pl.when(pl.program_id(1) == 0)
    def _():
        sf_ref[...] = jnp.zeros_like(sf_ref)
        sb_ref[...] = jnp.zeros_like(sb_ref)

    of_ref[0] = _gla_dir(qf[0], kf[0], vf[0], lf[0], wa_ref.at[0], ba_ref[0], sf_ref, False).astype(BF16)
    ob_ref[0] = _gla_dir(qb[0], kb[0], vb[0], lb[0], wa_ref.at[1], ba_ref[1], sb_ref, True).astype(BF16)


def _gla(z, zl, w_a2, b_a):
    b, t, _ = z.shape
    n_blk = t // G_CHUNK
    n_ctx_blk = N_CTX // G_CHUNK
    fwd = lambda n: n
    bwd = lambda n: _rwkv_order(1, n, n_ctx_blk, n_blk)
    wk = G_HEADS * G_DK
    cq, ck, cv = C_GLA // wk, (C_GLA + wk) // wk, (C_GLA + 2 * wk) // D
    cl = 0

    def specs(order):
        return [pl.BlockSpec((1, G_CHUNK, wk), lambda i, n: (i, order(n), cq)),
                pl.BlockSpec((1, G_CHUNK, wk), lambda i, n: (i, order(n), ck)),
                pl.BlockSpec((1, G_CHUNK, D), lambda i, n: (i, order(n), cv)),
                pl.BlockSpec((1, G_CHUNK, 128), lambda i, n: (i, order(n), cl))]

    out_sd = jax.ShapeDtypeStruct((b, t, D), BF16)
    state = pltpu.VMEM((G_HEADS, G_DV, G_DK), F32)
    return pl.pallas_call(
        _gla_kernel,
        grid=(b, n_blk),
        in_specs=specs(fwd) + specs(bwd) + [
            pl.BlockSpec((2, 2, 128, wk), lambda i, n: (0, 0, 0, 0)),
            pl.BlockSpec((2, 1, wk), lambda i, n: (0, 0, 0))],
        out_specs=[pl.BlockSpec((1, G_CHUNK, D), lambda i, n: (i, fwd(n), 0)),
                   pl.BlockSpec((1, G_CHUNK, D), lambda i, n: (i, bwd(n), 0))],
        out_shape=[out_sd, out_sd],
        scratch_shapes=[state, state],
        compiler_params=_cparams(("parallel", "arbitrary"), 24),
        name="gla",
    )(z, z, z, zl, z, z, z, zl, w_a2, b_a)


def _mix_kernel(ogf_ref, ogb_ref, gout_ref, orf_ref, orb_ref, bonus_ref, gr_ref, gg_ref, grw_ref, x_ref,
                gnw_ref, lnw_ref, lnb_ref, e_ref, et_ref, wout_ref, m2_ref, n2w_ref, m3_ref, m4_ref,
                h1_o, hmb_o):
    og = ogf_ref[0].astype(F32) + ogb_ref[0].astype(F32)
    parts = []
    for h in range(G_HEADS):
        seg = og[:, h * G_DV:(h + 1) * G_DV]
        parts.append(seg * lax.rsqrt(jnp.mean(seg * seg, axis=-1, keepdims=True) + EPS))
    gout = gout_ref[0].astype(F32)
    y_gla = jnp.concatenate(parts, axis=-1) * gnw_ref[...] * (gout * _sigmoid(gout))

    e = e_ref[...]
    et = et_ref[...]
    o_r = orf_ref[0].astype(F32) + orb_ref[0].astype(F32)
    mu = _dot_x2(_dot_x2(o_r, e) * (1.0 / R_N), et)
    dlt = o_r - mu
    var = _dot_x2(dlt * dlt, e) * (1.0 / R_N)
    o_r = (dlt * _dot_x2(lax.rsqrt(var + R_LN_EPS), et) * lnw_ref[...] + lnb_ref[...]
           + bonus_ref[0].astype(F32))
    y_rwkv = o_r * gr_ref[0].astype(F32)

    y = _sigmoid(gg_ref[0].astype(F32)) * y_gla + _sigmoid(grw_ref[0].astype(F32)) * y_rwkv
    h1 = x_ref[0] + m2_ref[0] * _dot(y.astype(BF16), wout_ref[...])
    h1_o[0] = h1
    hn = h1 * lax.rsqrt(jnp.mean(h1 * h1, axis=-1, keepdims=True) + EPS) * n2w_ref[...]
    hmb_o[0] = (hn * (1.0 + m4_ref[0]) + m3_ref[0]).astype(BF16)


def _mix(og_f, og_b, z, or_f, or_b, bonus, gate_r, x, gnw, lnw, lnb, e, et, w_out, m2, n2w, m3, m4):
    b, tl, _ = x.shape
    off = N_CTX // ROWS
    seq = lambda c: pl.BlockSpec((1, ROWS, D), lambda i, j: (i, j + off, c))
    full = lambda *s: pl.BlockSpec(s, lambda i, j: (0,) * len(s))
    per_b = pl.BlockSpec((1, 1, D), lambda i, j: (i, 0, 0))
    lat = pl.BlockSpec((1, ROWS, D), lambda i, j: (i, j, 0))
    return pl.pallas_call(
        _mix_kernel,
        grid=(b, tl // ROWS),
        in_specs=[seq(0), seq(0), seq((C_GLA + 2048) // D), seq(0), seq(0), seq(0), seq(0),
                  seq(C_GATE // D), seq(C_GATE // D + 1), lat,
                  full(1, D), full(1, D), full(1, D), full(D, R_HEADS), full(R_HEADS, D), full(D, D),
                  per_b, full(1, D), per_b, per_b],
        out_specs=[lat, lat],
        out_shape=[jax.ShapeDtypeStruct((b, tl, D), F32), jax.ShapeDtypeStruct((b, tl, D), BF16)],
        compiler_params=_cparams(("parallel", "parallel"), 52),
        name="mix",
    )(og_f, og_b, z, or_f, or_b, bonus, gate_r, z, z, x, gnw, lnw, lnb, e, et, w_out, m2, n2w, m3, m4)


ROUTE_TOK = 512


N_CAND = 80


def _dup_bf16(x):
    hi = lax.bitcast_convert_type(x.astype(BF16).astype(F32), jnp.uint32)
    return hi | (hi >> 16)


def _route_kernel(x_ref, wq_ref, keys_ref, n_o, e1_o, rho_o, e2_o,
                  q_scr, s_scr, t1_scr, t2_scr, cand_scr, rho_scr, cnt_scr):
    q_scr[...] = _dot(wq_ref[...], x_ref[...]).astype(BF16)
    ninf = -jnp.inf
    half = P_TOPK // 2
    tn = x_ref.shape[1]
    top_row = jnp.full((1, tn), jnp.inf, F32)

    def next_below(vals, bound):
        below = vals < bound
        return below, jnp.max(jnp.where(below, vals, ninf), axis=0, keepdims=True)

    def head(h, carry):
        q1 = q_scr[pl.ds(pl.multiple_of(h * 2 * P_KEYS, P_KEYS), P_KEYS), :]
        q2 = q_scr[pl.ds(pl.multiple_of(h * 2 * P_KEYS + P_KEYS, P_KEYS), P_KEYS), :]
        s_scr[0] = _dot(keys_ref[2 * h], q1)
        s_scr[1] = _dot(keys_ref[2 * h + 1], q2)
        rho_scr[...] = jnp.full(rho_scr.shape, -1.0, F32)

        def extract(i, bounds):
            m1, m2 = bounds
            _, m1 = next_below(s_scr[0], m1)
            below2, m2 = next_below(s_scr[1], m2)
            rho_scr[...] += jnp.where(below2, 1.0, 0.0)
            t1_scr[pl.ds(i, 1), :] = m1
            t2_scr[pl.ds(i, 1), :] = m2
            return m1, m2

        _, last2 = lax.fori_loop(0, P_TOPK, extract, (top_row, top_row))
        s1 = s_scr[0]
        s2 = s_scr[1]
        t1 = t1_scr[...]
        t2 = t2_scr[...]
        cand_scr[0:P_TOPK] = t1[0:1] + t2
        for i in range(1, half):
            cand_scr[P_TOPK + half * (i - 1):P_TOPK + half * i] = t1[i:i + 1] + t2[0:half]
        cand_scr[N_CAND - half:N_CAND] = t1[half:P_TOPK] + t2[0:1]
        cmax = t1[0:1] + t2[0:1]

        def pick(i, st):
            m, z = st
            _, m = next_below(cand_scr[...], m)
            return m, z + jnp.exp(m - cmax)

        tau, zsum = lax.fori_loop(0, P_TOPK, pick, (top_row, jnp.zeros_like(cmax)))
        cnt_scr[...] = jnp.zeros_like(cnt_scr)

        def count(j, c):
            cnt_scr[...] += jnp.where(s_scr[0] + t2_scr[pl.ds(j, 1), :] >= tau, 1.0, 0.0)
            return c

        lax.fori_loop(0, P_TOPK, count, 0)
        sel1 = s1 >= t1[P_TOPK - 1:P_TOPK]
        sel2 = s2 >= last2
        rho = rho_scr[...] + jnp.where(s2 < last2, 1.0, 0.0)
        n_o[h] = _dup_bf16(jnp.where(sel1, cnt_scr[...], 0.0))
        e1_o[h] = _dup_bf16(jnp.where(sel1, jnp.exp(s1 - t1[0:1]), 0.0))
        rho_o[h] = rho.astype(BF16)
        e2_o[h] = jnp.where(sel2, jnp.exp(s2 - t2[0:1]) / zsum, 0.0).astype(BF16)
        return carry

    lax.fori_loop(0, P_HEADS, head, 0)


def _route(hm_t, wq_t, keys):
    n = hm_t.shape[1]
    tn = ROUTE_TOK
    big = pl.BlockSpec((P_HEADS, P_KEYS, tn), lambda i: (0, 0, i))
    sd = lambda dt: jax.ShapeDtypeStruct((P_HEADS, P_KEYS, n), dt)
    return pl.pallas_call(
        _route_kernel,
        grid=(n // tn,),
        in_specs=[pl.BlockSpec((D, tn), lambda i: (0, i)),
                  pl.BlockSpec((2 * P_HEADS * P_KEYS, D), lambda i: (0, 0)),
                  pl.BlockSpec((2 * P_HEADS, P_KEYS, P_KEYS), lambda i: (0, 0, 0))],
        out_specs=[big, big, big, big],
        out_shape=[sd(jnp.uint32), sd(jnp.uint32), sd(BF16), sd(BF16)],
        scratch_shapes=[pltpu.VMEM((2 * P_HEADS * P_KEYS, tn), BF16), pltpu.VMEM((2, P_KEYS, tn), F32),
                        pltpu.VMEM((P_TOPK, tn), F32), pltpu.VMEM((P_TOPK, tn), F32),
                        pltpu.VMEM((N_CAND, tn), F32),
                        pltpu.VMEM((P_KEYS, tn), F32), pltpu.VMEM((P_KEYS, tn), F32)],
        compiler_params=_cparams(("parallel",), 48),
        name="route",
    )(hm_t, wq_t, keys)


PEER_TOK = 512
PEER_EXP = 1024


def _gelu(x):
    return 0.5 * x * (1.0 + lax.erf(x * (2.0 ** -0.5)))


BF16_ROWS = 16


def _peer_kernel(x_ref, u_ref, vt_ref, n_ref, e1_ref, rho_ref, e2_ref, h1_ref, m5_ref, fw_ref, o_ref, acc_ref):
    j = pl.program_id(1)

    @pl.when(j == 0)
    def _():
        acc_ref[...] = jnp.zeros_like(acc_ref)

    tn = x_ref.shape[1]
    x = x_ref[...]
    n_a = PEER_EXP // P_KEYS
    groups = P_KEYS // BF16_ROWS
    halves = 2
    a_per_half = n_a // halves
    p_halves = []
    for half in range(halves):
        rows_h = a_per_half * P_KEYS
        act = _dot(u_ref[half * rows_h:(half + 1) * rows_h, :], x)
        gates = []
        for al in range(half * a_per_half, (half + 1) * a_per_half):
            g = [None] * groups
            for h in range(P_HEADS):
                row = lambda ref: pltpu.bitcast(jnp.broadcast_to(ref[h, al:al + 1, :], (8, tn)), BF16)
                n_row = row(n_ref)
                e1_row = row(e1_ref)
                for q in range(groups):
                    rows = slice(q * BF16_ROWS, (q + 1) * BF16_ROWS)
                    term = jnp.where(rho_ref[h, rows, :] < n_row, e2_ref[h, rows, :], 0.0) * e1_row
                    g[q] = term if g[q] is None else g[q] + term
            gates.extend(g)
        p_halves.append(_gelu(act).astype(BF16) * jnp.concatenate(gates, axis=0))
    acc_ref[...] += _dot(vt_ref[...], jnp.concatenate(p_halves, axis=0))

    @pl.when(j == pl.num_programs(1) - 1)
    def _():
        h = h1_ref[...] + m5_ref[0] * acc_ref[...].T
        o_ref[...] = h * lax.rsqrt(jnp.mean(h * h, axis=-1, keepdims=True) + EPS) * fw_ref[...]


def _peer(hm_t, u_b, vt_b, n_cnt, e1, rho, e2, h1, m5, final_w, seq):
    n = hm_t.shape[1]
    n_exp = u_b.shape[0]
    tn, te = PEER_TOK, PEER_EXP
    big = pl.BlockSpec((P_HEADS, P_KEYS, tn), lambda i, j: (0, 0, i))
    a_rows = pl.BlockSpec((P_HEADS, te // P_KEYS, tn), lambda i, j: (0, j, i))
    tok = pl.BlockSpec((tn, D), lambda i, j: (i, 0))
    return pl.pallas_call(
        _peer_kernel,
        grid=(n // tn, n_exp // te),
        in_specs=[pl.BlockSpec((D, tn), lambda i, j: (0, i)),
                  pl.BlockSpec((te, D), lambda i, j: (j, 0)),
                  pl.BlockSpec((D, te), lambda i, j: (0, j)),
                  a_rows, a_rows, big, big,
                  tok, pl.BlockSpec((1, 1, D), lambda i, j: (i // (seq // tn), 0, 0)),
                  pl.BlockSpec((1, D), lambda i, j: (0, 0))],
        out_specs=tok,
        out_shape=jax.ShapeDtypeStruct((n, D), F32),
        scratch_shapes=[pltpu.VMEM((D, tn), F32)],
        compiler_params=_cparams(("parallel", "arbitrary"), 52),
        name="peer",
    )(hm_t, u_b, vt_b, n_cnt, e1, rho, e2, h1, m5, final_w)


def _reorder_cols(w):
    gla_main, gla_lora = w[..., 0:3072], w[..., 3072:3104]
    rw_main, rw_lora = w[..., 3104:6176], w[..., 6176:6528]
    gates = w[..., 6528:8576]
    pad = jnp.zeros(w.shape[:-1] + (512 - 32 - 352,), w.dtype)
    return jnp.concatenate([rw_main, gla_main, gates, gla_lora, rw_lora, pad], axis=-1)


def _hi_lo(w):
    hi = w.astype(BF16)
    return jnp.stack([hi, (w - hi.astype(F32)).astype(BF16)])


def _to_chains(a):
    b, t, _ = a.shape
    return a.reshape(b, t, R_HEADS, R_N).transpose(1, 3, 0, 2).reshape(t, R_N, b * R_HEADS)


def _from_chains(a, b):
    t = a.shape[0]
    return a.reshape(t, R_N, b, R_HEADS).transpose(2, 0, 3, 1).reshape(b, t, D)


def kernel(x, c, ctx, c_ctx, norm1_w, w_mod, b_mod, w_in, gla_w_a2, gla_b_a, gla_norm_w, rwkv_mu, rwkv_w0, rwkv_w2, rwkv_a0, rwkv_a2, rwkv_g2, rwkv_k_k, rwkv_k_a, rwkv_r_k, rwkv_ln_w, rwkv_ln_b, w_out, norm2_w, peer_w_q, peer_sub_keys, peer_u, peer_v, final_norm_w):
    b, seq, _ = x.shape
    assert w_in.shape[0] == 1 and ctx.shape[1] == N_CTX and seq % ROWS == 0
    row = lambda v: v.reshape(1, -1)

    c16 = jnp.zeros((16, D), F32).at[:b].set(c).at[b].set(c_ctx)
    m = _mod(c16, w_mod[0], row(b_mod[0])).reshape(16, N_MOD, D)
    m_lat, m_ctx = m[:b], m[b]
    per_b = lambda i: m_lat[:, i].reshape(b, 1, D)
    shift1 = jnp.stack([jnp.broadcast_to(m_ctx[0], (b, D)), m_lat[:, 0]], axis=1).reshape(b, 2, 1, D)
    scale1 = jnp.stack([jnp.broadcast_to(m_ctx[1], (b, D)), m_lat[:, 1]], axis=1).reshape(b, 2, 1, D)

    h_all = jnp.concatenate([ctx, x], axis=1)
    z, z_lora = _inproj(h_all, row(norm1_w[0]), scale1, shift1, _reorder_cols(w_in[0]).astype(BF16))

    mu = rwkv_mu[0]
    mu_lora = jnp.concatenate([jnp.zeros((32,), F32), mu[3072:], jnp.zeros((128,), F32)]).reshape(1, 512)
    e = jnp.repeat(jnp.eye(R_HEADS, dtype=BF16), R_N, axis=0)
    w2f = _hi_lo(jnp.zeros((128, D), F32).at[L_WF:L_WF + R_W_LORA].set(rwkv_w2[0, 0]))
    w2b = _hi_lo(jnp.zeros((256, D), F32).at[L_WB:L_WB + R_W_LORA].set(rwkv_w2[0, 1]))
    a2 = _hi_lo(jnp.zeros((128, D), F32).at[L_A - 128:L_A - 128 + R_A_LORA].set(rwkv_a2[0]))
    g2 = _hi_lo(jnp.zeros((256, D), F32).at[L_G - 128:L_G - 128 + R_G_LORA].set(rwkv_g2[0]))
    wa = jnp.zeros((2, 128, G_HEADS * G_DK), F32)
    wa = wa.at[0, 0:G_LORA].set(gla_w_a2[0, 0]).at[1, G_LORA:2 * G_LORA].set(gla_w_a2[0, 1])
    wa = jnp.stack([_hi_lo(wa[0]), _hi_lo(wa[1])])
    r, k2, vr, kk, kka, d_f, d_b, bonus, gate_r = _rprep(
        z, z_lora, row(mu[:3072]), mu_lora, rwkv_w0[0].reshape(2, 1, D), w2f, w2b, row(rwkv_a0[0]), a2,
        g2, row(rwkv_k_k[0]), row(rwkv_k_a[0]), row(rwkv_r_k[0]), e, e.T)

    o_r = _rwkv(_to_chains(r), _to_chains(k2), _to_chains(vr), _to_chains(kk), _to_chains(kka),
                _to_chains(d_f), _to_chains(d_b))
    or_f, or_b = _from_chains(o_r[0], b), _from_chains(o_r[1], b)

    og_f, og_b = _gla(z, z_lora, wa, gla_b_a[0].reshape(2, 1, -1))

    h1, hm_b = _mix(og_f, og_b, z, or_f, or_b, bonus, gate_r, x, row(gla_norm_w[0]), row(rwkv_ln_w[0]),
                        row(rwkv_ln_b[0]), e, e.T, w_out[0].astype(BF16), per_b(2), row(norm2_w[0]),
                        per_b(3), per_b(4))

    n_tok = b * seq
    hm_t = hm_b.reshape(n_tok, D).T
    n_cnt, e1, rho, e2 = _route(hm_t, peer_w_q[0].T.astype(BF16),
                                peer_sub_keys[0].reshape(2 * P_HEADS, P_KEYS, -1).astype(BF16))
    assert seq % PEER_TOK == 0
    out = _peer(hm_t, peer_u[0].astype(BF16), peer_v[0].T.astype(BF16), n_cnt, e1, rho, e2,
                h1.reshape(n_tok, D), per_b(5), row(final_norm_w), seq)
    return out.reshape(b, seq, D)
```

```python
import functools

import numpy as np
import jax
import jax.numpy as jnp
from jax import lax
from jax.experimental import pallas as pl
from jax.experimental.pallas import tpu as pltpu

F32 = jnp.float32
BF16 = jnp.bfloat16
HI = lax.Precision.HIGHEST

D = 1024
GRID_COLS = 64
N_CTX = 256
EPS = 1e-6
N_MOD = 6

G_HEADS = 4
G_DV = D // G_HEADS
G_DK = G_DV // 2
G_LORA = 16
G_GATE_NORM = 16.0
G_CHUNK = 64

R_N = 64
R_HEADS = D // R_N
R_LN_EPS = 64e-5
R_W_LORA = 64
R_A_LORA = 64
R_G_LORA = 160

P_HEADS = 8
P_KEYS = 128
P_TOPK = 16

C_RWKV = 0
C_GLA = 3072
C_GATE = 6144
C_LORA = 8192
LORA_W = 512
N_PROJ = C_LORA + LORA_W
L_WF, L_WB, L_A, L_G = 32, 96, 160, 224

ROWS = 256
V7X_VMEM_BYTES = 64 * 1024 * 1024


def _cparams(sem, vmem_mb):
    return pltpu.CompilerParams(dimension_semantics=sem,
                                vmem_limit_bytes=min(vmem_mb * 1024 * 1024, V7X_VMEM_BYTES - (8 << 20)))


def _sigmoid(x):
    return 1.0 / (1.0 + jnp.exp(-x))


def _softplus(x):
    return jnp.maximum(x, 0.0) + jnp.log(1.0 + jnp.exp(-jnp.abs(x)))


def _dot(a, b, precision=None):
    return jnp.dot(a, b, preferred_element_type=F32, precision=precision)


def _dot_nt(a, b, precision=None):
    return lax.dot_general(a, b, (((1,), (1,)), ((), ())), preferred_element_type=F32, precision=precision)


def _dot_tn(a, b, precision=None):
    return lax.dot_general(a, b, (((0,), (0,)), ((), ())), preferred_element_type=F32, precision=precision)


def _split2(x):
    hi = x.astype(BF16)
    return hi, (x - hi.astype(F32)).astype(BF16)


def _dot_x2(x, w):
    hi, lo = _split2(x)
    return _dot(hi, w) + _dot(lo, w)


def _dot_2x(w, x):
    hi, lo = _split2(x)
    return _dot(w, hi) + _dot(w, lo)


def _dot_x3(x, w_ref):
    hi, lo = _split2(x)
    return _dot(hi, w_ref[0]) + (_dot(lo, w_ref[0]) + _dot(hi, w_ref[1]))


def _mod_kernel(c_ref, w_ref, b_ref, o_ref):
    c = c_ref[...]
    s = c * _sigmoid(c)
    o_ref[...] = _dot(s.astype(BF16), w_ref[...].astype(BF16)) + b_ref[...]


def _mod(c16, w_mod, b_mod):
    n = w_mod.shape[1]
    tn = 768
    return pl.pallas_call(
        _mod_kernel,
        grid=(n // tn,),
        in_specs=[pl.BlockSpec((16, D), lambda j: (0, 0)),
                  pl.BlockSpec((D, tn), lambda j: (0, j)),
                  pl.BlockSpec((1, tn), lambda j: (0, j))],
        out_specs=pl.BlockSpec((16, tn), lambda j: (0, j)),
        out_shape=jax.ShapeDtypeStruct((16, n), F32),
        compiler_params=_cparams(("parallel",), 24),
        name="mod",
    )(c16, w_mod, b_mod)


INPROJ_ROWS = 768


def _inproj_kernel(h_ref, nw_ref, sc_ref, sh_ref, w_ref, o_ref, ol_ref, a_scr):
    n = pl.program_id(2)

    @pl.when(n == 0)
    def _():
        x = h_ref[0]
        y = x * lax.rsqrt(jnp.mean(x * x, axis=-1, keepdims=True) + EPS) * nw_ref[...]
        t = pl.program_id(1) * INPROJ_ROWS + lax.broadcasted_iota(jnp.int32, (INPROJ_ROWS, 1), 0)
        is_ctx = t < N_CTX
        sc = jnp.where(is_ctx, sc_ref[0, 0], sc_ref[0, 1])
        sh = jnp.where(is_ctx, sh_ref[0, 0], sh_ref[0, 1])
        a_scr[...] = (y * (1.0 + sc) + sh).astype(BF16)

    acc = _dot(a_scr[...], w_ref[...])
    n_main = C_LORA // LORA_W

    @pl.when(n < n_main)
    def _():
        o_ref[0] = acc.astype(BF16)

    @pl.when(n == n_main)
    def _():
        ol_ref[0] = acc


def _inproj(h_all, norm_w, scale, shift, w_p):
    b, t, _ = h_all.shape
    tn = LORA_W
    n_main = C_LORA // tn
    mod = pl.BlockSpec((1, 2, 1, D), lambda i, j, n: (i, 0, 0, 0))
    return pl.pallas_call(
        _inproj_kernel,
        grid=(b, t // INPROJ_ROWS, N_PROJ // tn),
        in_specs=[pl.BlockSpec((1, INPROJ_ROWS, D), lambda i, j, n: (i, j, 0)),
                  pl.BlockSpec((1, D), lambda i, j, n: (0, 0)),
                  mod, mod,
                  pl.BlockSpec((D, tn), lambda i, j, n: (0, n))],
        out_specs=[pl.BlockSpec((1, INPROJ_ROWS, tn), lambda i, j, n: (i, j, jnp.minimum(n, n_main - 1))),
                   pl.BlockSpec((1, INPROJ_ROWS, tn), lambda i, j, n: (i, j, 0))],
        out_shape=[jax.ShapeDtypeStruct((b, t, C_LORA), BF16), jax.ShapeDtypeStruct((b, t, tn), F32)],
        scratch_shapes=[pltpu.VMEM((INPROJ_ROWS, D), BF16)],
        compiler_params=_cparams(("parallel", "parallel", "arbitrary"), 32),
        name="inproj",
    )(h_all, norm_w, scale, shift, w_p)


def _shift_mix(zc, zp, zn, mu, is_ctx):
    rows, w = zc.shape
    row = lax.broadcasted_iota(jnp.int32, (rows, w), 0)
    lane = lax.broadcasted_iota(jnp.int32, (rows, w), 1)
    pmask = jnp.where(is_ctx, 1, 3)
    rmask = jnp.where(is_ctx, rows - 1, GRID_COLS - 1)
    cls = lane & pmask
    rr = row & rmask
    prev = pltpu.roll(zc, 1, 0)
    nxt = pltpu.roll(zc, rows - 1, 0)
    up = jnp.concatenate([zp, zc[:rows - GRID_COLS]], axis=0)
    down = jnp.concatenate([zc[GRID_COLS:], zn], axis=0)
    zs = jnp.where((cls == 0) & (rr != 0), prev,
                   jnp.where((cls == 1) & (rr != rmask), nxt,
                             jnp.where(cls == 2, up, jnp.where(cls == 3, down, 0.0))))
    return zc + (zs - zc) * mu


def _rprep_kernel(zc_ref, zp_ref, zn_ref, lc_ref, lp_ref, ln_ref,
                  mu_ref, mul_ref, w0_ref, w2f_ref, w2b_ref, a0_ref, a2_ref, g2_ref,
                  kk_w_ref, ka_w_ref, rk_w_ref, e_ref, et_ref,
                  r_o, k_o, v_o, kk_o, kka_o, df_o, db_o, bonus_o, gate_o):
    tb = pl.program_id(1)
    ntb = pl.num_programs(1)
    is_ctx = tb == 0
    has_up = (tb > 1).astype(F32)
    has_down = (tb < ntb - 1).astype(F32)

    zl = _shift_mix(lc_ref[0], lp_ref[0] * has_up, ln_ref[0] * has_down, mul_ref[...], is_ctx)
    def decay(window, w2p_ref, d):
        w = -_softplus(-(w0_ref[d] + _dot_x3(jnp.tanh(window), w2p_ref))) - 0.5
        return -jnp.exp(w)

    df_o[0] = decay(zl[:, 0:128], w2f_ref, 0).astype(BF16)
    db_o[0] = decay(zl[:, 0:256], w2b_ref, 1).astype(BF16)
    a = _sigmoid(a0_ref[...] + _dot_x3(zl[:, 128:256], a2_ref))
    gate_o[0] = _dot_x3(_sigmoid(zl[:, 128:384]), g2_ref).astype(BF16)

    def piece(i):
        sl = slice(i * D, (i + 1) * D)
        return _shift_mix(zc_ref[0, :, sl].astype(F32), zp_ref[0, :, sl].astype(F32) * has_up,
                          zn_ref[0, :, sl].astype(F32) * has_down, mu_ref[:, sl], is_ctx)

    r = piece(0)
    kr = piece(1)
    vr = piece(2)
    e = e_ref[...]
    et = et_ref[...]
    kkraw = kr * kk_w_ref[...]
    nrm = jnp.sqrt(_dot_x2(kkraw * kkraw, e))
    inv = 1.0 / jnp.maximum(nrm, 1e-12)
    kk = kkraw * _dot_x2(inv, et)
    k2 = kr * (1.0 + (a - 1.0) * ka_w_ref[...])
    rk = _dot_x2(r * k2 * rk_w_ref[...], e)
    r_o[0] = r.astype(BF16)
    k_o[0] = k2.astype(BF16)
    v_o[0] = vr.astype(BF16)
    kk_o[0] = kk.astype(BF16)
    kka_o[0] = (kk * a).astype(BF16)
    bonus_o[0] = (_dot_x2(rk, et) * vr).astype(BF16)


def _rprep(z, zl, mu_main, mu_lora, w0, w2f, w2b, a0, a2, g2, k_k, k_a, r_k, e, et):
    b, t, _ = z.shape
    ntb = t // ROWS
    q = ROWS // GRID_COLS
    last64 = t // GRID_COLS - 1
    wm = 3 * D
    cm = C_RWKV // wm
    cl = 0
    cur = lambda c: (lambda i, j: (i, j, c))
    prv = lambda c: (lambda i, j: (i, jnp.maximum(j * q - 1, 0), c))
    nxt = lambda c: (lambda i, j: (i, jnp.minimum(j * q + q, last64), c))
    full = lambda *s: pl.BlockSpec(s, lambda i, j: (0,) * len(s))
    out_spec = pl.BlockSpec((1, ROWS, D), lambda i, j: (i, j, 0))
    return pl.pallas_call(
        _rprep_kernel,
        grid=(b, ntb),
        in_specs=[pl.BlockSpec((1, ROWS, wm), cur(cm)),
                  pl.BlockSpec((1, GRID_COLS, wm), prv(cm)),
                  pl.BlockSpec((1, GRID_COLS, wm), nxt(cm)),
                  pl.BlockSpec((1, ROWS, 512), cur(cl)),
                  pl.BlockSpec((1, GRID_COLS, 512), prv(cl)),
                  pl.BlockSpec((1, GRID_COLS, 512), nxt(cl)),
                  full(1, wm), full(1, 512), full(2, 1, D), full(2, 128, D), full(2, 256, D),
                  full(1, D), full(2, 128, D), full(2, 256, D),
                  full(1, D), full(1, D), full(1, D), full(D, R_HEADS), full(R_HEADS, D)],
        out_specs=[out_spec] * 9,
        out_shape=[jax.ShapeDtypeStruct((b, t, D), BF16)] * 9,
        compiler_params=_cparams(("parallel", "parallel"), 52),
        name="rprep",
    )(z, z, z, zl, zl, zl, mu_main, mu_lora, w0, w2f, w2b, a0, a2, g2, k_k, k_a, r_k, e, et)


R_STEPS = 64


def _rwkv_kernel(r_ref, k_ref, v_ref, kk_ref, kka_ref, lwf_ref, lwb_ref, o_ref, s_ref, rows_ref):
    dirn = pl.program_id(0)
    tix = lambda i: jnp.where(dirn == 0, i, R_STEPS - 1 - i)

    @pl.when(pl.program_id(1) == 0)
    def _():
        s_ref[...] = jnp.zeros_like(s_ref)

    def weighted_sum(rows):
        acc = s_ref[0] * rows_ref[rows, 0:1, :]
        for k in range(1, R_N):
            acc = acc + s_ref[k] * rows_ref[rows, k:k + 1, :]
        return acc

    rows_ref[0] = kk_ref[tix(0)].astype(F32)
    sa0 = weighted_sum(0)

    def step(i, carry):
        sa, lcum = carry
        t = tix(i)
        t_next = tix(jnp.minimum(i + 1, R_STEPS - 1))
        lcum = lcum + jnp.where(dirn == 0, lwf_ref[t], lwb_ref[t]).astype(F32)
        e_pos = jnp.exp(lcum)
        e_neg = jnp.exp(-lcum)
        rows_ref[0] = kk_ref[t_next].astype(F32) * e_pos
        rows_ref[1] = kka_ref[t].astype(F32) * e_neg
        rows_ref[2] = k_ref[t].astype(F32) * e_neg
        rows_ref[3] = r_ref[t].astype(F32) * e_pos
        vv = v_ref[t].astype(F32)
        out = None
        sa_next = None
        for k in range(R_N):
            sk = s_ref[k] - sa * rows_ref[1, k:k + 1, :] + vv * rows_ref[2, k:k + 1, :]
            s_ref[k] = sk
            o_term = sk * rows_ref[3, k:k + 1, :]
            s_term = sk * rows_ref[0, k:k + 1, :]
            out = o_term if out is None else out + o_term
            sa_next = s_term if sa_next is None else sa_next + s_term
        o_ref[0, t] = out.astype(BF16)
        return sa_next, lcum

    _, lcum = lax.fori_loop(0, R_STEPS, step, (sa0, jnp.zeros((R_N, s_ref.shape[2]), F32)))
    rows_ref[0] = jnp.exp(lcum)
    for k in range(R_N):
        s_ref[k] = s_ref[k] * rows_ref[0, k:k + 1, :]


def _rwkv_order(dirn, nb, n_ctx_blk, n_blk):
    back = jnp.where(nb < n_ctx_blk, n_ctx_blk - 1 - nb, n_blk + n_ctx_blk - 1 - nb)
    return jnp.where(dirn == 0, nb, back)


def _rwkv(r_t, k_t, v_t, kk_t, kka_t, lwf_t, lwb_t):
    t, _, nch = r_t.shape
    n_blk = t // R_STEPS
    n_ctx_blk = N_CTX // R_STEPS
    blk = lambda d, n: (_rwkv_order(d, n, n_ctx_blk, n_blk), 0, 0)
    blk4 = lambda d, n: (d, _rwkv_order(d, n, n_ctx_blk, n_blk), 0, 0)
    only = lambda own: (lambda d, n: (jnp.where(d == own, _rwkv_order(d, n, n_ctx_blk, n_blk), 0), 0, 0))
    spec = pl.BlockSpec((R_STEPS, R_N, nch), blk)
    spec4 = pl.BlockSpec((1, R_STEPS, R_N, nch), blk4)
    return pl.pallas_call(
        _rwkv_kernel,
        grid=(2, n_blk),
        in_specs=[spec, spec, spec, spec, spec,
                  pl.BlockSpec((R_STEPS, R_N, nch), only(0)), pl.BlockSpec((R_STEPS, R_N, nch), only(1))],
        out_specs=spec4,
        out_shape=jax.ShapeDtypeStruct((2, t, R_N, nch), BF16),
        scratch_shapes=[pltpu.VMEM((R_N, R_N, nch), F32), pltpu.VMEM((4, R_N, nch), F32)],
        compiler_params=_cparams(("arbitrary", "arbitrary"), 48),
        name="rwkv",
    )(r_t, k_t, v_t, kk_t, kka_t, lwf_t, lwb_t)


def _log_sigmoid(x):
    return jnp.minimum(x, 0.0) - jnp.log(1.0 + jnp.exp(-jnp.abs(x)))


def _gla_dir(q, k, v, alo, wa, ba, st_ref, reverse):
    c = q.shape[0]
    q = q.astype(F32)
    k = k.astype(F32)
    g = _log_sigmoid(_dot_x3(alo, wa) + ba) * (1.0 / G_GATE_NORM)
    row = lax.broadcasted_iota(jnp.int32, (c, c), 0)
    col = lax.broadcasted_iota(jnp.int32, (c, c), 1)
    tri = (row <= col) if reverse else (row >= col)
    bcum = _dot_2x(tri.astype(BF16), g)
    blast = bcum[0:1] if reverse else bcum[c - 1:c]
    qd = (q * (G_DK ** -0.5) * jnp.exp(bcum)).astype(BF16)
    kd = (k * jnp.exp(-bcum)).astype(BF16)
    kl = (k * jnp.exp(blast - bcum)).astype(BF16)
    eb = jnp.exp(blast)
    vb = v.astype(BF16)
    outs = []
    for h in range(G_HEADS):
        sk = slice(h * G_DK, (h + 1) * G_DK)
        sv = slice(h * G_DV, (h + 1) * G_DV)
        att = jnp.where(tri, _dot_nt(qd[:, sk], kd[:, sk]), 0.0)
        st = st_ref[h]
        outs.append(_dot_nt(qd[:, sk], st.astype(BF16)) + _dot(att.astype(BF16), vb[:, sv]))
        st_ref[h] = st * eb[:, sk] + _dot_tn(vb[:, sv], kl[:, sk])
    return jnp.concatenate(outs, axis=-1)


GLA_BATCH = 2


def _gla_kernel(qf, kf, vf, lf, qb, kb, vb, lb, wa_ref, ba_ref, of_ref, ob_ref, sf_ref, sb_ref):
    @pl.when(pl.program_id(1) == 0)
    def _():
        sf_ref[...] = jnp.zeros_like(sf_ref)
        sb_ref[...] = jnp.zeros_like(sb_ref)

    for s in range(GLA_BATCH):
        of_ref[s] = _gla_dir(qf[s], kf[s], vf[s], lf[s], wa_ref.at[0], ba_ref[0], sf_ref.at[s], False).astype(BF16)
        ob_ref[s] = _gla_dir(qb[s], kb[s], vb[s], lb[s], wa_ref.at[1], ba_ref[1], sb_ref.at[s], True).astype(BF16)


def _gla(z, zl, w_a2, b_a):
    b, t, _ = z.shape
    n_blk = t // G_CHUNK
    n_ctx_blk = N_CTX // G_CHUNK
    fwd = lambda n: n
    bwd = lambda n: _rwkv_order(1, n, n_ctx_blk, n_blk)
    wk = G_HEADS * G_DK
    cq, ck, cv = C_GLA // wk, (C_GLA + wk) // wk, (C_GLA + 2 * wk) // D
    cl = 0

    nb = GLA_BATCH
    assert b % nb == 0

    def specs(order):
        return [pl.BlockSpec((nb, G_CHUNK, wk), lambda i, n: (i, order(n), cq)),
                pl.BlockSpec((nb, G_CHUNK, wk), lambda i, n: (i, order(n), ck)),
                pl.BlockSpec((nb, G_CHUNK, D), lambda i, n: (i, order(n), cv)),
                pl.BlockSpec((nb, G_CHUNK, 128), lambda i, n: (i, order(n), cl))]

    out_sd = jax.ShapeDtypeStruct((b, t, D), BF16)
    state = pltpu.VMEM((nb, G_HEADS, G_DV, G_DK), F32)
    return pl.pallas_call(
        _gla_kernel,
        grid=(b // nb, n_blk),
        in_specs=specs(fwd) + specs(bwd) + [
            pl.BlockSpec((2, 2, 128, wk), lambda i, n: (0, 0, 0, 0)),
            pl.BlockSpec((2, 1, wk), lambda i, n: (0, 0, 0))],
        out_specs=[pl.BlockSpec((nb, G_CHUNK, D), lambda i, n: (i, fwd(n), 0)),
                   pl.BlockSpec((nb, G_CHUNK, D), lambda i, n: (i, bwd(n), 0))],
        out_shape=[out_sd, out_sd],
        scratch_shapes=[state, state],
        compiler_params=_cparams(("parallel", "arbitrary"), 24),
        name="gla",
    )(z, z, z, zl, z, z, z, zl, w_a2, b_a)


def _mix_kernel(ogf_ref, ogb_ref, gout_ref, orf_ref, orb_ref, bonus_ref, gr_ref, gg_ref, grw_ref, x_ref,
                gnw_ref, lnw_ref, lnb_ref, e_ref, et_ref, wout_ref, m2_ref, n2w_ref, m3_ref, m4_ref,
                h1_o, hmb_o):
    og = ogf_ref[0].astype(F32) + ogb_ref[0].astype(F32)
    parts = []
    for h in range(G_HEADS):
        seg = og[:, h * G_DV:(h + 1) * G_DV]
        parts.append(seg * lax.rsqrt(jnp.mean(seg * seg, axis=-1, keepdims=True) + EPS))
    gout = gout_ref[0].astype(F32)
    y_gla = jnp.concatenate(parts, axis=-1) * gnw_ref[...] * (gout * _sigmoid(gout))

    e = e_ref[...]
    et = et_ref[...]
    o_r = orf_ref[0].astype(F32) + orb_ref[0].astype(F32)
    mu = _dot_x2(_dot_x2(o_r, e) * (1.0 / R_N), et)
    dlt = o_r - mu
    var = _dot_x2(dlt * dlt, e) * (1.0 / R_N)
    o_r = (dlt * _dot_x2(lax.rsqrt(var + R_LN_EPS), et) * lnw_ref[...] + lnb_ref[...]
           + bonus_ref[0].astype(F32))
    y_rwkv = o_r * gr_ref[0].astype(F32)

    y = _sigmoid(gg_ref[0].astype(F32)) * y_gla + _sigmoid(grw_ref[0].astype(F32)) * y_rwkv
    h1 = x_ref[0] + m2_ref[0] * _dot(y.astype(BF16), wout_ref[...])
    h1_o[0] = h1
    hn = h1 * lax.rsqrt(jnp.mean(h1 * h1, axis=-1, keepdims=True) + EPS) * n2w_ref[...]
    hmb_o[0] = (hn * (1.0 + m4_ref[0]) + m3_ref[0]).astype(BF16)


def _mix(og_f, og_b, z, or_f, or_b, bonus, gate_r, x, gnw, lnw, lnb, e, et, w_out, m2, n2w, m3, m4):
    b, tl, _ = x.shape
    off = N_CTX // ROWS
    seq = lambda c: pl.BlockSpec((1, ROWS, D), lambda i, j: (i, j + off, c))
    full = lambda *s: pl.BlockSpec(s, lambda i, j: (0,) * len(s))
    per_b = pl.BlockSpec((1, 1, D), lambda i, j: (i, 0, 0))
    lat = pl.BlockSpec((1, ROWS, D), lambda i, j: (i, j, 0))
    return pl.pallas_call(
        _mix_kernel,
        grid=(b, tl // ROWS),
        in_specs=[seq(0), seq(0), seq((C_GLA + 2048) // D), seq(0), seq(0), seq(0), seq(0),
                  seq(C_GATE // D), seq(C_GATE // D + 1), lat,
                  full(1, D), full(1, D), full(1, D), full(D, R_HEADS), full(R_HEADS, D), full(D, D),
                  per_b, full(1, D), per_b, per_b],
        out_specs=[lat, lat],
        out_shape=[jax.ShapeDtypeStruct((b, tl, D), F32), jax.ShapeDtypeStruct((b, tl, D), BF16)],
        compiler_params=_cparams(("parallel", "parallel"), 52),
        name="mix",
    )(og_f, og_b, z, or_f, or_b, bonus, gate_r, z, z, x, gnw, lnw, lnb, e, et, w_out, m2, n2w, m3, m4)


ROUTE_TOK = 512


N_CAND = 80


def _dup_bf16(x):
    hi = lax.bitcast_convert_type(x.astype(BF16).astype(F32), jnp.uint32)
    return hi | (hi >> 16)


def _route_kernel(x_ref, wq_ref, keys_ref, n_o, e1_o, rho_o, e2_o,
                  q_scr, s_scr, t1_scr, t2_scr, cand_scr, rho_scr, cnt_scr):
    q_scr[...] = _dot(wq_ref[...], x_ref[...]).astype(BF16)
    ninf = -jnp.inf
    half = P_TOPK // 2
    tn = x_ref.shape[1]
    top_row = jnp.full((1, tn), jnp.inf, F32)

    def next_below(vals, bound):
        below = vals < bound
        return below, jnp.max(jnp.where(below, vals, ninf), axis=0, keepdims=True)

    def head(h, carry):
        q1 = q_scr[pl.ds(pl.multiple_of(h * 2 * P_KEYS, P_KEYS), P_KEYS), :]
        q2 = q_scr[pl.ds(pl.multiple_of(h * 2 * P_KEYS + P_KEYS, P_KEYS), P_KEYS), :]
        s_scr[0] = _dot(keys_ref[2 * h], q1)
        s_scr[1] = _dot(keys_ref[2 * h + 1], q2)
        rho_scr[...] = jnp.full(rho_scr.shape, -1.0, F32)

        def extract(i, bounds):
            m1, m2 = bounds
            _, m1 = next_below(s_scr[0], m1)
            below2, m2 = next_below(s_scr[1], m2)
            rho_scr[...] += jnp.where(below2, 1.0, 0.0)
            t1_scr[pl.ds(i, 1), :] = m1
            t2_scr[pl.ds(i, 1), :] = m2
            return m1, m2

        _, last2 = lax.fori_loop(0, P_TOPK, extract, (top_row, top_row))
        s1 = s_scr[0]
        s2 = s_scr[1]
        t1 = t1_scr[...]
        t2 = t2_scr[...]
        cand_scr[0:P_TOPK] = t1[0:1] + t2
        for i in range(1, half):
            cand_scr[P_TOPK + half * (i - 1):P_TOPK + half * i] = t1[i:i + 1] + t2[0:half]
        cand_scr[N_CAND - half:N_CAND] = t1[half:P_TOPK] + t2[0:1]
        cmax = t1[0:1] + t2[0:1]

        def pick(i, st):
            m, z = st
            _, m = next_below(cand_scr[...], m)
            return m, z + jnp.exp(m - cmax)

        tau, zsum = lax.fori_loop(0, P_TOPK, pick, (top_row, jnp.zeros_like(cmax)))
        cnt_scr[...] = jnp.zeros_like(cnt_scr)

        def count(j, c):
            cnt_scr[...] += jnp.where(s_scr[0] + t2_scr[pl.ds(j, 1), :] >= tau, 1.0, 0.0)
            return c

        lax.fori_loop(0, P_TOPK, count, 0)
        sel1 = s1 >= t1[P_TOPK - 1:P_TOPK]
        sel2 = s2 >= last2
        rho = rho_scr[...] + jnp.where(s2 < last2, 1.0, 0.0)
        n_o[h] = _dup_bf16(jnp.where(sel1, cnt_scr[...], 0.0))
        e1_o[h] = _dup_bf16(jnp.where(sel1, jnp.exp(s1 - t1[0:1]), 0.0))
        rho_o[h] = rho.astype(BF16)
        e2_o[h] = jnp.where(sel2, jnp.exp(s2 - t2[0:1]) / zsum, 0.0).astype(BF16)
        return carry

    lax.fori_loop(0, P_HEADS, head, 0)


def _route(hm_t, wq_t, keys):
    n = hm_t.shape[1]
    tn = ROUTE_TOK
    big = pl.BlockSpec((P_HEADS, P_KEYS, tn), lambda i: (0, 0, i))
    sd = lambda dt: jax.ShapeDtypeStruct((P_HEADS, P_KEYS, n), dt)
    return pl.pallas_call(
        _route_kernel,
        grid=(n // tn,),
        in_specs=[pl.BlockSpec((D, tn), lambda i: (0, i)),
                  pl.BlockSpec((2 * P_HEADS * P_KEYS, D), lambda i: (0, 0)),
                  pl.BlockSpec((2 * P_HEADS, P_KEYS, P_KEYS), lambda i: (0, 0, 0))],
        out_specs=[big, big, big, big],
        out_shape=[sd(jnp.uint32), sd(jnp.uint32), sd(BF16), sd(BF16)],
        scratch_shapes=[pltpu.VMEM((2 * P_HEADS * P_KEYS, tn), BF16), pltpu.VMEM((2, P_KEYS, tn), F32),
                        pltpu.VMEM((P_TOPK, tn), F32), pltpu.VMEM((P_TOPK, tn), F32),
                        pltpu.VMEM((N_CAND, tn), F32),
                        pltpu.VMEM((P_KEYS, tn), F32), pltpu.VMEM((P_KEYS, tn), F32)],
        compiler_params=_cparams(("parallel",), 48),
        name="route",
    )(hm_t, wq_t, keys)


PEER_TOK = 512
PEER_EXP = 2048
PEER_SUB = 512


def _gelu(x):
    return 0.5 * x * (1.0 + lax.erf(x * (2.0 ** -0.5)))


BF16_ROWS = 16


def _peer_kernel(x_ref, u_ref, vt_ref, n_ref, e1_ref, rho_ref, e2_ref, h1_ref, m5_ref, fw_ref, o_ref, acc_ref):
    j = pl.program_id(1)

    @pl.when(j == 0)
    def _():
        acc_ref[...] = jnp.zeros_like(acc_ref)

    tn = x_ref.shape[1]
    x = x_ref[...]
    groups = P_KEYS // BF16_ROWS
    a_per_half = PEER_SUB // P_KEYS
    p_halves = []
    for half in range(PEER_EXP // PEER_SUB):
        rows_h = PEER_SUB
        act = _dot(u_ref[half * rows_h:(half + 1) * rows_h, :], x)
        gates = []
        for al in range(half * a_per_half, (half + 1) * a_per_half):
            g = [None] * groups
            for h in range(P_HEADS):
                row = lambda ref: pltpu.bitcast(jnp.broadcast_to(ref[h, al:al + 1, :], (8, tn)), BF16)
                n_row = row(n_ref)
                e1_row = row(e1_ref)
                for q in range(groups):
                    rows = slice(q * BF16_ROWS, (q + 1) * BF16_ROWS)
                    term = jnp.where(rho_ref[h, rows, :] < n_row, e2_ref[h, rows, :], 0.0) * e1_row
                    g[q] = term if g[q] is None else g[q] + term
            gates.extend(g)
        p_halves.append(_gelu(act).astype(BF16) * jnp.concatenate(gates, axis=0))
    acc_ref[...] += _dot(vt_ref[...], jnp.concatenate(p_halves, axis=0))

    @pl.when(j == pl.num_programs(1) - 1)
    def _():
        h = h1_ref[...] + m5_ref[0] * acc_ref[...].T
        o_ref[...] = h * lax.rsqrt(jnp.mean(h * h, axis=-1, keepdims=True) + EPS) * fw_ref[...]


def _peer(hm_t, u_b, vt_b, n_cnt, e1, rho, e2, h1, m5, final_w, seq):
    n = hm_t.shape[1]
    n_exp = u_b.shape[0]
    tn, te = PEER_TOK, PEER_EXP
    big = pl.BlockSpec((P_HEADS, P_KEYS, tn), lambda i, j: (0, 0, i))
    a_rows = pl.BlockSpec((P_HEADS, te // P_KEYS, tn), lambda i, j: (0, j, i))
    tok = pl.BlockSpec((tn, D), lambda i, j: (i, 0))
    return pl.pallas_call(
        _peer_kernel,
        grid=(n // tn, n_exp // te),
        in_specs=[pl.BlockSpec((D, tn), lambda i, j: (0, i)),
                  pl.BlockSpec((te, D), lambda i, j: (j, 0)),
                  pl.BlockSpec((D, te), lambda i, j: (0, j)),
                  a_rows, a_rows, big, big,
                  tok, pl.BlockSpec((1, 1, D), lambda i, j: (i // (seq // tn), 0, 0)),
                  pl.BlockSpec((1, D), lambda i, j: (0, 0))],
        out_specs=tok,
        out_shape=jax.ShapeDtypeStruct((n, D), F32),
        scratch_shapes=[pltpu.VMEM((D, tn), F32)],
        compiler_params=_cparams(("parallel", "arbitrary"), 52),
        name="peer",
    )(hm_t, u_b, vt_b, n_cnt, e1, rho, e2, h1, m5, final_w)


def _reorder_cols(w):
    gla_main, gla_lora = w[..., 0:3072], w[..., 3072:3104]
    rw_main, rw_lora = w[..., 3104:6176], w[..., 6176:6528]
    gates = w[..., 6528:8576]
    pad = jnp.zeros(w.shape[:-1] + (512 - 32 - 352,), w.dtype)
    return jnp.concatenate([rw_main, gla_main, gates, gla_lora, rw_lora, pad], axis=-1)


def _hi_lo(w):
    hi = w.astype(BF16)
    return jnp.stack([hi, (w - hi.astype(F32)).astype(BF16)])


def _to_chains(a):
    b, t, _ = a.shape
    return a.reshape(b, t, R_HEADS, R_N).transpose(1, 3, 0, 2).reshape(t, R_N, b * R_HEADS)


def _from_chains(a, b):
    t = a.shape[0]
    return a.reshape(t, R_N, b, R_HEADS).transpose(2, 0, 3, 1).reshape(b, t, D)


def kernel(x, c, ctx, c_ctx, norm1_w, w_mod, b_mod, w_in, gla_w_a2, gla_b_a, gla_norm_w, rwkv_mu, rwkv_w0, rwkv_w2, rwkv_a0, rwkv_a2, rwkv_g2, rwkv_k_k, rwkv_k_a, rwkv_r_k, rwkv_ln_w, rwkv_ln_b, w_out, norm2_w, peer_w_q, peer_sub_keys, peer_u, peer_v, final_norm_w):
    b, seq, _ = x.shape
    assert w_in.shape[0] == 1 and ctx.shape[1] == N_CTX and seq % ROWS == 0
    row = lambda v: v.reshape(1, -1)

    c16 = jnp.zeros((16, D), F32).at[:b].set(c).at[b].set(c_ctx)
    m = _mod(c16, w_mod[0], row(b_mod[0])).reshape(16, N_MOD, D)
    m_lat, m_ctx = m[:b], m[b]
    per_b = lambda i: m_lat[:, i].reshape(b, 1, D)
    shift1 = jnp.stack([jnp.broadcast_to(m_ctx[0], (b, D)), m_lat[:, 0]], axis=1).reshape(b, 2, 1, D)
    scale1 = jnp.stack([jnp.broadcast_to(m_ctx[1], (b, D)), m_lat[:, 1]], axis=1).reshape(b, 2, 1, D)

    h_all = jnp.concatenate([ctx, x], axis=1)
    z, z_lora = _inproj(h_all, row(norm1_w[0]), scale1, shift1, _reorder_cols(w_in[0]).astype(BF16))

    mu = rwkv_mu[0]
    mu_lora = jnp.concatenate([jnp.zeros((32,), F32), mu[3072:], jnp.zeros((128,), F32)]).reshape(1, 512)
    e = jnp.repeat(jnp.eye(R_HEADS, dtype=BF16), R_N, axis=0)
    w2f = _hi_lo(jnp.zeros((128, D), F32).at[L_WF:L_WF + R_W_LORA].set(rwkv_w2[0, 0]))
    w2b = _hi_lo(jnp.zeros((256, D), F32).at[L_WB:L_WB + R_W_LORA].set(rwkv_w2[0, 1]))
    a2 = _hi_lo(jnp.zeros((128, D), F32).at[L_A - 128:L_A - 128 + R_A_LORA].set(rwkv_a2[0]))
    g2 = _hi_lo(jnp.zeros((256, D), F32).at[L_G - 128:L_G - 128 + R_G_LORA].set(rwkv_g2[0]))
    wa = jnp.zeros((2, 128, G_HEADS * G_DK), F32)
    wa = wa.at[0, 0:G_LORA].set(gla_w_a2[0, 0]).at[1, G_LORA:2 * G_LORA].set(gla_w_a2[0, 1])
    wa = jnp.stack([_hi_lo(wa[0]), _hi_lo(wa[1])])
    r, k2, vr, kk, kka, d_f, d_b, bonus, gate_r = _rprep(
        z, z_lora, row(mu[:3072]), mu_lora, rwkv_w0[0].reshape(2, 1, D), w2f, w2b, row(rwkv_a0[0]), a2,
        g2, row(rwkv_k_k[0]), row(rwkv_k_a[0]), row(rwkv_r_k[0]), e, e.T)

    o_r = _rwkv(_to_chains(r), _to_chains(k2), _to_chains(vr), _to_chains(kk), _to_chains(kka),
                _to_chains(d_f), _to_chains(d_b))
    or_f, or_b = _from_chains(o_r[0], b), _from_chains(o_r[1], b)

    og_f, og_b = _gla(z, z_lora, wa, gla_b_a[0].reshape(2, 1, -1))

    h1, hm_b = _mix(og_f, og_b, z, or_f, or_b, bonus, gate_r, x, row(gla_norm_w[0]), row(rwkv_ln_w[0]),
                        row(rwkv_ln_b[0]), e, e.T, w_out[0].astype(BF16), per_b(2), row(norm2_w[0]),
                        per_b(3), per_b(4))

    n_tok = b * seq
    hm_t = hm_b.reshape(n_tok, D).T
    n_cnt, e1, rho, e2 = _route(hm_t, peer_w_q[0].T.astype(BF16),
                                peer_sub_keys[0].reshape(2 * P_HEADS, P_KEYS, -1).astype(BF16))
    assert seq % PEER_TOK == 0
    out = _peer(hm_t, peer_u[0].astype(BF16), peer_v[0].T.astype(BF16), n_cnt, e1, rho, e2,
                h1.reshape(n_tok, D), per_b(5), row(final_norm_w), seq)
    return out.reshape(b, seq, D)
```

```python
import functools

import numpy as np
import jax
import jax.numpy as jnp
from jax import lax
from jax.experimental import pallas as pl
from jax.experimental.pallas import tpu as pltpu

F32 = jnp.float32
BF16 = jnp.bfloat16
HI = lax.Precision.HIGHEST

D = 1024
GRID_COLS = 64
N_CTX = 256
EPS = 1e-6
N_MOD = 6

G_HEADS = 4
G_DV = D // G_HEADS
G_DK = G_DV // 2
G_LORA = 16
G_GATE_NORM = 16.0
G_CHUNK = 64

R_N = 64
R_HEADS = D // R_N
R_LN_EPS = 64e-5
R_W_LORA = 64
R_A_LORA = 64
R_G_LORA = 160

P_HEADS = 8
P_KEYS = 128
P_TOPK = 16

C_RWKV = 0
C_GLA = 3072
C_GATE = 6144
C_LORA = 8192
LORA_W = 512
N_PROJ = C_LORA + LORA_W
L_WF, L_WB, L_A, L_G = 32, 96, 160, 224

ROWS = 256
V7X_VMEM_BYTES = 64 * 1024 * 1024


def _cparams(sem, vmem_mb):
    return pltpu.CompilerParams(dimension_semantics=sem,
                                vmem_limit_bytes=min(vmem_mb * 1024 * 1024, V7X_VMEM_BYTES - (8 << 20)))


def _sigmoid(x):
    return 1.0 / (1.0 + jnp.exp(-x))


def _softplus(x):
    return jnp.maximum(x, 0.0) + jnp.log(1.0 + jnp.exp(-jnp.abs(x)))


def _dot(a, b, precision=None):
    return jnp.dot(a, b, preferred_element_type=F32, precision=precision)


def _dot_nt(a, b, precision=None):
    return lax.dot_general(a, b, (((1,), (1,)), ((), ())), preferred_element_type=F32, precision=precision)


def _dot_tn(a, b, precision=None):
    return lax.dot_general(a, b, (((0,), (0,)), ((), ())), preferred_element_type=F32, precision=precision)


def _split2(x):
    hi = x.astype(BF16)
    return hi, (x - hi.astype(F32)).astype(BF16)


def _dot_x2(x, w):
    hi, lo = _split2(x)
    return _dot(hi, w) + _dot(lo, w)


def _dot_2x(w, x):
    hi, lo = _split2(x)
    return _dot(w, hi) + _dot(w, lo)


def _dot_x3(x, w_ref):
    hi, lo = _split2(x)
    return _dot(hi, w_ref[0]) + (_dot(lo, w_ref[0]) + _dot(hi, w_ref[1]))


def _mod_kernel(c_ref, w_ref, b_ref, o_ref):
    c = c_ref[...]
    s = c * _sigmoid(c)
    o_ref[...] = _dot(s.astype(BF16), w_ref[...].astype(BF16)) + b_ref[...]


def _mod(c16, w_mod, b_mod):
    n = w_mod.shape[1]
    tn = 768
    return pl.pallas_call(
        _mod_kernel,
        grid=(n // tn,),
        in_specs=[pl.BlockSpec((16, D), lambda j: (0, 0)),
                  pl.BlockSpec((D, tn), lambda j: (0, j)),
                  pl.BlockSpec((1, tn), lambda j: (0, j))],
        out_specs=pl.BlockSpec((16, tn), lambda j: (0, j)),
        out_shape=jax.ShapeDtypeStruct((16, n), F32),
        compiler_params=_cparams(("parallel",), 24),
        name="mod",
    )(c16, w_mod, b_mod)


INPROJ_ROWS = 768


def _inproj_kernel(ctx_ref, x0_ref, x1_ref, x2_ref, nw_ref, sc_ref, sh_ref, w_ref, o_ref, ol_ref, a_scr):
    n = pl.program_id(2)

    @pl.when(n == 0)
    def _():
        first = pl.program_id(1) == 0
        for s, xs_ref in enumerate((x0_ref, x1_ref, x2_ref)):
            is_ctx = jnp.logical_and(first, s == 0)
            x = jnp.where(is_ctx, ctx_ref[0], xs_ref[0]) if s == 0 else xs_ref[0]
            y = x * lax.rsqrt(jnp.mean(x * x, axis=-1, keepdims=True) + EPS) * nw_ref[...]
            sc = jnp.where(is_ctx, sc_ref[0, 0], sc_ref[0, 1])
            sh = jnp.where(is_ctx, sh_ref[0, 0], sh_ref[0, 1])
            a_scr[s * N_CTX:(s + 1) * N_CTX] = (y * (1.0 + sc) + sh).astype(BF16)

    acc = _dot(a_scr[...], w_ref[...])
    n_main = C_LORA // LORA_W

    @pl.when(n < n_main)
    def _():
        o_ref[0] = acc.astype(BF16)

    @pl.when(n == n_main)
    def _():
        ol_ref[0] = acc


def _inproj(ctx, x, norm_w, scale, shift, w_p):
    b, seq, _ = x.shape
    t = N_CTX + seq
    tn = LORA_W
    n_main = C_LORA // tn
    per_tile = INPROJ_ROWS // N_CTX
    mod = pl.BlockSpec((1, 2, 1, D), lambda i, j, n: (i, 0, 0, 0))
    x_blk = lambda s: pl.BlockSpec((1, N_CTX, D), lambda i, j, n: (i, jnp.maximum(per_tile * j + s - 1, 0), 0))
    return pl.pallas_call(
        _inproj_kernel,
        grid=(b, t // INPROJ_ROWS, N_PROJ // tn),
        in_specs=[pl.BlockSpec((1, N_CTX, D), lambda i, j, n: (i, 0, 0)),
                  x_blk(0), x_blk(1), x_blk(2),
                  pl.BlockSpec((1, D), lambda i, j, n: (0, 0)),
                  mod, mod,
                  pl.BlockSpec((D, tn), lambda i, j, n: (0, n))],
        out_specs=[pl.BlockSpec((1, INPROJ_ROWS, tn), lambda i, j, n: (i, j, jnp.minimum(n, n_main - 1))),
                   pl.BlockSpec((1, INPROJ_ROWS, tn), lambda i, j, n: (i, j, 0))],
        out_shape=[jax.ShapeDtypeStruct((b, t, C_LORA), BF16), jax.ShapeDtypeStruct((b, t, tn), F32)],
        scratch_shapes=[pltpu.VMEM((INPROJ_ROWS, D), BF16)],
        compiler_params=_cparams(("parallel", "parallel", "arbitrary"), 32),
        name="inproj",
    )(ctx, x, x, x, norm_w, scale, shift, w_p)


def _shift_mix(zc, zp, zn, mu, is_ctx):
    rows, w = zc.shape
    row = lax.broadcasted_iota(jnp.int32, (rows, w), 0)
    lane = lax.broadcasted_iota(jnp.int32, (rows, w), 1)
    pmask = jnp.where(is_ctx, 1, 3)
    rmask = jnp.where(is_ctx, rows - 1, GRID_COLS - 1)
    cls = lane & pmask
    rr = row & rmask
    prev = pltpu.roll(zc, 1, 0)
    nxt = pltpu.roll(zc, rows - 1, 0)
    up = jnp.concatenate([zp, zc[:rows - GRID_COLS]], axis=0)
    down = jnp.concatenate([zc[GRID_COLS:], zn], axis=0)
    zs = jnp.where((cls == 0) & (rr != 0), prev,
                   jnp.where((cls == 1) & (rr != rmask), nxt,
                             jnp.where(cls == 2, up, jnp.where(cls == 3, down, 0.0))))
    return zc + (zs - zc) * mu


def _rprep_kernel(zc_ref, zp_ref, zn_ref, lc_ref, lp_ref, ln_ref,
                  mu_ref, mul_ref, w0_ref, w2f_ref, w2b_ref, a0_ref, a2_ref, g2_ref,
                  kk_w_ref, ka_w_ref, rk_w_ref, e_ref, et_ref,
                  r_o, k_o, v_o, kk_o, kka_o, df_o, db_o, bonus_o, gate_o):
    tb = pl.program_id(1)
    ntb = pl.num_programs(1)
    is_ctx = tb == 0
    has_up = (tb > 1).astype(F32)
    has_down = (tb < ntb - 1).astype(F32)

    zl = _shift_mix(lc_ref[0], lp_ref[0] * has_up, ln_ref[0] * has_down, mul_ref[...], is_ctx)
    def decay(window, w2p_ref, d):
        w = -_softplus(-(w0_ref[d] + _dot_x3(jnp.tanh(window), w2p_ref))) - 0.5
        return -jnp.exp(w)

    df_o[0] = decay(zl[:, 0:128], w2f_ref, 0).astype(BF16)
    db_o[0] = decay(zl[:, 0:256], w2b_ref, 1).astype(BF16)
    a = _sigmoid(a0_ref[...] + _dot_x3(zl[:, 128:256], a2_ref))
    gate_o[0] = _dot_x3(_sigmoid(zl[:, 128:384]), g2_ref).astype(BF16)

    def piece(i):
        sl = slice(i * D, (i + 1) * D)
        return _shift_mix(zc_ref[0, :, sl].astype(F32), zp_ref[0, :, sl].astype(F32) * has_up,
                          zn_ref[0, :, sl].astype(F32) * has_down, mu_ref[:, sl], is_ctx)

    r = piece(0)
    kr = piece(1)
    vr = piece(2)
    e = e_ref[...]
    et = et_ref[...]
    kkraw = kr * kk_w_ref[...]
    nrm = jnp.sqrt(_dot_x2(kkraw * kkraw, e))
    inv = 1.0 / jnp.maximum(nrm, 1e-12)
    kk = kkraw * _dot_x2(inv, et)
    k2 = kr * (1.0 + (a - 1.0) * ka_w_ref[...])
    rk = _dot_x2(r * k2 * rk_w_ref[...], e)
    r_o[0] = r.astype(BF16)
    k_o[0] = k2.astype(BF16)
    v_o[0] = vr.astype(BF16)
    kk_o[0] = kk.astype(BF16)
    kka_o[0] = (kk * a).astype(BF16)
    bonus_o[0] = (_dot_x2(rk, et) * vr).astype(BF16)


def _rprep(z, zl, mu_main, mu_lora, w0, w2f, w2b, a0, a2, g2, k_k, k_a, r_k, e, et):
    b, t, _ = z.shape
    ntb = t // ROWS
    q = ROWS // GRID_COLS
    last64 = t // GRID_COLS - 1
    wm = 3 * D
    cm = C_RWKV // wm
    cl = 0
    cur = lambda c: (lambda i, j: (i, j, c))
    prv = lambda c: (lambda i, j: (i, jnp.maximum(j * q - 1, 0), c))
    nxt = lambda c: (lambda i, j: (i, jnp.minimum(j * q + q, last64), c))
    full = lambda *s: pl.BlockSpec(s, lambda i, j: (0,) * len(s))
    out_spec = pl.BlockSpec((1, ROWS, D), lambda i, j: (i, j, 0))
    return pl.pallas_call(
        _rprep_kernel,
        grid=(b, ntb),
        in_specs=[pl.BlockSpec((1, ROWS, wm), cur(cm)),
                  pl.BlockSpec((1, GRID_COLS, wm), prv(cm)),
                  pl.BlockSpec((1, GRID_COLS, wm), nxt(cm)),
                  pl.BlockSpec((1, ROWS, 512), cur(cl)),
                  pl.BlockSpec((1, GRID_COLS, 512), prv(cl)),
                  pl.BlockSpec((1, GRID_COLS, 512), nxt(cl)),
                  full(1, wm), full(1, 512), full(2, 1, D), full(2, 128, D), full(2, 256, D),
                  full(1, D), full(2, 128, D), full(2, 256, D),
                  full(1, D), full(1, D), full(1, D), full(D, R_HEADS), full(R_HEADS, D)],
        out_specs=[out_spec] * 9,
        out_shape=[jax.ShapeDtypeStruct((b, t, D), BF16)] * 9,
        compiler_params=_cparams(("parallel", "parallel"), 52),
        name="rprep",
    )(z, z, z, zl, zl, zl, mu_main, mu_lora, w0, w2f, w2b, a0, a2, g2, k_k, k_a, r_k, e, et)


R_STEPS = 64


def _rwkv_kernel(r_ref, k_ref, v_ref, kk_ref, kka_ref, lwf_ref, lwb_ref, o_ref, s_ref, rows_ref):
    dirn = pl.program_id(0)
    tix = lambda i: jnp.where(dirn == 0, i, R_STEPS - 1 - i)

    @pl.when(pl.program_id(1) == 0)
    def _():
        s_ref[...] = jnp.zeros_like(s_ref)

    def weighted_sum(rows):
        acc = s_ref[0] * rows_ref[rows, 0:1, :]
        for k in range(1, R_N):
            acc = acc + s_ref[k] * rows_ref[rows, k:k + 1, :]
        return acc

    rows_ref[0] = kk_ref[tix(0)].astype(F32)
    sa0 = weighted_sum(0)

    def step(i, carry):
        sa, lcum = carry
        t = tix(i)
        t_next = tix(jnp.minimum(i + 1, R_STEPS - 1))
        lcum = lcum + jnp.where(dirn == 0, lwf_ref[t], lwb_ref[t]).astype(F32)
        e_pos = jnp.exp(lcum)
        e_neg = jnp.exp(-lcum)
        rows_ref[0] = kk_ref[t_next].astype(F32) * e_pos
        rows_ref[1] = kka_ref[t].astype(F32) * e_neg
        rows_ref[2] = k_ref[t].astype(F32) * e_neg
        rows_ref[3] = r_ref[t].astype(F32) * e_pos
        vv = v_ref[t].astype(F32)
        out = None
        sa_next = None
        for k in range(R_N):
            sk = s_ref[k] - sa * rows_ref[1, k:k + 1, :] + vv * rows_ref[2, k:k + 1, :]
            s_ref[k] = sk
            o_term = sk * rows_ref[3, k:k + 1, :]
            s_term = sk * rows_ref[0, k:k + 1, :]
            out = o_term if out is None else out + o_term
            sa_next = s_term if sa_next is None else sa_next + s_term
        o_ref[0, t] = out.astype(BF16)
        return sa_next, lcum

    _, lcum = lax.fori_loop(0, R_STEPS, step, (sa0, jnp.zeros((R_N, s_ref.shape[2]), F32)))
    rows_ref[0] = jnp.exp(lcum)
    for k in range(R_N):
        s_ref[k] = s_ref[k] * rows_ref[0, k:k + 1, :]


def _rwkv_order(dirn, nb, n_ctx_blk, n_blk):
    back = jnp.where(nb < n_ctx_blk, n_ctx_blk - 1 - nb, n_blk + n_ctx_blk - 1 - nb)
    return jnp.where(dirn == 0, nb, back)


def _rwkv(r_t, k_t, v_t, kk_t, kka_t, lwf_t, lwb_t):
    t, _, nch = r_t.shape
    n_blk = t // R_STEPS
    n_ctx_blk = N_CTX // R_STEPS
    blk = lambda d, n: (_rwkv_order(d, n, n_ctx_blk, n_blk), 0, 0)
    blk4 = lambda d, n: (d, _rwkv_order(d, n, n_ctx_blk, n_blk), 0, 0)
    only = lambda own: (lambda d, n: (jnp.where(d == own, _rwkv_order(d, n, n_ctx_blk, n_blk), 0), 0, 0))
    spec = pl.BlockSpec((R_STEPS, R_N, nch), blk)
    spec4 = pl.BlockSpec((1, R_STEPS, R_N, nch), blk4)
    return pl.pallas_call(
        _rwkv_kernel,
        grid=(2, n_blk),
        in_specs=[spec, spec, spec, spec, spec,
                  pl.BlockSpec((R_STEPS, R_N, nch), only(0)), pl.BlockSpec((R_STEPS, R_N, nch), only(1))],
        out_specs=spec4,
        out_shape=jax.ShapeDtypeStruct((2, t, R_N, nch), BF16),
        scratch_shapes=[pltpu.VMEM((R_N, R_N, nch), F32), pltpu.VMEM((4, R_N, nch), F32)],
        compiler_params=_cparams(("arbitrary", "arbitrary"), 48),
        name="rwkv",
    )(r_t, k_t, v_t, kk_t, kka_t, lwf_t, lwb_t)


def _log_sigmoid(x):
    return jnp.minimum(x, 0.0) - jnp.log(1.0 + jnp.exp(-jnp.abs(x)))


def _gla_dir(q, k, v, alo, wa, ba, st_ref, reverse):
    c = q.shape[0]
    q = q.astype(F32)
    k = k.astype(F32)
    g = _log_sigmoid(_dot_x3(alo, wa) + ba) * (1.0 / G_GATE_NORM)
    row = lax.broadcasted_iota(jnp.int32, (c, c), 0)
    col = lax.broadcasted_iota(jnp.int32, (c, c), 1)
    tri = (row <= col) if reverse else (row >= col)
    bcum = _dot_2x(tri.astype(BF16), g)
    blast = bcum[0:1] if reverse else bcum[c - 1:c]
    qd = (q * (G_DK ** -0.5) * jnp.exp(bcum)).astype(BF16)
    kd = (k * jnp.exp(-bcum)).astype(BF16)
    kl = (k * jnp.exp(blast - bcum)).astype(BF16)
    eb = jnp.exp(blast)
    vb = v.astype(BF16)
    outs = []
    for h in range(G_HEADS):
        sk = slice(h * G_DK, (h + 1) * G_DK)
        sv = slice(h * G_DV, (h + 1) * G_DV)
        att = jnp.where(tri, _dot_nt(qd[:, sk], kd[:, sk]), 0.0)
        st = st_ref[h]
        outs.append(_dot_nt(qd[:, sk], st.astype(BF16)) + _dot(att.astype(BF16), vb[:, sv]))
        st_ref[h] = st * eb[:, sk] + _dot_tn(vb[:, sv], kl[:, sk])
    return jnp.concatenate(outs, axis=-1)


GLA_BATCH = 2


def _gla_kernel(qf, kf, vf, lf, qb, kb, vb, lb, wa_ref, ba_ref, of_ref, ob_ref, sf_ref, sb_ref):
    @pl.when(pl.program_id(1) == 0)
    def _():
        sf_ref[...] = jnp.zeros_like(sf_ref)
        sb_ref[...] = jnp.zeros_like(sb_ref)

    for s in range(GLA_BATCH):
        of_ref[s] = _gla_dir(qf[s], kf[s], vf[s], lf[s], wa_ref.at[0], ba_ref[0], sf_ref.at[s], False).astype(BF16)
        ob_ref[s] = _gla_dir(qb[s], kb[s], vb[s], lb[s], wa_ref.at[1], ba_ref[1], sb_ref.at[s], True).astype(BF16)


def _gla(z, zl, w_a2, b_a):
    b, t, _ = z.shape
    n_blk = t // G_CHUNK
    n_ctx_blk = N_CTX // G_CHUNK
    fwd = lambda n: n
    bwd = lambda n: _rwkv_order(1, n, n_ctx_blk, n_blk)
    wk = G_HEADS * G_DK
    cq, ck, cv = C_GLA // wk, (C_GLA + wk) // wk, (C_GLA + 2 * wk) // D
    cl = 0

    nb = GLA_BATCH
    assert b % nb == 0

    def specs(order):
        return [pl.BlockSpec((nb, G_CHUNK, wk), lambda i, n: (i, order(n), cq)),
                pl.BlockSpec((nb, G_CHUNK, wk), lambda i, n: (i, order(n), ck)),
                pl.BlockSpec((nb, G_CHUNK, D), lambda i, n: (i, order(n), cv)),
                pl.BlockSpec((nb, G_CHUNK, 128), lambda i, n: (i, order(n), cl))]

    out_sd = jax.ShapeDtypeStruct((b, t, D), BF16)
    state = pltpu.VMEM((nb, G_HEADS, G_DV, G_DK), F32)
    return pl.pallas_call(
        _gla_kernel,
        grid=(b // nb, n_blk),
        in_specs=specs(fwd) + specs(bwd) + [
            pl.BlockSpec((2, 2, 128, wk), lambda i, n: (0, 0, 0, 0)),
            pl.BlockSpec((2, 1, wk), lambda i, n: (0, 0, 0))],
        out_specs=[pl.BlockSpec((nb, G_CHUNK, D), lambda i, n: (i, fwd(n), 0)),
                   pl.BlockSpec((nb, G_CHUNK, D), lambda i, n: (i, bwd(n), 0))],
        out_shape=[out_sd, out_sd],
        scratch_shapes=[state, state],
        compiler_params=_cparams(("parallel", "arbitrary"), 24),
        name="gla",
    )(z, z, z, zl, z, z, z, zl, w_a2, b_a)


def _mix_kernel(ogf_ref, ogb_ref, gout_ref, orf_ref, orb_ref, bonus_ref, gr_ref, gg_ref, grw_ref, x_ref,
                gnw_ref, lnw_ref, lnb_ref, e_ref, et_ref, wout_ref, m2_ref, n2w_ref, m3_ref, m4_ref,
                h1_o, hmb_o):
    og = ogf_ref[0].astype(F32) + ogb_ref[0].astype(F32)
    parts = []
    for h in range(G_HEADS):
        seg = og[:, h * G_DV:(h + 1) * G_DV]
        parts.append(seg * lax.rsqrt(jnp.mean(seg * seg, axis=-1, keepdims=True) + EPS))
    gout = gout_ref[0].astype(F32)
    y_gla = jnp.concatenate(parts, axis=-1) * gnw_ref[...] * (gout * _sigmoid(gout))

    e = e_ref[...]
    et = et_ref[...]
    o_r = orf_ref[0].astype(F32) + orb_ref[0].astype(F32)
    mu = _dot_x2(_dot_x2(o_r, e) * (1.0 / R_N), et)
    dlt = o_r - mu
    var = _dot_x2(dlt * dlt, e) * (1.0 / R_N)
    o_r = (dlt * _dot_x2(lax.rsqrt(var + R_LN_EPS), et) * lnw_ref[...] + lnb_ref[...]
           + bonus_ref[0].astype(F32))
    y_rwkv = o_r * gr_ref[0].astype(F32)

    y = _sigmoid(gg_ref[0].astype(F32)) * y_gla + _sigmoid(grw_ref[0].astype(F32)) * y_rwkv
    h1 = x_ref[0] + m2_ref[0] * _dot(y.astype(BF16), wout_ref[...])
    h1_o[0] = h1
    hn = h1 * lax.rsqrt(jnp.mean(h1 * h1, axis=-1, keepdims=True) + EPS) * n2w_ref[...]
    hmb_o[...] = (hn * (1.0 + m4_ref[0]) + m3_ref[0]).T.astype(BF16)


def _mix(og_f, og_b, z, or_f, or_b, bonus, gate_r, x, gnw, lnw, lnb, e, et, w_out, m2, n2w, m3, m4):
    b, tl, _ = x.shape
    off = N_CTX // ROWS
    seq = lambda c: pl.BlockSpec((1, ROWS, D), lambda i, j: (i, j + off, c))
    full = lambda *s: pl.BlockSpec(s, lambda i, j: (0,) * len(s))
    per_b = pl.BlockSpec((1, 1, D), lambda i, j: (i, 0, 0))
    lat = pl.BlockSpec((1, ROWS, D), lambda i, j: (i, j, 0))
    return pl.pallas_call(
        _mix_kernel,
        grid=(b, tl // ROWS),
        in_specs=[seq(0), seq(0), seq((C_GLA + 2048) // D), seq(0), seq(0), seq(0), seq(0),
                  seq(C_GATE // D), seq(C_GATE // D + 1), lat,
                  full(1, D), full(1, D), full(1, D), full(D, R_HEADS), full(R_HEADS, D), full(D, D),
                  per_b, full(1, D), per_b, per_b],
        out_specs=[lat, pl.BlockSpec((D, ROWS), lambda i, j: (0, i * (tl // ROWS) + j))],
        out_shape=[jax.ShapeDtypeStruct((b, tl, D), F32), jax.ShapeDtypeStruct((D, b * tl), BF16)],
        compiler_params=_cparams(("parallel", "parallel"), 52),
        name="mix",
    )(og_f, og_b, z, or_f, or_b, bonus, gate_r, z, z, x, gnw, lnw, lnb, e, et, w_out, m2, n2w, m3, m4)


ROUTE_TOK = 512


N_CAND = 80


def _dup_bf16(x):
    hi = lax.bitcast_convert_type(x.astype(BF16).astype(F32), jnp.uint32)
    return hi | (hi >> 16)


def _route_kernel(x_ref, wq_ref, keys_ref, n_o, e1_o, rho_o, e2_o,
                  q_scr, s_scr, t1_scr, t2_scr, cand_scr, rho_scr, cnt_scr):
    q_scr[...] = _dot(wq_ref[...], x_ref[...]).astype(BF16)
    ninf = -jnp.inf
    half = P_TOPK // 2
    tn = x_ref.shape[1]
    top_row = jnp.full((1, tn), jnp.inf, F32)

    def next_below(vals, bound):
        below = vals < bound
        return below, jnp.max(jnp.where(below, vals, ninf), axis=0, keepdims=True)

    def head(h, carry):
        q1 = q_scr[pl.ds(pl.multiple_of(h * 2 * P_KEYS, P_KEYS), P_KEYS), :]
        q2 = q_scr[pl.ds(pl.multiple_of(h * 2 * P_KEYS + P_KEYS, P_KEYS), P_KEYS), :]
        s_scr[0] = _dot(keys_ref[2 * h], q1)
        s_scr[1] = _dot(keys_ref[2 * h + 1], q2)
        rho_scr[...] = jnp.full(rho_scr.shape, -1.0, F32)

        def extract(i, bounds):
            m1, m2 = bounds
            _, m1 = next_below(s_scr[0], m1)
            below2, m2 = next_below(s_scr[1], m2)
            rho_scr[...] += jnp.where(below2, 1.0, 0.0)
            t1_scr[pl.ds(i, 1), :] = m1
            t2_scr[pl.ds(i, 1), :] = m2
            return m1, m2

        _, last2 = lax.fori_loop(0, P_TOPK, extract, (top_row, top_row))
        s1 = s_scr[0]
        s2 = s_scr[1]
        t1 = t1_scr[...]
        t2 = t2_scr[...]
        cand_scr[0:P_TOPK] = t1[0:1] + t2
        for i in range(1, half):
            cand_scr[P_TOPK + half * (i - 1):P_TOPK + half * i] = t1[i:i + 1] + t2[0:half]
        cand_scr[N_CAND - half:N_CAND] = t1[half:P_TOPK] + t2[0:1]
        cmax = t1[0:1] + t2[0:1]

        def pick(i, st):
            m, z = st
            _, m = next_below(cand_scr[...], m)
            return m, z + jnp.exp(m - cmax)

        tau, zsum = lax.fori_loop(0, P_TOPK, pick, (top_row, jnp.zeros_like(cmax)))
        cnt_scr[...] = jnp.zeros_like(cnt_scr)

        def count(j, c):
            cnt_scr[...] += jnp.where(s_scr[0] + t2_scr[pl.ds(j, 1), :] >= tau, 1.0, 0.0)
            return c

        lax.fori_loop(0, P_TOPK, count, 0)
        sel1 = s1 >= t1[P_TOPK - 1:P_TOPK]
        sel2 = s2 >= last2
        rho = rho_scr[...] + jnp.where(s2 < last2, 1.0, 0.0)
        n_o[h] = _dup_bf16(jnp.where(sel1, cnt_scr[...], 0.0))
        e1_o[h] = _dup_bf16(jnp.where(sel1, jnp.exp(s1 - t1[0:1]), 0.0))
        rho_o[h] = rho.astype(BF16)
        e2_o[h] = jnp.where(sel2, jnp.exp(s2 - t2[0:1]) / zsum, 0.0).astype(BF16)
        return carry

    lax.fori_loop(0, P_HEADS, head, 0)


def _route(hm_t, wq_t, keys):
    n = hm_t.shape[1]
    tn = ROUTE_TOK
    big = pl.BlockSpec((P_HEADS, P_KEYS, tn), lambda i: (0, 0, i))
    sd = lambda dt: jax.ShapeDtypeStruct((P_HEADS, P_KEYS, n), dt)
    return pl.pallas_call(
        _route_kernel,
        grid=(n // tn,),
        in_specs=[pl.BlockSpec((D, tn), lambda i: (0, i)),
                  pl.BlockSpec((2 * P_HEADS * P_KEYS, D), lambda i: (0, 0)),
                  pl.BlockSpec((2 * P_HEADS, P_KEYS, P_KEYS), lambda i: (0, 0, 0))],
        out_specs=[big, big, big, big],
        out_shape=[sd(jnp.uint32), sd(jnp.uint32), sd(BF16), sd(BF16)],
        scratch_shapes=[pltpu.VMEM((2 * P_HEADS * P_KEYS, tn), BF16), pltpu.VMEM((2, P_KEYS, tn), F32),
                        pltpu.VMEM((P_TOPK, tn), F32), pltpu.VMEM((P_TOPK, tn), F32),
                        pltpu.VMEM((N_CAND, tn), F32),
                        pltpu.VMEM((P_KEYS, tn), F32), pltpu.VMEM((P_KEYS, tn), F32)],
        compiler_params=_cparams(("parallel",), 48),
        name="route",
    )(hm_t, wq_t, keys)


PEER_TOK = 512
PEER_EXP = 2048
PEER_SUB = 512


def _gelu(x):
    return 0.5 * x * (1.0 + lax.erf(x * (2.0 ** -0.5)))


BF16_ROWS = 16


def _peer_kernel(x_ref, u_ref, vt_ref, n_ref, e1_ref, rho_ref, e2_ref, h1_ref, m5_ref, fw_ref, o_ref, acc_ref):
    j = pl.program_id(1)

    @pl.when(j == 0)
    def _():
        acc_ref[...] = jnp.zeros_like(acc_ref)

    tn = x_ref.shape[1]
    x = x_ref[...]
    groups = P_KEYS // BF16_ROWS
    a_per_half = PEER_SUB // P_KEYS
    p_halves = []
    for half in range(PEER_EXP // PEER_SUB):
        rows_h = PEER_SUB
        act = _dot(u_ref[half * rows_h:(half + 1) * rows_h, :], x)
        gates = []
        for al in range(half * a_per_half, (half + 1) * a_per_half):
            g = [None] * groups
            for h in range(P_HEADS):
                row = lambda ref: pltpu.bitcast(jnp.broadcast_to(ref[h, al:al + 1, :], (8, tn)), BF16)
                n_row = row(n_ref)
                e1_row = row(e1_ref)
                for q in range(groups):
                    rows = slice(q * BF16_ROWS, (q + 1) * BF16_ROWS)
                    term = jnp.where(rho_ref[h, rows, :] < n_row, e2_ref[h, rows, :], 0.0) * e1_row
                    g[q] = term if g[q] is None else g[q] + term
            gates.extend(g)
        p_halves.append(_gelu(act).astype(BF16) * jnp.concatenate(gates, axis=0))
    acc_ref[...] += _dot(vt_ref[...], jnp.concatenate(p_halves, axis=0))

    @pl.when(j == pl.num_programs(1) - 1)
    def _():
        h = h1_ref[...] + m5_ref[0] * acc_ref[...].T
        o_ref[...] = h * lax.rsqrt(jnp.mean(h * h, axis=-1, keepdims=True) + EPS) * fw_ref[...]


def _peer(hm_t, u_b, vt_b, n_cnt, e1, rho, e2, h1, m5, final_w, seq):
    n = hm_t.shape[1]
    n_exp = u_b.shape[0]
    tn, te = PEER_TOK, PEER_EXP
    big = pl.BlockSpec((P_HEADS, P_KEYS, tn), lambda i, j: (0, 0, i))
    a_rows = pl.BlockSpec((P_HEADS, te // P_KEYS, tn), lambda i, j: (0, j, i))
    tok = pl.BlockSpec((tn, D), lambda i, j: (i, 0))
    return pl.pallas_call(
        _peer_kernel,
        grid=(n // tn, n_exp // te),
        in_specs=[pl.BlockSpec((D, tn), lambda i, j: (0, i)),
                  pl.BlockSpec((te, D), lambda i, j: (j, 0)),
                  pl.BlockSpec((D, te), lambda i, j: (0, j)),
                  a_rows, a_rows, big, big,
                  tok, pl.BlockSpec((1, 1, D), lambda i, j: (i // (seq // tn), 0, 0)),
                  pl.BlockSpec((1, D), lambda i, j: (0, 0))],
        out_specs=tok,
        out_shape=jax.ShapeDtypeStruct((n, D), F32),
        scratch_shapes=[pltpu.VMEM((D, tn), F32)],
        compiler_params=_cparams(("parallel", "arbitrary"), 52),
        name="peer",
    )(hm_t, u_b, vt_b, n_cnt, e1, rho, e2, h1, m5, final_w)


def _reorder_cols(w):
    gla_main, gla_lora = w[..., 0:3072], w[..., 3072:3104]
    rw_main, rw_lora = w[..., 3104:6176], w[..., 6176:6528]
    gates = w[..., 6528:8576]
    pad = jnp.zeros(w.shape[:-1] + (512 - 32 - 352,), w.dtype)
    return jnp.concatenate([rw_main, gla_main, gates, gla_lora, rw_lora, pad], axis=-1)


def _hi_lo(w):
    hi = w.astype(BF16)
    return jnp.stack([hi, (w - hi.astype(F32)).astype(BF16)])


def _to_chains(a):
    b, t, _ = a.shape
    return a.reshape(b, t, R_HEADS, R_N).transpose(1, 3, 0, 2).reshape(t, R_N, b * R_HEADS)


def _from_chains(a, b):
    t = a.shape[0]
    return a.reshape(t, R_N, b, R_HEADS).transpose(2, 0, 3, 1).reshape(b, t, D)


def kernel(x, c, ctx, c_ctx, norm1_w, w_mod, b_mod, w_in, gla_w_a2, gla_b_a, gla_norm_w, rwkv_mu, rwkv_w0, rwkv_w2, rwkv_a0, rwkv_a2, rwkv_g2, rwkv_k_k, rwkv_k_a, rwkv_r_k, rwkv_ln_w, rwkv_ln_b, w_out, norm2_w, peer_w_q, peer_sub_keys, peer_u, peer_v, final_norm_w):
    b, seq, _ = x.shape
    assert w_in.shape[0] == 1 and ctx.shape[1] == N_CTX and seq % ROWS == 0
    row = lambda v: v.reshape(1, -1)

    c16 = jnp.zeros((16, D), F32).at[:b].set(c).at[b].set(c_ctx)
    m = _mod(c16, w_mod[0], row(b_mod[0])).reshape(16, N_MOD, D)
    m_lat, m_ctx = m[:b], m[b]
    per_b = lambda i: m_lat[:, i].reshape(b, 1, D)
    shift1 = jnp.stack([jnp.broadcast_to(m_ctx[0], (b, D)), m_lat[:, 0]], axis=1).reshape(b, 2, 1, D)
    scale1 = jnp.stack([jnp.broadcast_to(m_ctx[1], (b, D)), m_lat[:, 1]], axis=1).reshape(b, 2, 1, D)

    assert INPROJ_ROWS % N_CTX == 0 and (N_CTX + seq) % INPROJ_ROWS == 0
    z, z_lora = _inproj(ctx, x, row(norm1_w[0]), scale1, shift1, _reorder_cols(w_in[0]).astype(BF16))

    mu = rwkv_mu[0]
    mu_lora = jnp.concatenate([jnp.zeros((32,), F32), mu[3072:], jnp.zeros((128,), F32)]).reshape(1, 512)
    e = jnp.repeat(jnp.eye(R_HEADS, dtype=BF16), R_N, axis=0)
    w2f = _hi_lo(jnp.zeros((128, D), F32).at[L_WF:L_WF + R_W_LORA].set(rwkv_w2[0, 0]))
    w2b = _hi_lo(jnp.zeros((256, D), F32).at[L_WB:L_WB + R_W_LORA].set(rwkv_w2[0, 1]))
    a2 = _hi_lo(jnp.zeros((128, D), F32).at[L_A - 128:L_A - 128 + R_A_LORA].set(rwkv_a2[0]))
    g2 = _hi_lo(jnp.zeros((256, D), F32).at[L_G - 128:L_G - 128 + R_G_LORA].set(rwkv_g2[0]))
    wa = jnp.zeros((2, 128, G_HEADS * G_DK), F32)
    wa = wa.at[0, 0:G_LORA].set(gla_w_a2[0, 0]).at[1, G_LORA:2 * G_LORA].set(gla_w_a2[0, 1])
    wa = jnp.stack([_hi_lo(wa[0]), _hi_lo(wa[1])])
    r, k2, vr, kk, kka, d_f, d_b, bonus, gate_r = _rprep(
        z, z_lora, row(mu[:3072]), mu_lora, rwkv_w0[0].reshape(2, 1, D), w2f, w2b, row(rwkv_a0[0]), a2,
        g2, row(rwkv_k_k[0]), row(rwkv_k_a[0]), row(rwkv_r_k[0]), e, e.T)

    o_r = _rwkv(_to_chains(r), _to_chains(k2), _to_chains(vr), _to_chains(kk), _to_chains(kka),
                _to_chains(d_f), _to_chains(d_b))
    or_f, or_b = _from_chains(o_r[0], b), _from_chains(o_r[1], b)

    og_f, og_b = _gla(z, z_lora, wa, gla_b_a[0].reshape(2, 1, -1))

    h1, hm_t = _mix(og_f, og_b, z, or_f, or_b, bonus, gate_r, x, row(gla_norm_w[0]), row(rwkv_ln_w[0]),
                        row(rwkv_ln_b[0]), e, e.T, w_out[0].astype(BF16), per_b(2), row(norm2_w[0]),
                        per_b(3), per_b(4))

    n_tok = b * seq
    n_cnt, e1, rho, e2 = _route(hm_t, peer_w_q[0].T.astype(BF16),
                                peer_sub_keys[0].reshape(2 * P_HEADS, P_KEYS, -1).astype(BF16))
    assert seq % PEER_TOK == 0
    out = _peer(hm_t, peer_u[0].astype(BF16), peer_v[0].T.astype(BF16), n_cnt, e1, rho, e2,
                h1.reshape(n_tok, D), per_b(5), row(final_norm_w), seq)
    return out.reshape(b, seq, D)
```

```python
import jax
import jax.numpy as jnp
from jax import lax
from jax.experimental import pallas as pl
from jax.experimental.pallas import tpu as pltpu

F32 = jnp.float32
BF16 = jnp.bfloat16

D = 1024
GRID_COLS = 64
N_CTX = 256
EPS = 1e-6
N_MOD = 6

G_HEADS = 4
G_DV = D // G_HEADS
G_DK = G_DV // 2
G_LORA = 16
G_GATE_NORM = 16.0
G_CHUNK = 64

R_N = 64
R_HEADS = D // R_N
R_LN_EPS = 64e-5
R_W_LORA = 64
R_A_LORA = 64
R_G_LORA = 160

P_HEADS = 8
P_KEYS = 128
P_TOPK = 16

C_RWKV = 0
C_GLA = 3072
C_GATE = 6144
C_LORA = 8192
LORA_W = 512
N_PROJ = C_LORA + LORA_W
L_WF, L_WB, L_A, L_G = 32, 96, 160, 224

ROWS = 256
V7X_VMEM_BYTES = 64 * 1024 * 1024


def _cparams(sem, vmem_mb):
    return pltpu.CompilerParams(dimension_semantics=sem,
                                vmem_limit_bytes=min(vmem_mb * 1024 * 1024, V7X_VMEM_BYTES - (8 << 20)))


def _sigmoid(x):
    return 1.0 / (1.0 + jnp.exp(-x))


def _softplus(x):
    return jnp.maximum(x, 0.0) + jnp.log(1.0 + jnp.exp(-jnp.abs(x)))


def _dot(a, b):
    return jnp.dot(a, b, preferred_element_type=F32)


def _dot_nt(a, b):
    return lax.dot_general(a, b, (((1,), (1,)), ((), ())), preferred_element_type=F32)


def _dot_tn(a, b):
    return lax.dot_general(a, b, (((0,), (0,)), ((), ())), preferred_element_type=F32)


def _split2(x):
    hi = x.astype(BF16)
    return hi, (x - hi.astype(F32)).astype(BF16)


def _dot_x2(x, w):
    hi, lo = _split2(x)
    return _dot(hi, w) + _dot(lo, w)


def _dot_2x(w, x):
    hi, lo = _split2(x)
    return _dot(w, hi) + _dot(w, lo)


def _dot_x3(x, w_ref):
    hi, lo = _split2(x)
    return _dot(hi, w_ref[0]) + (_dot(lo, w_ref[0]) + _dot(hi, w_ref[1]))


def _mod_kernel(c_ref, w_ref, b_ref, o_ref):
    c = c_ref[...]
    s = c * _sigmoid(c)
    o_ref[...] = _dot(s.astype(BF16), w_ref[...].astype(BF16)) + b_ref[...]


def _mod(c16, w_mod, b_mod):
    n = w_mod.shape[1]
    tn = 768
    return pl.pallas_call(
        _mod_kernel,
        grid=(n // tn,),
        in_specs=[pl.BlockSpec((16, D), lambda j: (0, 0)),
                  pl.BlockSpec((D, tn), lambda j: (0, j)),
                  pl.BlockSpec((1, tn), lambda j: (0, j))],
        out_specs=pl.BlockSpec((16, tn), lambda j: (0, j)),
        out_shape=jax.ShapeDtypeStruct((16, n), F32),
        compiler_params=_cparams(("parallel",), 24),
        name="mod",
    )(c16, w_mod, b_mod)


INPROJ_ROWS = 768


def _inproj_kernel(h_ref, nw_ref, sc_ref, sh_ref, w_ref, o_ref, ol_ref, a_scr):
    n = pl.program_id(2)

    @pl.when(n == 0)
    def _():
        x = h_ref[0]
        y = x * lax.rsqrt(jnp.mean(x * x, axis=-1, keepdims=True) + EPS) * nw_ref[...]
        t = pl.program_id(1) * INPROJ_ROWS + lax.broadcasted_iota(jnp.int32, (INPROJ_ROWS, 1), 0)
        is_ctx = t < N_CTX
        sc = jnp.where(is_ctx, sc_ref[0, 0], sc_ref[0, 1])
        sh = jnp.where(is_ctx, sh_ref[0, 0], sh_ref[0, 1])
        a_scr[...] = (y * (1.0 + sc) + sh).astype(BF16)

    acc = _dot(a_scr[...], w_ref[...])
    n_main = C_LORA // LORA_W

    @pl.when(n < n_main)
    def _():
        o_ref[0] = acc.astype(BF16)

    @pl.when(n == n_main)
    def _():
        ol_ref[0] = acc


def _inproj(h_all, norm_w, scale, shift, w_p):
    b, t, _ = h_all.shape
    tn = LORA_W
    n_main = C_LORA // tn
    mod = pl.BlockSpec((1, 2, 1, D), lambda i, j, n: (i, 0, 0, 0))
    return pl.pallas_call(
        _inproj_kernel,
        grid=(b, t // INPROJ_ROWS, N_PROJ // tn),
        in_specs=[pl.BlockSpec((1, INPROJ_ROWS, D), lambda i, j, n: (i, j, 0)),
                  pl.BlockSpec((1, D), lambda i, j, n: (0, 0)),
                  mod, mod,
                  pl.BlockSpec((D, tn), lambda i, j, n: (0, n))],
        out_specs=[pl.BlockSpec((1, INPROJ_ROWS, tn), lambda i, j, n: (i, j, jnp.minimum(n, n_main - 1))),
                   pl.BlockSpec((1, INPROJ_ROWS, tn), lambda i, j, n: (i, j, 0))],
        out_shape=[jax.ShapeDtypeStruct((b, t, C_LORA), BF16), jax.ShapeDtypeStruct((b, t, tn), F32)],
        scratch_shapes=[pltpu.VMEM((INPROJ_ROWS, D), BF16)],
        compiler_params=_cparams(("parallel", "parallel", "arbitrary"), 32),
        name="inproj",
    )(h_all, norm_w, scale, shift, w_p)


def _shift_mix(zc, zp, zn, mu, is_ctx):
    rows, w = zc.shape
    row = lax.broadcasted_iota(jnp.int32, (rows, w), 0)
    lane = lax.broadcasted_iota(jnp.int32, (rows, w), 1)
    pmask = jnp.where(is_ctx, 1, 3)
    rmask = jnp.where(is_ctx, rows - 1, GRID_COLS - 1)
    cls = lane & pmask
    rr = row & rmask
    prev = pltpu.roll(zc, 1, 0)
    nxt = pltpu.roll(zc, rows - 1, 0)
    up = jnp.concatenate([zp, zc[:rows - GRID_COLS]], axis=0)
    down = jnp.concatenate([zc[GRID_COLS:], zn], axis=0)
    zs = jnp.where(cls == 0, jnp.where(rr != 0, prev, 0.0),
                   jnp.where(cls == 1, jnp.where(rr != rmask, nxt, 0.0),
                             jnp.where(cls == 2, up, down)))
    return zc + (zs - zc) * mu


def _rprep_kernel(zc_ref, zp_ref, zn_ref, lc_ref, lp_ref, ln_ref,
                  mu_ref, mul_ref, w0_ref, w2f_ref, w2b_ref, a0_ref, a2_ref, g2_ref,
                  kk_w_ref, ka_w_ref, rk_w_ref, e_ref, et_ref,
                  r_o, k_o, v_o, kk_o, kka_o, df_o, db_o, bonus_o, gate_o):
    tb = pl.program_id(1)
    ntb = pl.num_programs(1)
    is_ctx = tb == 0
    has_up = (tb > 1).astype(F32)
    has_down = (tb < ntb - 1).astype(F32)

    zl = _shift_mix(lc_ref[0], lp_ref[0] * has_up, ln_ref[0] * has_down, mul_ref[...], is_ctx)
    def decay(window, w2p_ref, d):
        w = -_softplus(-(w0_ref[d] + _dot_x3(jnp.tanh(window), w2p_ref))) - 0.5
        return -jnp.exp(w)

    df_o[0] = decay(zl[:, 0:128], w2f_ref, 0).astype(BF16)
    db_o[0] = decay(zl[:, 0:256], w2b_ref, 1).astype(BF16)
    a = _sigmoid(a0_ref[...] + _dot_x3(zl[:, 128:256], a2_ref))
    gate_o[0] = _dot_x3(_sigmoid(zl[:, 128:384]), g2_ref).astype(BF16)

    def piece(i):
        sl = slice(i * D, (i + 1) * D)
        return _shift_mix(zc_ref[0, :, sl].astype(F32), zp_ref[0, :, sl].astype(F32) * has_up,
                          zn_ref[0, :, sl].astype(F32) * has_down, mu_ref[:, sl], is_ctx)

    r = piece(0)
    kr = piece(1)
    vr = piece(2)
    e = e_ref[...]
    et = et_ref[...]
    kkraw = kr * kk_w_ref[...]
    nrm = jnp.sqrt(_dot_x2(kkraw * kkraw, e))
    inv = 1.0 / jnp.maximum(nrm, 1e-12)
    kk = kkraw * _dot_x2(inv, et)
    k2 = kr * (1.0 + (a - 1.0) * ka_w_ref[...])
    rk = _dot_x2(r * k2 * rk_w_ref[...], e)
    r_o[0] = r.astype(BF16)
    k_o[0] = k2.astype(BF16)
    v_o[0] = vr.astype(BF16)
    kk_o[0] = kk.astype(BF16)
    kka_o[0] = (kk * a).astype(BF16)
    bonus_o[0] = (_dot_x2(rk, et) * vr).astype(BF16)


def _rprep(z, zl, mu_main, mu_lora, w0, w2f, w2b, a0, a2, g2, k_k, k_a, r_k, e, et):
    b, t, _ = z.shape
    ntb = t // ROWS
    q = ROWS // GRID_COLS
    last64 = t // GRID_COLS - 1
    wm = 3 * D
    cm = C_RWKV // wm
    cl = 0
    cur = lambda c: (lambda i, j: (i, j, c))
    prv = lambda c: (lambda i, j: (i, jnp.maximum(j * q - 1, 0), c))
    nxt = lambda c: (lambda i, j: (i, jnp.minimum(j * q + q, last64), c))
    full = lambda *s: pl.BlockSpec(s, lambda i, j: (0,) * len(s))
    out_spec = pl.BlockSpec((1, ROWS, D), lambda i, j: (i, j, 0))
    return pl.pallas_call(
        _rprep_kernel,
        grid=(b, ntb),
        in_specs=[pl.BlockSpec((1, ROWS, wm), cur(cm)),
                  pl.BlockSpec((1, GRID_COLS, wm), prv(cm)),
                  pl.BlockSpec((1, GRID_COLS, wm), nxt(cm)),
                  pl.BlockSpec((1, ROWS, 512), cur(cl)),
                  pl.BlockSpec((1, GRID_COLS, 512), prv(cl)),
                  pl.BlockSpec((1, GRID_COLS, 512), nxt(cl)),
                  full(1, wm), full(1, 512), full(2, 1, D), full(2, 128, D), full(2, 256, D),
                  full(1, D), full(2, 128, D), full(2, 256, D),
                  full(1, D), full(1, D), full(1, D), full(D, R_HEADS), full(R_HEADS, D)],
        out_specs=[out_spec] * 9,
        out_shape=[jax.ShapeDtypeStruct((b, t, D), BF16)] * 9,
        compiler_params=_cparams(("parallel", "parallel"), 52),
        name="rprep",
    )(z, z, z, zl, zl, zl, mu_main, mu_lora, w0, w2f, w2b, a0, a2, g2, k_k, k_a, r_k, e, et)


R_STEPS = 64


def _rwkv_kernel(r_ref, k_ref, v_ref, kk_ref, kka_ref, lwf_ref, lwb_ref, o_ref, s_ref, rows_ref):
    dirn = pl.program_id(0)
    tix = lambda i: jnp.where(dirn == 0, i, R_STEPS - 1 - i)

    @pl.when(pl.program_id(1) == 0)
    def _():
        s_ref[...] = jnp.zeros_like(s_ref)

    def weighted_sum(rows):
        acc = s_ref[0] * rows_ref[rows, 0:1, :]
        for k in range(1, R_N):
            acc = acc + s_ref[k] * rows_ref[rows, k:k + 1, :]
        return acc

    rows_ref[0] = kk_ref[tix(0)].astype(F32)
    sa0 = weighted_sum(0)

    def step(i, carry):
        sa, lcum = carry
        t = tix(i)
        t_next = tix(jnp.minimum(i + 1, R_STEPS - 1))
        lcum = lcum + jnp.where(dirn == 0, lwf_ref[t], lwb_ref[t]).astype(F32)
        e_pos = jnp.exp(lcum)
        e_neg = jnp.exp(-lcum)
        rows_ref[0] = kk_ref[t_next].astype(F32) * e_pos
        rows_ref[1] = kka_ref[t].astype(F32) * e_neg
        rows_ref[2] = k_ref[t].astype(F32) * e_neg
        rows_ref[3] = r_ref[t].astype(F32) * e_pos
        vv = v_ref[t].astype(F32)
        out = None
        sa_next = None
        for k in range(R_N):
            sk = s_ref[k] - sa * rows_ref[1, k:k + 1, :] + vv * rows_ref[2, k:k + 1, :]
            s_ref[k] = sk
            o_term = sk * rows_ref[3, k:k + 1, :]
            s_term = sk * rows_ref[0, k:k + 1, :]
            out = o_term if out is None else out + o_term
            sa_next = s_term if sa_next is None else sa_next + s_term
        o_ref[0, t] = out.astype(BF16)
        return sa_next, lcum

    _, lcum = lax.fori_loop(0, R_STEPS, step, (sa0, jnp.zeros((R_N, s_ref.shape[2]), F32)))
    rows_ref[0] = jnp.exp(lcum)
    for k in range(R_N):
        s_ref[k] = s_ref[k] * rows_ref[0, k:k + 1, :]


def _rwkv_order(dirn, nb, n_ctx_blk, n_blk):
    back = jnp.where(nb < n_ctx_blk, n_ctx_blk - 1 - nb, n_blk + n_ctx_blk - 1 - nb)
    return jnp.where(dirn == 0, nb, back)


def _rwkv(r_t, k_t, v_t, kk_t, kka_t, lwf_t, lwb_t):
    t, _, nch = r_t.shape
    n_blk = t // R_STEPS
    n_ctx_blk = N_CTX // R_STEPS
    blk = lambda d, n: (_rwkv_order(d, n, n_ctx_blk, n_blk), 0, 0)
    blk4 = lambda d, n: (d, _rwkv_order(d, n, n_ctx_blk, n_blk), 0, 0)
    only = lambda own: (lambda d, n: (jnp.where(d == own, _rwkv_order(d, n, n_ctx_blk, n_blk), 0), 0, 0))
    spec = pl.BlockSpec((R_STEPS, R_N, nch), blk)
    spec4 = pl.BlockSpec((1, R_STEPS, R_N, nch), blk4)
    return pl.pallas_call(
        _rwkv_kernel,
        grid=(2, n_blk),
        in_specs=[spec, spec, spec, spec, spec,
                  pl.BlockSpec((R_STEPS, R_N, nch), only(0)), pl.BlockSpec((R_STEPS, R_N, nch), only(1))],
        out_specs=spec4,
        out_shape=jax.ShapeDtypeStruct((2, t, R_N, nch), BF16),
        scratch_shapes=[pltpu.VMEM((R_N, R_N, nch), F32), pltpu.VMEM((4, R_N, nch), F32)],
        compiler_params=_cparams(("arbitrary", "arbitrary"), 48),
        name="rwkv",
    )(r_t, k_t, v_t, kk_t, kka_t, lwf_t, lwb_t)


def _log_sigmoid(x):
    return jnp.minimum(x, 0.0) - jnp.log(1.0 + jnp.exp(-jnp.abs(x)))


def _gla_dir(q, k, v, alo, wa, ba, st_ref, reverse):
    c = q.shape[0]
    q = q.astype(F32)
    k = k.astype(F32)
    g = _log_sigmoid(_dot_x3(alo, wa) + ba) * (1.0 / G_GATE_NORM)
    row = lax.broadcasted_iota(jnp.int32, (c, c), 0)
    col = lax.broadcasted_iota(jnp.int32, (c, c), 1)
    tri = (row <= col) if reverse else (row >= col)
    bcum = _dot_2x(tri.astype(BF16), g)
    blast = bcum[0:1] if reverse else bcum[c - 1:c]
    qd = (q * (G_DK ** -0.5) * jnp.exp(bcum)).astype(BF16)
    kd = (k * jnp.exp(-bcum)).astype(BF16)
    kl = (k * jnp.exp(blast - bcum)).astype(BF16)
    eb = jnp.exp(blast)
    vb = v.astype(BF16)
    outs = []
    for h in range(G_HEADS):
        sk = slice(h * G_DK, (h + 1) * G_DK)
        sv = slice(h * G_DV, (h + 1) * G_DV)
        att = jnp.where(tri, _dot_nt(qd[:, sk], kd[:, sk]), 0.0)
        st = st_ref[h]
        outs.append(_dot_nt(qd[:, sk], st.astype(BF16)) + _dot(att.astype(BF16), vb[:, sv]))
        st_ref[h] = st * eb[:, sk] + _dot_tn(vb[:, sv], kl[:, sk])
    return jnp.concatenate(outs, axis=-1)


GLA_BATCH = 4


def _gla_kernel(qf, kf, vf, lf, qb, kb, vb, lb, wa_ref, ba_ref, of_ref, ob_ref, sf_ref, sb_ref):
    @pl.when(pl.program_id(1) == 0)
    def _():
        sf_ref[...] = jnp.zeros_like(sf_ref)
        sb_ref[...] = jnp.zeros_like(sb_ref)

    for s in range(GLA_BATCH):
        of_ref[s] = _gla_dir(qf[s], kf[s], vf[s], lf[s], wa_ref.at[0], ba_ref[0], sf_ref.at[s], False).astype(BF16)
        ob_ref[s] = _gla_dir(qb[s], kb[s], vb[s], lb[s], wa_ref.at[1], ba_ref[1], sb_ref.at[s], True).astype(BF16)


def _gla(z, zl, w_a2, b_a):
    b, t, _ = z.shape
    n_blk = t // G_CHUNK
    n_ctx_blk = N_CTX // G_CHUNK
    fwd = lambda n: n
    bwd = lambda n: _rwkv_order(1, n, n_ctx_blk, n_blk)
    wk = G_HEADS * G_DK
    cq, ck, cv = C_GLA // wk, (C_GLA + wk) // wk, (C_GLA + 2 * wk) // D
    cl = 0

    nb = GLA_BATCH
    assert b % nb == 0

    def specs(order):
        return [pl.BlockSpec((nb, G_CHUNK, wk), lambda i, n: (i, order(n), cq)),
                pl.BlockSpec((nb, G_CHUNK, wk), lambda i, n: (i, order(n), ck)),
                pl.BlockSpec((nb, G_CHUNK, D), lambda i, n: (i, order(n), cv)),
                pl.BlockSpec((nb, G_CHUNK, 128), lambda i, n: (i, order(n), cl))]

    out_sd = jax.ShapeDtypeStruct((b, t, D), BF16)
    state = pltpu.VMEM((nb, G_HEADS, G_DV, G_DK), F32)
    return pl.pallas_call(
        _gla_kernel,
        grid=(b // nb, n_blk),
        in_specs=specs(fwd) + specs(bwd) + [
            pl.BlockSpec((2, 2, 128, wk), lambda i, n: (0, 0, 0, 0)),
            pl.BlockSpec((2, 1, wk), lambda i, n: (0, 0, 0))],
        out_specs=[pl.BlockSpec((nb, G_CHUNK, D), lambda i, n: (i, fwd(n), 0)),
                   pl.BlockSpec((nb, G_CHUNK, D), lambda i, n: (i, bwd(n), 0))],
        out_shape=[out_sd, out_sd],
        scratch_shapes=[state, state],
        compiler_params=_cparams(("parallel", "arbitrary"), 24),
        name="gla",
    )(z, z, z, zl, z, z, z, zl, w_a2, b_a)


def _mix_kernel(ogf_ref, ogb_ref, gout_ref, orf_ref, orb_ref, bonus_ref, gr_ref, gg_ref, grw_ref, x_ref,
                gnw_ref, lnw_ref, lnb_ref, e_ref, et_ref, wout_ref, m2_ref, n2w_ref, m3_ref, m4_ref,
                h1_o, hmb_o):
    og = ogf_ref[0].astype(F32) + ogb_ref[0].astype(F32)
    parts = []
    for h in range(G_HEADS):
        seg = og[:, h * G_DV:(h + 1) * G_DV]
        parts.append(seg * lax.rsqrt(jnp.mean(seg * seg, axis=-1, keepdims=True) + EPS))
    gout = gout_ref[0].astype(F32)
    y_gla = jnp.concatenate(parts, axis=-1) * gnw_ref[...] * (gout * _sigmoid(gout))

    e = e_ref[...]
    et = et_ref[...]
    o_r = orf_ref[0].astype(F32) + orb_ref[0].astype(F32)
    mu = _dot_x2(_dot_x2(o_r, e) * (1.0 / R_N), et)
    dlt = o_r - mu
    var = _dot_x2(dlt * dlt, e) * (1.0 / R_N)
    o_r = (dlt * _dot_x2(lax.rsqrt(var + R_LN_EPS), et) * lnw_ref[...] + lnb_ref[...]
           + bonus_ref[0].astype(F32))
    y_rwkv = o_r * gr_ref[0].astype(F32)

    y = _sigmoid(gg_ref[0].astype(F32)) * y_gla + _sigmoid(grw_ref[0].astype(F32)) * y_rwkv
    h1 = x_ref[0] + m2_ref[0] * _dot(y.astype(BF16), wout_ref[...])
    h1_o[0] = h1
    hn = h1 * lax.rsqrt(jnp.mean(h1 * h1, axis=-1, keepdims=True) + EPS) * n2w_ref[...]
    hmb_o[0] = (hn * (1.0 + m4_ref[0]) + m3_ref[0]).astype(BF16)


def _mix(og_f, og_b, z, or_f, or_b, bonus, gate_r, x, gnw, lnw, lnb, e, et, w_out, m2, n2w, m3, m4):
    b, tl, _ = x.shape
    off = N_CTX // ROWS
    seq = lambda c: pl.BlockSpec((1, ROWS, D), lambda i, j: (i, j + off, c))
    full = lambda *s: pl.BlockSpec(s, lambda i, j: (0,) * len(s))
    per_b = pl.BlockSpec((1, 1, D), lambda i, j: (i, 0, 0))
    lat = pl.BlockSpec((1, ROWS, D), lambda i, j: (i, j, 0))
    return pl.pallas_call(
        _mix_kernel,
        grid=(b, tl // ROWS),
        in_specs=[seq(0), seq(0), seq((C_GLA + 2048) // D), seq(0), seq(0), seq(0), seq(0),
                  seq(C_GATE // D), seq(C_GATE // D + 1), lat,
                  full(1, D), full(1, D), full(1, D), full(D, R_HEADS), full(R_HEADS, D), full(D, D),
                  per_b, full(1, D), per_b, per_b],
        out_specs=[lat, lat],
        out_shape=[jax.ShapeDtypeStruct((b, tl, D), F32), jax.ShapeDtypeStruct((b, tl, D), BF16)],
        compiler_params=_cparams(("parallel", "parallel"), 52),
        name="mix",
    )(og_f, og_b, z, or_f, or_b, bonus, gate_r, z, z, x, gnw, lnw, lnb, e, et, w_out, m2, n2w, m3, m4)


ROUTE_TOK = 1024


N_CAND = 80


def _dup_bf16(x):
    hi = lax.bitcast_convert_type(x.astype(BF16).astype(F32), jnp.uint32)
    return hi | (hi >> 16)


def _route_kernel(x_ref, wq_ref, keys_ref, n_o, e1_o, rho_o, e2_o,
                  q_scr, s_scr, t1_scr, t2_scr, cand_scr, rho_scr, cnt_scr):
    q_scr[...] = _dot(wq_ref[...], x_ref[...]).astype(BF16)
    ninf = -jnp.inf
    half = P_TOPK // 2
    tn = x_ref.shape[1]
    top_row = jnp.full((1, tn), jnp.inf, F32)

    def next_below(vals, bound):
        below = vals < bound
        return below, jnp.max(jnp.where(below, vals, ninf), axis=0, keepdims=True)

    def head(h, carry):
        q1 = q_scr[pl.ds(pl.multiple_of(h * 2 * P_KEYS, P_KEYS), P_KEYS), :]
        q2 = q_scr[pl.ds(pl.multiple_of(h * 2 * P_KEYS + P_KEYS, P_KEYS), P_KEYS), :]
        s_scr[0] = _dot(keys_ref[2 * h], q1)
        s_scr[1] = _dot(keys_ref[2 * h + 1], q2)
        rho_scr[...] = jnp.full(rho_scr.shape, -1.0, F32)

        def extract(i, bounds):
            m1, m2 = bounds
            _, m1 = next_below(s_scr[0], m1)
            below2, m2 = next_below(s_scr[1], m2)
            rho_scr[...] += jnp.where(below2, 1.0, 0.0)
            t1_scr[pl.ds(i, 1), :] = m1
            t2_scr[pl.ds(i, 1), :] = m2
            return m1, m2

        _, last2 = lax.fori_loop(0, P_TOPK, extract, (top_row, top_row))
        s1 = s_scr[0]
        s2 = s_scr[1]
        t1 = t1_scr[...]
        t2 = t2_scr[...]
        cand_scr[0:P_TOPK] = t1[0:1] + t2
        for i in range(1, half):
            cand_scr[P_TOPK + half * (i - 1):P_TOPK + half * i] = t1[i:i + 1] + t2[0:half]
        cand_scr[N_CAND - half:N_CAND] = t1[half:P_TOPK] + t2[0:1]
        cmax = t1[0:1] + t2[0:1]

        def pick(i, st):
            m, z = st
            _, m = next_below(cand_scr[...], m)
            return m, z + jnp.exp(m - cmax)

        tau, zsum = lax.fori_loop(0, P_TOPK, pick, (top_row, jnp.zeros_like(cmax)))
        cnt_scr[...] = jnp.zeros_like(cnt_scr)

        def count(j, c):
            cnt_scr[...] += jnp.where(s_scr[0] + t2_scr[pl.ds(j, 1), :] >= tau, 1.0, 0.0)
            return c

        lax.fori_loop(0, P_TOPK, count, 0)
        sel1 = s1 >= t1[P_TOPK - 1:P_TOPK]
        sel2 = s2 >= last2
        rho = rho_scr[...] + jnp.where(s2 < last2, 1.0, 0.0)
        n_o[h] = _dup_bf16(jnp.where(sel1, cnt_scr[...], 0.0))
        e1_o[h] = _dup_bf16(jnp.where(sel1, jnp.exp(s1 - t1[0:1]), 0.0))
        rho_o[h] = rho.astype(BF16)
        e2_o[h] = jnp.where(sel2, jnp.exp(s2 - t2[0:1]) / zsum, 0.0).astype(BF16)
        return carry

    lax.fori_loop(0, P_HEADS, head, 0)


def _route(hm_t, wq_t, keys):
    n = hm_t.shape[1]
    tn = ROUTE_TOK
    big = pl.BlockSpec((P_HEADS, P_KEYS, tn), lambda i: (0, 0, i))
    sd = lambda dt: jax.ShapeDtypeStruct((P_HEADS, P_KEYS, n), dt)
    return pl.pallas_call(
        _route_kernel,
        grid=(n // tn,),
        in_specs=[pl.BlockSpec((D, tn), lambda i: (0, i)),
                  pl.BlockSpec((2 * P_HEADS * P_KEYS, D), lambda i: (0, 0)),
                  pl.BlockSpec((2 * P_HEADS, P_KEYS, P_KEYS), lambda i: (0, 0, 0))],
        out_specs=[big, big, big, big],
        out_shape=[sd(jnp.uint32), sd(jnp.uint32), sd(BF16), sd(BF16)],
        scratch_shapes=[pltpu.VMEM((2 * P_HEADS * P_KEYS, tn), BF16), pltpu.VMEM((2, P_KEYS, tn), F32),
                        pltpu.VMEM((P_TOPK, tn), F32), pltpu.VMEM((P_TOPK, tn), F32),
                        pltpu.VMEM((N_CAND, tn), F32),
                        pltpu.VMEM((P_KEYS, tn), F32), pltpu.VMEM((P_KEYS, tn), F32)],
        compiler_params=_cparams(("parallel",), 48),
        name="route",
    )(hm_t, wq_t, keys)


PEER_TOK = 512
PEER_EXP = 2048
PEER_SUB = 512


def _gelu(x):
    return 0.5 * x * (1.0 + lax.erf(x * (2.0 ** -0.5)))


BF16_ROWS = 16


def _peer_kernel(x_ref, u_ref, vt_ref, n_ref, e1_ref, rho_ref, e2_ref, h1_ref, m5_ref, fw_ref, o_ref, acc_ref):
    j = pl.program_id(1)

    @pl.when(j == 0)
    def _():
        acc_ref[...] = jnp.zeros_like(acc_ref)

    tn = x_ref.shape[1]
    x = x_ref[...]
    groups = P_KEYS // BF16_ROWS
    a_per_half = PEER_SUB // P_KEYS
    p_halves = []
    for half in range(PEER_EXP // PEER_SUB):
        rows_h = PEER_SUB
        act = _dot(u_ref[half * rows_h:(half + 1) * rows_h, :], x)
        gates = []
        for al in range(half * a_per_half, (half + 1) * a_per_half):
            g = [None] * groups
            for h in range(P_HEADS):
                row = lambda ref: pltpu.bitcast(jnp.broadcast_to(ref[h, al:al + 1, :], (8, tn)), BF16)
                n_row = row(n_ref)
                e1_row = row(e1_ref)
                for q in range(groups):
                    rows = slice(q * BF16_ROWS, (q + 1) * BF16_ROWS)
                    term = jnp.where(rho_ref[h, rows, :] < n_row, e2_ref[h, rows, :], 0.0) * e1_row
                    g[q] = term if g[q] is None else g[q] + term
            gates.extend(g)
        p_halves.append(_gelu(act).astype(BF16) * jnp.concatenate(gates, axis=0))
    acc_ref[...] += _dot(vt_ref[...], jnp.concatenate(p_halves, axis=0))

    @pl.when(j == pl.num_programs(1) - 1)
    def _():
        h = h1_ref[...] + m5_ref[0] * acc_ref[...].T
        o_ref[...] = h * lax.rsqrt(jnp.mean(h * h, axis=-1, keepdims=True) + EPS) * fw_ref[...]


def _peer(hm_t, u_b, vt_b, n_cnt, e1, rho, e2, h1, m5, final_w, seq):
    n = hm_t.shape[1]
    n_exp = u_b.shape[0]
    tn, te = PEER_TOK, PEER_EXP
    big = pl.BlockSpec((P_HEADS, P_KEYS, tn), lambda i, j: (0, 0, i))
    a_rows = pl.BlockSpec((P_HEADS, te // P_KEYS, tn), lambda i, j: (0, j, i))
    tok = pl.BlockSpec((tn, D), lambda i, j: (i, 0))
    return pl.pallas_call(
        _peer_kernel,
        grid=(n // tn, n_exp // te),
        in_specs=[pl.BlockSpec((D, tn), lambda i, j: (0, i)),
                  pl.BlockSpec((te, D), lambda i, j: (j, 0)),
                  pl.BlockSpec((D, te), lambda i, j: (0, j)),
                  a_rows, a_rows, big, big,
                  tok, pl.BlockSpec((1, 1, D), lambda i, j: (i // (seq // tn), 0, 0)),
                  pl.BlockSpec((1, D), lambda i, j: (0, 0))],
        out_specs=tok,
        out_shape=jax.ShapeDtypeStruct((n, D), F32),
        scratch_shapes=[pltpu.VMEM((D, tn), F32)],
        compiler_params=_cparams(("parallel", "arbitrary"), 52),
        name="peer",
    )(hm_t, u_b, vt_b, n_cnt, e1, rho, e2, h1, m5, final_w)


def _reorder_cols(w):
    gla_main, gla_lora = w[..., 0:3072], w[..., 3072:3104]
    rw_main, rw_lora = w[..., 3104:6176], w[..., 6176:6528]
    gates = w[..., 6528:8576]
    pad = jnp.zeros(w.shape[:-1] + (512 - 32 - 352,), w.dtype)
    return jnp.concatenate([rw_main, gla_main, gates, gla_lora, rw_lora, pad], axis=-1)


def _hi_lo(w):
    hi = w.astype(BF16)
    return jnp.stack([hi, (w - hi.astype(F32)).astype(BF16)])


def _to_chains(a):
    b, t, _ = a.shape
    return a.reshape(b, t, R_HEADS, R_N).transpose(1, 3, 0, 2).reshape(t, R_N, b * R_HEADS)


def _from_chains(a, b):
    t = a.shape[0]
    return a.reshape(t, R_N, b, R_HEADS).transpose(2, 0, 3, 1).reshape(b, t, D)


def kernel(x, c, ctx, c_ctx, norm1_w, w_mod, b_mod, w_in, gla_w_a2, gla_b_a, gla_norm_w, rwkv_mu, rwkv_w0, rwkv_w2, rwkv_a0, rwkv_a2, rwkv_g2, rwkv_k_k, rwkv_k_a, rwkv_r_k, rwkv_ln_w, rwkv_ln_b, w_out, norm2_w, peer_w_q, peer_sub_keys, peer_u, peer_v, final_norm_w):
    b, seq, _ = x.shape
    assert w_in.shape[0] == 1 and ctx.shape[1] == N_CTX and seq % ROWS == 0
    row = lambda v: v.reshape(1, -1)

    c16 = jnp.zeros((16, D), F32).at[:b].set(c).at[b].set(c_ctx)
    m = _mod(c16, w_mod[0], row(b_mod[0])).reshape(16, N_MOD, D)
    m_lat, m_ctx = m[:b], m[b]
    per_b = lambda i: m_lat[:, i].reshape(b, 1, D)
    shift1 = jnp.stack([jnp.broadcast_to(m_ctx[0], (b, D)), m_lat[:, 0]], axis=1).reshape(b, 2, 1, D)
    scale1 = jnp.stack([jnp.broadcast_to(m_ctx[1], (b, D)), m_lat[:, 1]], axis=1).reshape(b, 2, 1, D)

    h_all = jnp.concatenate([ctx, x], axis=1)
    z, z_lora = _inproj(h_all, row(norm1_w[0]), scale1, shift1, _reorder_cols(w_in[0]).astype(BF16))

    mu = rwkv_mu[0]
    mu_lora = jnp.concatenate([jnp.zeros((32,), F32), mu[3072:], jnp.zeros((128,), F32)]).reshape(1, 512)
    e = jnp.repeat(jnp.eye(R_HEADS, dtype=BF16), R_N, axis=0)
    w2f = _hi_lo(jnp.zeros((128, D), F32).at[L_WF:L_WF + R_W_LORA].set(rwkv_w2[0, 0]))
    w2b = _hi_lo(jnp.zeros((256, D), F32).at[L_WB:L_WB + R_W_LORA].set(rwkv_w2[0, 1]))
    a2 = _hi_lo(jnp.zeros((128, D), F32).at[L_A - 128:L_A - 128 + R_A_LORA].set(rwkv_a2[0]))
    g2 = _hi_lo(jnp.zeros((256, D), F32).at[L_G - 128:L_G - 128 + R_G_LORA].set(rwkv_g2[0]))
    wa = jnp.zeros((2, 128, G_HEADS * G_DK), F32)
    wa = wa.at[0, 0:G_LORA].set(gla_w_a2[0, 0]).at[1, G_LORA:2 * G_LORA].set(gla_w_a2[0, 1])
    wa = jnp.stack([_hi_lo(wa[0]), _hi_lo(wa[1])])
    r, k2, vr, kk, kka, d_f, d_b, bonus, gate_r = _rprep(
        z, z_lora, row(mu[:3072]), mu_lora, rwkv_w0[0].reshape(2, 1, D), w2f, w2b, row(rwkv_a0[0]), a2,
        g2, row(rwkv_k_k[0]), row(rwkv_k_a[0]), row(rwkv_r_k[0]), e, e.T)

    o_r = _rwkv(_to_chains(r), _to_chains(k2), _to_chains(vr), _to_chains(kk), _to_chains(kka),
                _to_chains(d_f), _to_chains(d_b))
    or_f, or_b = _from_chains(o_r[0], b), _from_chains(o_r[1], b)

    og_f, og_b = _gla(z, z_lora, wa, gla_b_a[0].reshape(2, 1, -1))

    h1, hm_b = _mix(og_f, og_b, z, or_f, or_b, bonus, gate_r, x, row(gla_norm_w[0]), row(rwkv_ln_w[0]),
                        row(rwkv_ln_b[0]), e, e.T, w_out[0].astype(BF16), per_b(2), row(norm2_w[0]),
                        per_b(3), per_b(4))

    n_tok = b * seq
    hm_t = hm_b.reshape(n_tok, D).T
    n_cnt, e1, rho, e2 = _route(hm_t, peer_w_q[0].T.astype(BF16),
                                peer_sub_keys[0].reshape(2 * P_HEADS, P_KEYS, -1).astype(BF16))
    assert seq % PEER_TOK == 0
    out = _peer(hm_t, peer_u[0].astype(BF16), peer_v[0].T.astype(BF16), n_cnt, e1, rho, e2,
                h1.reshape(n_tok, D), per_b(5), row(final_norm_w), seq)
    return out.reshape(b, seq, D)
```

```python
import jax
import jax.numpy as jnp
from jax import lax
from jax.experimental import pallas as pl
from jax.experimental.pallas import tpu as pltpu

F32 = jnp.float32
BF16 = jnp.bfloat16

D = 1024
GRID_COLS = 64
N_CTX = 256
EPS = 1e-6
N_MOD = 6

G_HEADS = 4
G_DV = D // G_HEADS
G_DK = G_DV // 2
G_LORA = 16
G_GATE_NORM = 16.0
G_CHUNK = 64

R_N = 64
R_HEADS = D // R_N
R_LN_EPS = 64e-5
R_W_LORA = 64
R_A_LORA = 64
R_G_LORA = 160

P_HEADS = 8
P_KEYS = 128
P_TOPK = 16

C_RWKV = 0
C_GLA = 3072
C_GATE = 6144
C_LORA = 8192
LORA_W = 512
N_PROJ = C_LORA + LORA_W
L_WF, L_WB, L_A, L_G = 32, 96, 160, 224

ROWS = 256
V7X_VMEM_BYTES = 64 * 1024 * 1024


def _cparams(sem, vmem_mb):
    return pltpu.CompilerParams(dimension_semantics=sem,
                                vmem_limit_bytes=min(vmem_mb * 1024 * 1024, V7X_VMEM_BYTES - (8 << 20)))


def _sigmoid(x):
    return 1.0 / (1.0 + jnp.exp(-x))


def _softplus(x):
    return jnp.maximum(x, 0.0) + jnp.log(1.0 + jnp.exp(-jnp.abs(x)))


def _dot(a, b):
    return jnp.dot(a, b, preferred_element_type=F32)


def _dot_nt(a, b):
    return lax.dot_general(a, b, (((1,), (1,)), ((), ())), preferred_element_type=F32)


def _dot_tn(a, b):
    return lax.dot_general(a, b, (((0,), (0,)), ((), ())), preferred_element_type=F32)


def _split2(x):
    hi = x.astype(BF16)
    return hi, (x - hi.astype(F32)).astype(BF16)


def _dot_x2(x, w):
    hi, lo = _split2(x)
    return _dot(hi, w) + _dot(lo, w)


def _dot_2x(w, x):
    hi, lo = _split2(x)
    return _dot(w, hi) + _dot(w, lo)


def _dot_x3(x, w_ref):
    hi, lo = _split2(x)
    return _dot(hi, w_ref[0]) + (_dot(lo, w_ref[0]) + _dot(hi, w_ref[1]))


def _mod_kernel(c_ref, w_ref, b_ref, o_ref):
    c = c_ref[...]
    s = c * _sigmoid(c)
    o_ref[...] = _dot(s.astype(BF16), w_ref[...].astype(BF16)) + b_ref[...]


def _mod(c16, w_mod, b_mod):
    n = w_mod.shape[1]
    tn = 768
    return pl.pallas_call(
        _mod_kernel,
        grid=(n // tn,),
        in_specs=[pl.BlockSpec((16, D), lambda j: (0, 0)),
                  pl.BlockSpec((D, tn), lambda j: (0, j)),
                  pl.BlockSpec((1, tn), lambda j: (0, j))],
        out_specs=pl.BlockSpec((16, tn), lambda j: (0, j)),
        out_shape=jax.ShapeDtypeStruct((16, n), F32),
        compiler_params=_cparams(("parallel",), 24),
        name="mod",
    )(c16, w_mod, b_mod)


INPROJ_ROWS = 768


def _inproj_kernel(h_ref, nw_ref, sc_ref, sh_ref, w_ref, o_ref, ol_ref, a_scr):
    n = pl.program_id(2)

    @pl.when(n == 0)
    def _():
        x = h_ref[0]
        y = x * lax.rsqrt(jnp.mean(x * x, axis=-1, keepdims=True) + EPS) * nw_ref[...]
        t = pl.program_id(1) * INPROJ_ROWS + lax.broadcasted_iota(jnp.int32, (INPROJ_ROWS, 1), 0)
        is_ctx = t < N_CTX
        sc = jnp.where(is_ctx, sc_ref[0, 0], sc_ref[0, 1])
        sh = jnp.where(is_ctx, sh_ref[0, 0], sh_ref[0, 1])
        a_scr[...] = (y * (1.0 + sc) + sh).astype(BF16)

    acc = _dot(a_scr[...], w_ref[...])
    n_main = C_LORA // LORA_W

    @pl.when(n < n_main)
    def _():
        o_ref[0] = acc.astype(BF16)

    @pl.when(n == n_main)
    def _():
        ol_ref[0] = acc


def _inproj(h_all, norm_w, scale, shift, w_p):
    b, t, _ = h_all.shape
    tn = LORA_W
    n_main = C_LORA // tn
    mod = pl.BlockSpec((1, 2, 1, D), lambda i, j, n: (i, 0, 0, 0))
    return pl.pallas_call(
        _inproj_kernel,
        grid=(b, t // INPROJ_ROWS, N_PROJ // tn),
        in_specs=[pl.BlockSpec((1, INPROJ_ROWS, D), lambda i, j, n: (i, j, 0)),
                  pl.BlockSpec((1, D), lambda i, j, n: (0, 0)),
                  mod, mod,
                  pl.BlockSpec((D, tn), lambda i, j, n: (0, n))],
        out_specs=[pl.BlockSpec((1, INPROJ_ROWS, tn), lambda i, j, n: (i, j, jnp.minimum(n, n_main - 1))),
                   pl.BlockSpec((1, INPROJ_ROWS, tn), lambda i, j, n: (i, j, 0))],
        out_shape=[jax.ShapeDtypeStruct((b, t, C_LORA), BF16), jax.ShapeDtypeStruct((b, t, tn), F32)],
        scratch_shapes=[pltpu.VMEM((INPROJ_ROWS, D), BF16)],
        compiler_params=_cparams(("parallel", "parallel", "arbitrary"), 32),
        name="inproj",
    )(h_all, norm_w, scale, shift, w_p)


def _shift_mix(zc, zp, zn, mu, is_ctx):
    rows, w = zc.shape
    row = lax.broadcasted_iota(jnp.int32, (rows, w), 0)
    lane = lax.broadcasted_iota(jnp.int32, (rows, w), 1)
    pmask = jnp.where(is_ctx, 1, 3)
    rmask = jnp.where(is_ctx, rows - 1, GRID_COLS - 1)
    cls = lane & pmask
    rr = row & rmask
    prev = pltpu.roll(zc, 1, 0)
    nxt = pltpu.roll(zc, rows - 1, 0)
    up = jnp.concatenate([zp, zc[:rows - GRID_COLS]], axis=0)
    down = jnp.concatenate([zc[GRID_COLS:], zn], axis=0)
    zs = jnp.where(cls == 0, jnp.where(rr != 0, prev, 0.0),
                   jnp.where(cls == 1, jnp.where(rr != rmask, nxt, 0.0),
                             jnp.where(cls == 2, up, down)))
    return zc + (zs - zc) * mu


def _rprep_kernel(zc_ref, zp_ref, zn_ref, lc_ref, lp_ref, ln_ref,
                  mu_ref, mul_ref, w0_ref, w2f_ref, w2b_ref, a0_ref, a2_ref, g2_ref,
                  kk_w_ref, ka_w_ref, rk_w_ref, e_ref, et_ref,
                  r_o, k_o, v_o, kk_o, kka_o, df_o, db_o, bonus_o, gate_o):
    tb = pl.program_id(1)
    ntb = pl.num_programs(1)
    is_ctx = tb == 0
    has_up = (tb > 1).astype(F32)
    has_down = (tb < ntb - 1).astype(F32)

    zl = _shift_mix(lc_ref[0], lp_ref[0] * has_up, ln_ref[0] * has_down, mul_ref[...], is_ctx)
    def decay(window, w2p_ref, d):
        w = -_softplus(-(w0_ref[d] + _dot_x3(jnp.tanh(window), w2p_ref))) - 0.5
        return -jnp.exp(w)

    df_o[0] = decay(zl[:, 0:128], w2f_ref, 0).astype(BF16)
    db_o[0] = decay(zl[:, 0:256], w2b_ref, 1).astype(BF16)
    a = _sigmoid(a0_ref[...] + _dot_x3(zl[:, 128:256], a2_ref))
    gate_o[0] = _dot_x3(_sigmoid(zl[:, 128:384]), g2_ref).astype(BF16)

    def piece(i):
        sl = slice(i * D, (i + 1) * D)
        return _shift_mix(zc_ref[0, :, sl].astype(F32), zp_ref[0, :, sl].astype(F32) * has_up,
                          zn_ref[0, :, sl].astype(F32) * has_down, mu_ref[:, sl], is_ctx)

    r = piece(0)
    kr = piece(1)
    vr = piece(2)
    e = e_ref[...]
    et = et_ref[...]
    kkraw = kr * kk_w_ref[...]
    nrm = jnp.sqrt(_dot_x2(kkraw * kkraw, e))
    inv = 1.0 / jnp.maximum(nrm, 1e-12)
    kk = kkraw * _dot_x2(inv, et)
    k2 = kr * (1.0 + (a - 1.0) * ka_w_ref[...])
    rk = _dot_x2(r * k2 * rk_w_ref[...], e)
    r_o[0] = r.astype(BF16)
    k_o[0] = k2.astype(BF16)
    v_o[0] = vr.astype(BF16)
    kk_o[0] = kk.astype(BF16)
    kka_o[0] = (kk * a).astype(BF16)
    bonus_o[0] = (_dot_x2(rk, et) * vr).astype(BF16)


def _rprep(z, zl, mu_main, mu_lora, w0, w2f, w2b, a0, a2, g2, k_k, k_a, r_k, e, et):
    b, t, _ = z.shape
    ntb = t // ROWS
    q = ROWS // GRID_COLS
    last64 = t // GRID_COLS - 1
    wm = 3 * D
    cm = C_RWKV // wm
    cl = 0
    cur = lambda c: (lambda i, j: (i, j, c))
    prv = lambda c: (lambda i, j: (i, jnp.maximum(j * q - 1, 0), c))
    nxt = lambda c: (lambda i, j: (i, jnp.minimum(j * q + q, last64), c))
    full = lambda *s: pl.BlockSpec(s, lambda i, j: (0,) * len(s))
    out_spec = pl.BlockSpec((1, ROWS, D), lambda i, j: (i, j, 0))
    return pl.pallas_call(
        _rprep_kernel,
        grid=(b, ntb),
        in_specs=[pl.BlockSpec((1, ROWS, wm), cur(cm)),
                  pl.BlockSpec((1, GRID_COLS, wm), prv(cm)),
                  pl.BlockSpec((1, GRID_COLS, wm), nxt(cm)),
                  pl.BlockSpec((1, ROWS, 512), cur(cl)),
                  pl.BlockSpec((1, GRID_COLS, 512), prv(cl)),
                  pl.BlockSpec((1, GRID_COLS, 512), nxt(cl)),
                  full(1, wm), full(1, 512), full(2, 1, D), full(2, 128, D), full(2, 256, D),
                  full(1, D), full(2, 128, D), full(2, 256, D),
                  full(1, D), full(1, D), full(1, D), full(D, R_HEADS), full(R_HEADS, D)],
        out_specs=[out_spec] * 9,
        out_shape=[jax.ShapeDtypeStruct((b, t, D), BF16)] * 9,
        compiler_params=_cparams(("parallel", "parallel"), 52),
        name="rprep",
    )(z, z, z, zl, zl, zl, mu_main, mu_lora, w0, w2f, w2b, a0, a2, g2, k_k, k_a, r_k, e, et)


R_STEPS = 64


def _rwkv_kernel(r_ref, k_ref, v_ref, kk_ref, kka_ref, lwf_ref, lwb_ref, o_ref, s_ref, rows_ref):
    dirn = pl.program_id(0)
    tix = lambda i: jnp.where(dirn == 0, i, R_STEPS - 1 - i)

    @pl.when(pl.program_id(1) == 0)
    def _():
        s_ref[...] = jnp.zeros_like(s_ref)

    def weighted_sum(rows):
        acc = s_ref[0] * rows_ref[rows, 0:1, :]
        for k in range(1, R_N):
            acc = acc + s_ref[k] * rows_ref[rows, k:k + 1, :]
        return acc

    rows_ref[0] = kk_ref[tix(0)].astype(F32)
    sa0 = weighted_sum(0)

    def run(with_out):
        def step(i, carry):
            sa, lcum = carry
            t = tix(i)
            t_next = tix(jnp.minimum(i + 1, R_STEPS - 1))
            lcum = lcum + jnp.where(dirn == 0, lwf_ref[t], lwb_ref[t]).astype(F32)
            e_pos = jnp.exp(lcum)
            e_neg = jnp.exp(-lcum)
            rows_ref[0] = kk_ref[t_next].astype(F32) * e_pos
            rows_ref[1] = kka_ref[t].astype(F32) * e_neg
            rows_ref[2] = k_ref[t].astype(F32) * e_neg
            if with_out:
                rows_ref[3] = r_ref[t].astype(F32) * e_pos
            vv = v_ref[t].astype(F32)
            out = None
            sa_next = None
            for k in range(R_N):
                sk = s_ref[k] - sa * rows_ref[1, k:k + 1, :] + vv * rows_ref[2, k:k + 1, :]
                s_ref[k] = sk
                s_term = sk * rows_ref[0, k:k + 1, :]
                sa_next = s_term if sa_next is None else sa_next + s_term
                if with_out:
                    o_term = sk * rows_ref[3, k:k + 1, :]
                    out = o_term if out is None else out + o_term
            o_ref[0, t] = out.astype(BF16) if with_out else jnp.zeros(o_ref.shape[2:], BF16)
            return sa_next, lcum

        return lax.fori_loop(0, R_STEPS, step, (sa0, jnp.zeros((R_N, s_ref.shape[2]), F32)))[1]

    lcum = lax.cond(pl.program_id(1) < N_CTX // R_STEPS, lambda: run(False), lambda: run(True))
    rows_ref[0] = jnp.exp(lcum)
    for k in range(R_N):
        s_ref[k] = s_ref[k] * rows_ref[0, k:k + 1, :]


def _rwkv_order(dirn, nb, n_ctx_blk, n_blk):
    back = jnp.where(nb < n_ctx_blk, n_ctx_blk - 1 - nb, n_blk + n_ctx_blk - 1 - nb)
    return jnp.where(dirn == 0, nb, back)


def _rwkv(r_t, k_t, v_t, kk_t, kka_t, lwf_t, lwb_t):
    t, _, nch = r_t.shape
    n_blk = t // R_STEPS
    n_ctx_blk = N_CTX // R_STEPS
    blk = lambda d, n: (_rwkv_order(d, n, n_ctx_blk, n_blk), 0, 0)
    blk4 = lambda d, n: (d, _rwkv_order(d, n, n_ctx_blk, n_blk), 0, 0)
    only = lambda own: (lambda d, n: (jnp.where(d == own, _rwkv_order(d, n, n_ctx_blk, n_blk), 0), 0, 0))
    spec = pl.BlockSpec((R_STEPS, R_N, nch), blk)
    spec4 = pl.BlockSpec((1, R_STEPS, R_N, nch), blk4)
    return pl.pallas_call(
        _rwkv_kernel,
        grid=(2, n_blk),
        in_specs=[spec, spec, spec, spec, spec,
                  pl.BlockSpec((R_STEPS, R_N, nch), only(0)), pl.BlockSpec((R_STEPS, R_N, nch), only(1))],
        out_specs=spec4,
        out_shape=jax.ShapeDtypeStruct((2, t, R_N, nch), BF16),
        scratch_shapes=[pltpu.VMEM((R_N, R_N, nch), F32), pltpu.VMEM((4, R_N, nch), F32)],
        compiler_params=_cparams(("arbitrary", "arbitrary"), 48),
        name="rwkv",
    )(r_t, k_t, v_t, kk_t, kka_t, lwf_t, lwb_t)


def _log_sigmoid(x):
    return jnp.minimum(x, 0.0) - jnp.log(1.0 + jnp.exp(-jnp.abs(x)))


def _gla_dir(q, k, v, alo, wa, ba, st_ref, reverse, need_out):
    c = q.shape[0]
    q = q.astype(F32)
    k = k.astype(F32)
    g = _log_sigmoid(_dot_x3(alo, wa) + ba) * (1.0 / G_GATE_NORM)
    row = lax.broadcasted_iota(jnp.int32, (c, c), 0)
    col = lax.broadcasted_iota(jnp.int32, (c, c), 1)
    tri = (row <= col) if reverse else (row >= col)
    bcum = _dot_2x(tri.astype(BF16), g)
    blast = bcum[0:1] if reverse else bcum[c - 1:c]
    kl = (k * jnp.exp(blast - bcum)).astype(BF16)
    eb = jnp.exp(blast)
    vb = v.astype(BF16)
    if need_out:
        qd = (q * (G_DK ** -0.5) * jnp.exp(bcum)).astype(BF16)
        kd = (k * jnp.exp(-bcum)).astype(BF16)
    outs = []
    for h in range(G_HEADS):
        sk = slice(h * G_DK, (h + 1) * G_DK)
        sv = slice(h * G_DV, (h + 1) * G_DV)
        st = st_ref[h]
        if need_out:
            att = jnp.where(tri, _dot_nt(qd[:, sk], kd[:, sk]), 0.0)
            outs.append(_dot_nt(qd[:, sk], st.astype(BF16)) + _dot(att.astype(BF16), vb[:, sv]))
        st_ref[h] = st * eb[:, sk] + _dot_tn(vb[:, sv], kl[:, sk])
    return jnp.concatenate(outs, axis=-1).astype(BF16) if need_out else jnp.zeros(v.shape, BF16)


GLA_BATCH = 4


def _gla_kernel(qf, kf, vf, lf, qb, kb, vb, lb, wa_ref, ba_ref, of_ref, ob_ref, sf_ref, sb_ref):
    @pl.when(pl.program_id(1) == 0)
    def _():
        sf_ref[...] = jnp.zeros_like(sf_ref)
        sb_ref[...] = jnp.zeros_like(sb_ref)

    def run(need_out):
        for s in range(GLA_BATCH):
            of_ref[s] = _gla_dir(qf[s], kf[s], vf[s], lf[s], wa_ref.at[0], ba_ref[0], sf_ref.at[s], False, need_out)
            ob_ref[s] = _gla_dir(qb[s], kb[s], vb[s], lb[s], wa_ref.at[1], ba_ref[1], sb_ref.at[s], True, need_out)

    is_ctx = pl.program_id(1) < N_CTX // G_CHUNK
    pl.when(is_ctx)(lambda: run(False))
    pl.when(jnp.logical_not(is_ctx))(lambda: run(True))


def _gla(z, zl, w_a2, b_a):
    b, t, _ = z.shape
    n_blk = t // G_CHUNK
    n_ctx_blk = N_CTX // G_CHUNK
    fwd = lambda n: n
    bwd = lambda n: _rwkv_order(1, n, n_ctx_blk, n_blk)
    wk = G_HEADS * G_DK
    cq, ck, cv = C_GLA // wk, (C_GLA + wk) // wk, (C_GLA + 2 * wk) // D
    cl = 0

    nb = GLA_BATCH
    assert b % nb == 0

    def specs(order):
        return [pl.BlockSpec((nb, G_CHUNK, wk), lambda i, n: (i, order(n), cq)),
                pl.BlockSpec((nb, G_CHUNK, wk), lambda i, n: (i, order(n), ck)),
                pl.BlockSpec((nb, G_CHUNK, D), lambda i, n: (i, order(n), cv)),
                pl.BlockSpec((nb, G_CHUNK, 128), lambda i, n: (i, order(n), cl))]

    out_sd = jax.ShapeDtypeStruct((b, t, D), BF16)
    state = pltpu.VMEM((nb, G_HEADS, G_DV, G_DK), F32)
    return pl.pallas_call(
        _gla_kernel,
        grid=(b // nb, n_blk),
        in_specs=specs(fwd) + specs(bwd) + [
            pl.BlockSpec((2, 2, 128, wk), lambda i, n: (0, 0, 0, 0)),
            pl.BlockSpec((2, 1, wk), lambda i, n: (0, 0, 0))],
        out_specs=[pl.BlockSpec((nb, G_CHUNK, D), lambda i, n: (i, fwd(n), 0)),
                   pl.BlockSpec((nb, G_CHUNK, D), lambda i, n: (i, bwd(n), 0))],
        out_shape=[out_sd, out_sd],
        scratch_shapes=[state, state],
        compiler_params=_cparams(("parallel", "arbitrary"), 24),
        name="gla",
    )(z, z, z, zl, z, z, z, zl, w_a2, b_a)


def _mix_kernel(ogf_ref, ogb_ref, gout_ref, orf_ref, orb_ref, bonus_ref, gr_ref, gg_ref, grw_ref, x_ref,
                gnw_ref, lnw_ref, lnb_ref, e_ref, et_ref, wout_ref, m2_ref, n2w_ref, m3_ref, m4_ref,
                h1_o, hmb_o):
    og = ogf_ref[0].astype(F32) + ogb_ref[0].astype(F32)
    parts = []
    for h in range(G_HEADS):
        seg = og[:, h * G_DV:(h + 1) * G_DV]
        parts.append(seg * lax.rsqrt(jnp.mean(seg * seg, axis=-1, keepdims=True) + EPS))
    gout = gout_ref[0].astype(F32)
    y_gla = jnp.concatenate(parts, axis=-1) * gnw_ref[...] * (gout * _sigmoid(gout))

    e = e_ref[...]
    et = et_ref[...]
    o_r = orf_ref[0].astype(F32) + orb_ref[0].astype(F32)
    mu = _dot_x2(_dot_x2(o_r, e) * (1.0 / R_N), et)
    dlt = o_r - mu
    var = _dot_x2(dlt * dlt, e) * (1.0 / R_N)
    o_r = (dlt * _dot_x2(lax.rsqrt(var + R_LN_EPS), et) * lnw_ref[...] + lnb_ref[...]
           + bonus_ref[0].astype(F32))
    y_rwkv = o_r * gr_ref[0].astype(F32)

    y = _sigmoid(gg_ref[0].astype(F32)) * y_gla + _sigmoid(grw_ref[0].astype(F32)) * y_rwkv
    h1 = x_ref[0] + m2_ref[0] * _dot(y.astype(BF16), wout_ref[...])
    h1_o[0] = h1
    hn = h1 * lax.rsqrt(jnp.mean(h1 * h1, axis=-1, keepdims=True) + EPS) * n2w_ref[...]
    hmb_o[0] = (hn * (1.0 + m4_ref[0]) + m3_ref[0]).astype(BF16)


def _mix(og_f, og_b, z, or_f, or_b, bonus, gate_r, x, gnw, lnw, lnb, e, et, w_out, m2, n2w, m3, m4):
    b, tl, _ = x.shape
    off = N_CTX // ROWS
    seq = lambda c: pl.BlockSpec((1, ROWS, D), lambda i, j: (i, j + off, c))
    full = lambda *s: pl.BlockSpec(s, lambda i, j: (0,) * len(s))
    per_b = pl.BlockSpec((1, 1, D), lambda i, j: (i, 0, 0))
    lat = pl.BlockSpec((1, ROWS, D), lambda i, j: (i, j, 0))
    return pl.pallas_call(
        _mix_kernel,
        grid=(b, tl // ROWS),
        in_specs=[seq(0), seq(0), seq((C_GLA + 2048) // D), seq(0), seq(0), seq(0), seq(0),
                  seq(C_GATE // D), seq(C_GATE // D + 1), lat,
                  full(1, D), full(1, D), full(1, D), full(D, R_HEADS), full(R_HEADS, D), full(D, D),
                  per_b, full(1, D), per_b, per_b],
        out_specs=[lat, lat],
        out_shape=[jax.ShapeDtypeStruct((b, tl, D), F32), jax.ShapeDtypeStruct((b, tl, D), BF16)],
        compiler_params=_cparams(("parallel", "parallel"), 52),
        name="mix",
    )(og_f, og_b, z, or_f, or_b, bonus, gate_r, z, z, x, gnw, lnw, lnb, e, et, w_out, m2, n2w, m3, m4)


ROUTE_TOK = 1024


N_CAND = 80


def _dup_bf16(x):
    hi = lax.bitcast_convert_type(x.astype(BF16).astype(F32), jnp.uint32)
    return hi | (hi >> 16)


def _route_kernel(x_ref, wq_ref, keys_ref, n_o, e1_o, rho_o, e2_o,
                  q_scr, s_scr, t1_scr, t2_scr, cand_scr, rho_scr, cnt_scr):
    q_scr[...] = _dot(wq_ref[...], x_ref[...]).astype(BF16)
    ninf = -jnp.inf
    half = P_TOPK // 2
    tn = x_ref.shape[1]
    top_row = jnp.full((1, tn), jnp.inf, F32)

    def next_below(vals, bound):
        below = vals < bound
        return below, jnp.max(jnp.where(below, vals, ninf), axis=0, keepdims=True)

    def head(h, carry):
        q1 = q_scr[pl.ds(pl.multiple_of(h * 2 * P_KEYS, P_KEYS), P_KEYS), :]
        q2 = q_scr[pl.ds(pl.multiple_of(h * 2 * P_KEYS + P_KEYS, P_KEYS), P_KEYS), :]
        s_scr[0] = _dot(keys_ref[2 * h], q1)
        s_scr[1] = _dot(keys_ref[2 * h + 1], q2)
        rho_scr[...] = jnp.full(rho_scr.shape, -1.0, F32)

        def extract(i, bounds):
            m1, m2 = bounds
            _, m1 = next_below(s_scr[0], m1)
            below2, m2 = next_below(s_scr[1], m2)
            rho_scr[...] += jnp.where(below2, 1.0, 0.0)
            t1_scr[pl.ds(i, 1), :] = m1
            t2_scr[pl.ds(i, 1), :] = m2
            return m1, m2

        _, last2 = lax.fori_loop(0, P_TOPK, extract, (top_row, top_row))
        s1 = s_scr[0]
        s2 = s_scr[1]
        t1 = t1_scr[...]
        t2 = t2_scr[...]
        cand_scr[0:P_TOPK] = t1[0:1] + t2
        for i in range(1, half):
            cand_scr[P_TOPK + half * (i - 1):P_TOPK + half * i] = t1[i:i + 1] + t2[0:half]
        cand_scr[N_CAND - half:N_CAND] = t1[half:P_TOPK] + t2[0:1]
        cmax = t1[0:1] + t2[0:1]

        def pick(i, st):
            m, z = st
            _, m = next_below(cand_scr[...], m)
            return m, z + jnp.exp(m - cmax)

        tau, zsum = lax.fori_loop(0, P_TOPK, pick, (top_row, jnp.zeros_like(cmax)))
        cnt_scr[...] = jnp.zeros_like(cnt_scr)

        def count(j, c):
            cnt_scr[...] += jnp.where(s_scr[0] + t2_scr[pl.ds(j, 1), :] >= tau, 1.0, 0.0)
            return c

        lax.fori_loop(0, P_TOPK, count, 0)
        sel1 = s1 >= t1[P_TOPK - 1:P_TOPK]
        sel2 = s2 >= last2
        rho = rho_scr[...] + jnp.where(s2 < last2, 1.0, 0.0)
        n_o[h] = _dup_bf16(jnp.where(sel1, cnt_scr[...], 0.0))
        e1_o[h] = _dup_bf16(jnp.where(sel1, jnp.exp(s1 - t1[0:1]), 0.0))
        rho_o[h] = rho.astype(BF16)
        e2_o[h] = jnp.where(sel2, jnp.exp(s2 - t2[0:1]) / zsum, 0.0).astype(BF16)
        return carry

    lax.fori_loop(0, P_HEADS, head, 0)


def _route(hm_t, wq_t, keys):
    n = hm_t.shape[1]
    tn = ROUTE_TOK
    big = pl.BlockSpec((P_HEADS, P_KEYS, tn), lambda i: (0, 0, i))
    sd = lambda dt: jax.ShapeDtypeStruct((P_HEADS, P_KEYS, n), dt)
    return pl.pallas_call(
        _route_kernel,
        grid=(n // tn,),
        in_specs=[pl.BlockSpec((D, tn), lambda i: (0, i)),
                  pl.BlockSpec((2 * P_HEADS * P_KEYS, D), lambda i: (0, 0)),
                  pl.BlockSpec((2 * P_HEADS, P_KEYS, P_KEYS), lambda i: (0, 0, 0))],
        out_specs=[big, big, big, big],
        out_shape=[sd(jnp.uint32), sd(jnp.uint32), sd(BF16), sd(BF16)],
        scratch_shapes=[pltpu.VMEM((2 * P_HEADS * P_KEYS, tn), BF16), pltpu.VMEM((2, P_KEYS, tn), F32),
                        pltpu.VMEM((P_TOPK, tn), F32), pltpu.VMEM((P_TOPK, tn), F32),
                        pltpu.VMEM((N_CAND, tn), F32),
                        pltpu.VMEM((P_KEYS, tn), F32), pltpu.VMEM((P_KEYS, tn), F32)],
        compiler_params=_cparams(("parallel",), 48),
        name="route",
    )(hm_t, wq_t, keys)


PEER_TOK = 512
PEER_EXP = 2048
PEER_SUB = 512


def _gelu(x):
    return 0.5 * x * (1.0 + lax.erf(x * (2.0 ** -0.5)))


BF16_ROWS = 16


def _peer_kernel(x_ref, u_ref, vt_ref, n_ref, e1_ref, rho_ref, e2_ref, h1_ref, m5_ref, fw_ref, o_ref, acc_ref):
    j = pl.program_id(1)

    @pl.when(j == 0)
    def _():
        acc_ref[...] = jnp.zeros_like(acc_ref)

    tn = x_ref.shape[1]
    x = x_ref[...]
    groups = P_KEYS // BF16_ROWS
    a_per_half = PEER_SUB // P_KEYS
    p_halves = []
    for half in range(PEER_EXP // PEER_SUB):
        rows_h = PEER_SUB
        act = _dot(u_ref[half * rows_h:(half + 1) * rows_h, :], x)
        gates = []
        for al in range(half * a_per_half, (half + 1) * a_per_half):
            g = [None] * groups
            for h in range(P_HEADS):
                row = lambda ref: pltpu.bitcast(jnp.broadcast_to(ref[h, al:al + 1, :], (8, tn)), BF16)
                n_row = row(n_ref)
                e1_row = row(e1_ref)
                for q in range(groups):
                    rows = slice(q * BF16_ROWS, (q + 1) * BF16_ROWS)
                    term = jnp.where(rho_ref[h, rows, :] < n_row, e2_ref[h, rows, :], 0.0) * e1_row
                    g[q] = term if g[q] is None else g[q] + term
            gates.extend(g)
        p_halves.append(_gelu(act).astype(BF16) * jnp.concatenate(gates, axis=0))
    acc_ref[...] += _dot(vt_ref[...], jnp.concatenate(p_halves, axis=0))

    @pl.when(j == pl.num_programs(1) - 1)
    def _():
        h = h1_ref[...] + m5_ref[0] * acc_ref[...].T
        o_ref[...] = h * lax.rsqrt(jnp.mean(h * h, axis=-1, keepdims=True) + EPS) * fw_ref[...]


def _peer(hm_t, u_b, vt_b, n_cnt, e1, rho, e2, h1, m5, final_w, seq):
    n = hm_t.shape[1]
    n_exp = u_b.shape[0]
    tn, te = PEER_TOK, PEER_EXP
    big = pl.BlockSpec((P_HEADS, P_KEYS, tn), lambda i, j: (0, 0, i))
    a_rows = pl.BlockSpec((P_HEADS, te // P_KEYS, tn), lambda i, j: (0, j, i))
    tok = pl.BlockSpec((tn, D), lambda i, j: (i, 0))
    return pl.pallas_call(
        _peer_kernel,
        grid=(n // tn, n_exp // te),
        in_specs=[pl.BlockSpec((D, tn), lambda i, j: (0, i)),
                  pl.BlockSpec((te, D), lambda i, j: (j, 0)),
                  pl.BlockSpec((D, te), lambda i, j: (0, j)),
                  a_rows, a_rows, big, big,
                  tok, pl.BlockSpec((1, 1, D), lambda i, j: (i // (seq // tn), 0, 0)),
                  pl.BlockSpec((1, D), lambda i, j: (0, 0))],
        out_specs=tok,
        out_shape=jax.ShapeDtypeStruct((n, D), F32),
        scratch_shapes=[pltpu.VMEM((D, tn), F32)],
        compiler_params=_cparams(("parallel", "arbitrary"), 52),
        name="peer",
    )(hm_t, u_b, vt_b, n_cnt, e1, rho, e2, h1, m5, final_w)


def _reorder_cols(w):
    gla_main, gla_lora = w[..., 0:3072], w[..., 3072:3104]
    rw_main, rw_lora = w[..., 3104:6176], w[..., 6176:6528]
    gates = w[..., 6528:8576]
    pad = jnp.zeros(w.shape[:-1] + (512 - 32 - 352,), w.dtype)
    return jnp.concatenate([rw_main, gla_main, gates, gla_lora, rw_lora, pad], axis=-1)


def _hi_lo(w):
    hi = w.astype(BF16)
    return jnp.stack([hi, (w - hi.astype(F32)).astype(BF16)])


def _to_chains(a):
    b, t, _ = a.shape
    return a.reshape(b, t, R_HEADS, R_N).transpose(1, 3, 0, 2).reshape(t, R_N, b * R_HEADS)


def _from_chains(a, b):
    t = a.shape[0]
    return a.reshape(t, R_N, b, R_HEADS).transpose(2, 0, 3, 1).reshape(b, t, D)


def kernel(x, c, ctx, c_ctx, norm1_w, w_mod, b_mod, w_in, gla_w_a2, gla_b_a, gla_norm_w, rwkv_mu, rwkv_w0, rwkv_w2, rwkv_a0, rwkv_a2, rwkv_g2, rwkv_k_k, rwkv_k_a, rwkv_r_k, rwkv_ln_w, rwkv_ln_b, w_out, norm2_w, peer_w_q, peer_sub_keys, peer_u, peer_v, final_norm_w):
    b, seq, _ = x.shape
    assert w_in.shape[0] == 1 and ctx.shape[1] == N_CTX and seq % ROWS == 0
    row = lambda v: v.reshape(1, -1)

    c16 = jnp.zeros((16, D), F32).at[:b].set(c).at[b].set(c_ctx)
    m = _mod(c16, w_mod[0], row(b_mod[0])).reshape(16, N_MOD, D)
    m_lat, m_ctx = m[:b], m[b]
    per_b = lambda i: m_lat[:, i].reshape(b, 1, D)
    shift1 = jnp.stack([jnp.broadcast_to(m_ctx[0], (b, D)), m_lat[:, 0]], axis=1).reshape(b, 2, 1, D)
    scale1 = jnp.stack([jnp.broadcast_to(m_ctx[1], (b, D)), m_lat[:, 1]], axis=1).reshape(b, 2, 1, D)

    h_all = jnp.concatenate([ctx, x], axis=1)
    z, z_lora = _inproj(h_all, row(norm1_w[0]), scale1, shift1, _reorder_cols(w_in[0]).astype(BF16))

    mu = rwkv_mu[0]
    mu_lora = jnp.concatenate([jnp.zeros((32,), F32), mu[3072:], jnp.zeros((128,), F32)]).reshape(1, 512)
    e = jnp.repeat(jnp.eye(R_HEADS, dtype=BF16), R_N, axis=0)
    w2f = _hi_lo(jnp.zeros((128, D), F32).at[L_WF:L_WF + R_W_LORA].set(rwkv_w2[0, 0]))
    w2b = _hi_lo(jnp.zeros((256, D), F32).at[L_WB:L_WB + R_W_LORA].set(rwkv_w2[0, 1]))
    a2 = _hi_lo(jnp.zeros((128, D), F32).at[L_A - 128:L_A - 128 + R_A_LORA].set(rwkv_a2[0]))
    g2 = _hi_lo(jnp.zeros((256, D), F32).at[L_G - 128:L_G - 128 + R_G_LORA].set(rwkv_g2[0]))
    wa = jnp.zeros((2, 128, G_HEADS * G_DK), F32)
    wa = wa.at[0, 0:G_LORA].set(gla_w_a2[0, 0]).at[1, G_LORA:2 * G_LORA].set(gla_w_a2[0, 1])
    wa = jnp.stack([_hi_lo(wa[0]), _hi_lo(wa[1])])
    r, k2, vr, kk, kka, d_f, d_b, bonus, gate_r = _rprep(
        z, z_lora, row(mu[:3072]), mu_lora, rwkv_w0[0].reshape(2, 1, D), w2f, w2b, row(rwkv_a0[0]), a2,
        g2, row(rwkv_k_k[0]), row(rwkv_k_a[0]), row(rwkv_r_k[0]), e, e.T)

    o_r = _rwkv(_to_chains(r), _to_chains(k2), _to_chains(vr), _to_chains(kk), _to_chains(kka),
                _to_chains(d_f), _to_chains(d_b))
    or_f, or_b = _from_chains(o_r[0], b), _from_chains(o_r[1], b)

    og_f, og_b = _gla(z, z_lora, wa, gla_b_a[0].reshape(2, 1, -1))

    h1, hm_b = _mix(og_f, og_b, z, or_f, or_b, bonus, gate_r, x, row(gla_norm_w[0]), row(rwkv_ln_w[0]),
                        row(rwkv_ln_b[0]), e, e.T, w_out[0].astype(BF16), per_b(2), row(norm2_w[0]),
                        per_b(3), per_b(4))

    n_tok = b * seq
    hm_t = hm_b.reshape(n_tok, D).T
    n_cnt, e1, rho, e2 = _route(hm_t, peer_w_q[0].T.astype(BF16),
                                peer_sub_keys[0].reshape(2 * P_HEADS, P_KEYS, -1).astype(BF16))
    assert seq % PEER_TOK == 0
    out = _peer(hm_t, peer_u[0].astype(BF16), peer_v[0].T.astype(BF16), n_cnt, e1, rho, e2,
                h1.reshape(n_tok, D), per_b(5), row(final_norm_w), seq)
    return out.reshape(b, seq, D)
```

```python
import jax
import jax.numpy as jnp
from jax import lax
from jax.experimental import pallas as pl
from jax.experimental.pallas import tpu as pltpu

F32 = jnp.float32
BF16 = jnp.bfloat16

D = 1024
GRID_COLS = 64
N_CTX = 256
EPS = 1e-6
N_MOD = 6

G_HEADS = 4
G_DV = D // G_HEADS
G_DK = G_DV // 2
G_LORA = 16
G_GATE_NORM = 16.0
G_CHUNK = 64

R_N = 64
R_HEADS = D // R_N
R_LN_EPS = 64e-5
R_W_LORA = 64
R_A_LORA = 64
R_G_LORA = 160

P_HEADS = 8
P_KEYS = 128
P_TOPK = 16

C_RWKV = 0
C_GLA = 3072
C_GATE = 6144
C_LORA = 8192
LORA_W = 512
N_PROJ = C_LORA + LORA_W
L_WF, L_WB, L_A, L_G = 32, 96, 160, 224

ROWS = 256
V7X_VMEM_BYTES = 64 * 1024 * 1024


def _cparams(sem, vmem_mb):
    return pltpu.CompilerParams(dimension_semantics=sem,
                                vmem_limit_bytes=min(vmem_mb * 1024 * 1024, V7X_VMEM_BYTES - (8 << 20)))


def _sigmoid(x):
    return 1.0 / (1.0 + jnp.exp(-x))


def _softplus(x):
    return jnp.maximum(x, 0.0) + jnp.log(1.0 + jnp.exp(-jnp.abs(x)))


def _dot(a, b):
    return jnp.dot(a, b, preferred_element_type=F32)


def _dot_nt(a, b):
    return lax.dot_general(a, b, (((1,), (1,)), ((), ())), preferred_element_type=F32)


def _dot_tn(a, b):
    return lax.dot_general(a, b, (((0,), (0,)), ((), ())), preferred_element_type=F32)


def _split2(x):
    hi = x.astype(BF16)
    return hi, (x - hi.astype(F32)).astype(BF16)


def _dot_x2(x, w):
    hi, lo = _split2(x)
    return _dot(hi, w) + _dot(lo, w)


def _dot_2x(w, x):
    hi, lo = _split2(x)
    return _dot(w, hi) + _dot(w, lo)


def _dot_x3(x, w_ref):
    hi, lo = _split2(x)
    return _dot(hi, w_ref[0]) + (_dot(lo, w_ref[0]) + _dot(hi, w_ref[1]))


def _mod_kernel(c_ref, w_ref, b_ref, o_ref):
    c = c_ref[...]
    s = c * _sigmoid(c)
    o_ref[...] = _dot(s.astype(BF16), w_ref[...].astype(BF16)) + b_ref[...]


def _mod(c16, w_mod, b_mod):
    n = w_mod.shape[1]
    tn = 768
    return pl.pallas_call(
        _mod_kernel,
        grid=(n // tn,),
        in_specs=[pl.BlockSpec((16, D), lambda j: (0, 0)),
                  pl.BlockSpec((D, tn), lambda j: (0, j)),
                  pl.BlockSpec((1, tn), lambda j: (0, j))],
        out_specs=pl.BlockSpec((16, tn), lambda j: (0, j)),
        out_shape=jax.ShapeDtypeStruct((16, n), F32),
        compiler_params=_cparams(("parallel",), 24),
        name="mod",
    )(c16, w_mod, b_mod)


INPROJ_ROWS = 2304


def _inproj_kernel(h_ref, nw_ref, sc_ref, sh_ref, w_ref, o_ref, ol_ref, a_scr):
    n = pl.program_id(2)

    @pl.when(n == 0)
    def _():
        x = h_ref[0]
        y = x * lax.rsqrt(jnp.mean(x * x, axis=-1, keepdims=True) + EPS) * nw_ref[...]
        t = pl.program_id(1) * INPROJ_ROWS + lax.broadcasted_iota(jnp.int32, (INPROJ_ROWS, 1), 0)
        is_ctx = t < N_CTX
        sc = jnp.where(is_ctx, sc_ref[0, 0], sc_ref[0, 1])
        sh = jnp.where(is_ctx, sh_ref[0, 0], sh_ref[0, 1])
        a_scr[...] = (y * (1.0 + sc) + sh).astype(BF16)

    acc = _dot(a_scr[...], w_ref[...])
    n_main = C_LORA // LORA_W

    @pl.when(n < n_main)
    def _():
        o_ref[0] = acc.astype(BF16)

    @pl.when(n == n_main)
    def _():
        ol_ref[0] = acc


def _inproj(h_all, norm_w, scale, shift, w_p):
    b, t, _ = h_all.shape
    tn = LORA_W
    n_main = C_LORA // tn
    mod = pl.BlockSpec((1, 2, 1, D), lambda i, j, n: (i, 0, 0, 0))
    return pl.pallas_call(
        _inproj_kernel,
        grid=(b, t // INPROJ_ROWS, N_PROJ // tn),
        in_specs=[pl.BlockSpec((1, INPROJ_ROWS, D), lambda i, j, n: (i, j, 0)),
                  pl.BlockSpec((1, D), lambda i, j, n: (0, 0)),
                  mod, mod,
                  pl.BlockSpec((D, tn), lambda i, j, n: (0, n))],
        out_specs=[pl.BlockSpec((1, INPROJ_ROWS, tn), lambda i, j, n: (i, j, jnp.minimum(n, n_main - 1))),
                   pl.BlockSpec((1, INPROJ_ROWS, tn), lambda i, j, n: (i, j, 0))],
        out_shape=[jax.ShapeDtypeStruct((b, t, C_LORA), BF16), jax.ShapeDtypeStruct((b, t, tn), F32)],
        scratch_shapes=[pltpu.VMEM((INPROJ_ROWS, D), BF16)],
        compiler_params=_cparams(("parallel", "parallel", "arbitrary"), 52),
        name="inproj",
    )(h_all, norm_w, scale, shift, w_p)


def _shift_mix(zc, zp, zn, mu, is_ctx):
    rows, w = zc.shape
    row = lax.broadcasted_iota(jnp.int32, (rows, w), 0)
    lane = lax.broadcasted_iota(jnp.int32, (rows, w), 1)
    pmask = jnp.where(is_ctx, 1, 3)
    rmask = jnp.where(is_ctx, rows - 1, GRID_COLS - 1)
    cls = lane & pmask
    rr = row & rmask
    prev = pltpu.roll(zc, 1, 0)
    nxt = pltpu.roll(zc, rows - 1, 0)
    up = jnp.concatenate([zp, zc[:rows - GRID_COLS]], axis=0)
    down = jnp.concatenate([zc[GRID_COLS:], zn], axis=0)
    zs = jnp.where(cls == 0, jnp.where(rr != 0, prev, 0.0),
                   jnp.where(cls == 1, jnp.where(rr != rmask, nxt, 0.0),
                             jnp.where(cls == 2, up, down)))
    return zc + (zs - zc) * mu


def _rprep_kernel(zc_ref, zp_ref, zn_ref, lc_ref, lp_ref, ln_ref,
                  mu_ref, mul_ref, w0_ref, w2f_ref, w2b_ref, a0_ref, a2_ref, g2_ref,
                  kk_w_ref, ka_w_ref, rk_w_ref, e_ref, et_ref,
                  r_o, k_o, v_o, kk_o, kka_o, df_o, db_o, bonus_o, gate_o):
    tb = pl.program_id(1)
    ntb = pl.num_programs(1)
    is_ctx = tb == 0
    has_up = (tb > 1).astype(F32)
    has_down = (tb < ntb - 1).astype(F32)

    zl = _shift_mix(lc_ref[0], lp_ref[0] * has_up, ln_ref[0] * has_down, mul_ref[...], is_ctx)
    def decay(window, w2p_ref, d):
        w = -_softplus(-(w0_ref[d] + _dot_x3(jnp.tanh(window), w2p_ref))) - 0.5
        return -jnp.exp(w)

    df_o[0] = decay(zl[:, 0:128], w2f_ref, 0).astype(BF16)
    db_o[0] = decay(zl[:, 0:256], w2b_ref, 1).astype(BF16)
    a = _sigmoid(a0_ref[...] + _dot_x3(zl[:, 128:256], a2_ref))
    gate_o[0] = _dot_x3(_sigmoid(zl[:, 128:384]), g2_ref).astype(BF16)

    def piece(i):
        sl = slice(i * D, (i + 1) * D)
        return _shift_mix(zc_ref[0, :, sl].astype(F32), zp_ref[0, :, sl].astype(F32) * has_up,
                          zn_ref[0, :, sl].astype(F32) * has_down, mu_ref[:, sl], is_ctx)

    r = piece(0)
    kr = piece(1)
    vr = piece(2)
    e = e_ref[...]
    et = et_ref[...]
    kkraw = kr * kk_w_ref[...]
    nrm = jnp.sqrt(_dot_x2(kkraw * kkraw, e))
    inv = 1.0 / jnp.maximum(nrm, 1e-12)
    kk = kkraw * _dot_x2(inv, et)
    k2 = kr * (1.0 + (a - 1.0) * ka_w_ref[...])
    rk = _dot_x2(r * k2 * rk_w_ref[...], e)
    r_o[0] = r.astype(BF16)
    k_o[0] = k2.astype(BF16)
    v_o[0] = vr.astype(BF16)
    kk_o[0] = kk.astype(BF16)
    kka_o[0] = (kk * a).astype(BF16)
    bonus_o[0] = (_dot_x2(rk, et) * vr).astype(BF16)


def _rprep(z, zl, mu_main, mu_lora, w0, w2f, w2b, a0, a2, g2, k_k, k_a, r_k, e, et):
    b, t, _ = z.shape
    ntb = t // ROWS
    q = ROWS // GRID_COLS
    last64 = t // GRID_COLS - 1
    wm = 3 * D
    cm = C_RWKV // wm
    cl = 0
    cur = lambda c: (lambda i, j: (i, j, c))
    prv = lambda c: (lambda i, j: (i, jnp.maximum(j * q - 1, 0), c))
    nxt = lambda c: (lambda i, j: (i, jnp.minimum(j * q + q, last64), c))
    full = lambda *s: pl.BlockSpec(s, lambda i, j: (0,) * len(s))
    out_spec = pl.BlockSpec((1, ROWS, D), lambda i, j: (i, j, 0))
    return pl.pallas_call(
        _rprep_kernel,
        grid=(b, ntb),
        in_specs=[pl.BlockSpec((1, ROWS, wm), cur(cm)),
                  pl.BlockSpec((1, GRID_COLS, wm), prv(cm)),
                  pl.BlockSpec((1, GRID_COLS, wm), nxt(cm)),
                  pl.BlockSpec((1, ROWS, 512), cur(cl)),
                  pl.BlockSpec((1, GRID_COLS, 512), prv(cl)),
                  pl.BlockSpec((1, GRID_COLS, 512), nxt(cl)),
                  full(1, wm), full(1, 512), full(2, 1, D), full(2, 128, D), full(2, 256, D),
                  full(1, D), full(2, 128, D), full(2, 256, D),
                  full(1, D), full(1, D), full(1, D), full(D, R_HEADS), full(R_HEADS, D)],
        out_specs=[out_spec] * 9,
        out_shape=[jax.ShapeDtypeStruct((b, t, D), BF16)] * 9,
        compiler_params=_cparams(("parallel", "parallel"), 52),
        name="rprep",
    )(z, z, z, zl, zl, zl, mu_main, mu_lora, w0, w2f, w2b, a0, a2, g2, k_k, k_a, r_k, e, et)


R_STEPS = 64


def _rwkv_kernel(r_ref, k_ref, v_ref, kk_ref, kka_ref, lwf_ref, lwb_ref, o_ref, s_ref, rows_ref):
    dirn = pl.program_id(0)
    tix = lambda i: jnp.where(dirn == 0, i, R_STEPS - 1 - i)

    @pl.when(pl.program_id(1) == 0)
    def _():
        s_ref[...] = jnp.zeros_like(s_ref)

    def weighted_sum(rows):
        acc = s_ref[0] * rows_ref[rows, 0:1, :]
        for k in range(1, R_N):
            acc = acc + s_ref[k] * rows_ref[rows, k:k + 1, :]
        return acc

    rows_ref[0] = kk_ref[tix(0)].astype(F32)
    sa0 = weighted_sum(0)

    def run(with_out):
        def step(i, carry):
            sa, lcum = carry
            t = tix(i)
            t_next = tix(jnp.minimum(i + 1, R_STEPS - 1))
            lcum = lcum + jnp.where(dirn == 0, lwf_ref[t], lwb_ref[t]).astype(F32)
            e_pos = jnp.exp(lcum)
            e_neg = jnp.exp(-lcum)
            rows_ref[0] = kk_ref[t_next].astype(F32) * e_pos
            rows_ref[1] = kka_ref[t].astype(F32) * e_neg
            rows_ref[2] = k_ref[t].astype(F32) * e_neg
            if with_out:
                rows_ref[3] = r_ref[t].astype(F32) * e_pos
            vv = v_ref[t].astype(F32)
            out = None
            sa_next = None
            for k in range(R_N):
                sk = s_ref[k] - sa * rows_ref[1, k:k + 1, :] + vv * rows_ref[2, k:k + 1, :]
                s_ref[k] = sk
                s_term = sk * rows_ref[0, k:k + 1, :]
                sa_next = s_term if sa_next is None else sa_next + s_term
                if with_out:
                    o_term = sk * rows_ref[3, k:k + 1, :]
                    out = o_term if out is None else out + o_term
            o_ref[0, t] = out.astype(BF16) if with_out else jnp.zeros(o_ref.shape[2:], BF16)
            return sa_next, lcum

        return lax.fori_loop(0, R_STEPS, step, (sa0, jnp.zeros((R_N, s_ref.shape[2]), F32)))[1]

    lcum = lax.cond(pl.program_id(1) < N_CTX // R_STEPS, lambda: run(False), lambda: run(True))
    rows_ref[0] = jnp.exp(lcum)
    for k in range(R_N):
        s_ref[k] = s_ref[k] * rows_ref[0, k:k + 1, :]


def _rwkv_order(dirn, nb, n_ctx_blk, n_blk):
    back = jnp.where(nb < n_ctx_blk, n_ctx_blk - 1 - nb, n_blk + n_ctx_blk - 1 - nb)
    return jnp.where(dirn == 0, nb, back)


def _rwkv(r_t, k_t, v_t, kk_t, kka_t, lwf_t, lwb_t):
    t, _, nch = r_t.shape
    n_blk = t // R_STEPS
    n_ctx_blk = N_CTX // R_STEPS
    blk = lambda d, n: (_rwkv_order(d, n, n_ctx_blk, n_blk), 0, 0)
    blk4 = lambda d, n: (d, _rwkv_order(d, n, n_ctx_blk, n_blk), 0, 0)
    only = lambda own: (lambda d, n: (jnp.where(d == own, _rwkv_order(d, n, n_ctx_blk, n_blk), 0), 0, 0))
    spec = pl.BlockSpec((R_STEPS, R_N, nch), blk)
    spec4 = pl.BlockSpec((1, R_STEPS, R_N, nch), blk4)
    return pl.pallas_call(
        _rwkv_kernel,
        grid=(2, n_blk),
        in_specs=[spec, spec, spec, spec, spec,
                  pl.BlockSpec((R_STEPS, R_N, nch), only(0)), pl.BlockSpec((R_STEPS, R_N, nch), only(1))],
        out_specs=spec4,
        out_shape=jax.ShapeDtypeStruct((2, t, R_N, nch), BF16),
        scratch_shapes=[pltpu.VMEM((R_N, R_N, nch), F32), pltpu.VMEM((4, R_N, nch), F32)],
        compiler_params=_cparams(("arbitrary", "arbitrary"), 48),
        name="rwkv",
    )(r_t, k_t, v_t, kk_t, kka_t, lwf_t, lwb_t)


def _log_sigmoid(x):
    return jnp.minimum(x, 0.0) - jnp.log(1.0 + jnp.exp(-jnp.abs(x)))


def _gla_dir(q, k, v, alo, wa, ba, st_ref, reverse, need_out):
    c = q.shape[0]
    q = q.astype(F32)
    k = k.astype(F32)
    g = _log_sigmoid(_dot_x3(alo, wa) + ba) * (1.0 / G_GATE_NORM)
    row = lax.broadcasted_iota(jnp.int32, (c, c), 0)
    col = lax.broadcasted_iota(jnp.int32, (c, c), 1)
    tri = (row <= col) if reverse else (row >= col)
    bcum = _dot_2x(tri.astype(BF16), g)
    blast = bcum[0:1] if reverse else bcum[c - 1:c]
    kl = (k * jnp.exp(blast - bcum)).astype(BF16)
    eb = jnp.exp(blast)
    vb = v.astype(BF16)
    if need_out:
        qd = (q * (G_DK ** -0.5) * jnp.exp(bcum)).astype(BF16)
        kd = (k * jnp.exp(-bcum)).astype(BF16)
    outs = []
    for h in range(G_HEADS):
        sk = slice(h * G_DK, (h + 1) * G_DK)
        sv = slice(h * G_DV, (h + 1) * G_DV)
        st = st_ref[h]
        if need_out:
            att = jnp.where(tri, _dot_nt(qd[:, sk], kd[:, sk]), 0.0)
            outs.append(_dot_nt(qd[:, sk], st.astype(BF16)) + _dot(att.astype(BF16), vb[:, sv]))
        st_ref[h] = st * eb[:, sk] + _dot_tn(vb[:, sv], kl[:, sk])
    return jnp.concatenate(outs, axis=-1).astype(BF16) if need_out else jnp.zeros(v.shape, BF16)


GLA_BATCH = 4


def _gla_kernel(qf, kf, vf, lf, qb, kb, vb, lb, wa_ref, ba_ref, of_ref, ob_ref, sf_ref, sb_ref):
    @pl.when(pl.program_id(1) == 0)
    def _():
        sf_ref[...] = jnp.zeros_like(sf_ref)
        sb_ref[...] = jnp.zeros_like(sb_ref)

    def run(need_out):
        for s in range(GLA_BATCH):
            of_ref[s] = _gla_dir(qf[s], kf[s], vf[s], lf[s], wa_ref.at[0], ba_ref[0], sf_ref.at[s], False, need_out)
            ob_ref[s] = _gla_dir(qb[s], kb[s], vb[s], lb[s], wa_ref.at[1], ba_ref[1], sb_ref.at[s], True, need_out)

    is_ctx = pl.program_id(1) < N_CTX // G_CHUNK
    pl.when(is_ctx)(lambda: run(False))
    pl.when(jnp.logical_not(is_ctx))(lambda: run(True))


def _gla(z, zl, w_a2, b_a):
    b, t, _ = z.shape
    n_blk = t // G_CHUNK
    n_ctx_blk = N_CTX // G_CHUNK
    fwd = lambda n: n
    bwd = lambda n: _rwkv_order(1, n, n_ctx_blk, n_blk)
    wk = G_HEADS * G_DK
    cq, ck, cv = C_GLA // wk, (C_GLA + wk) // wk, (C_GLA + 2 * wk) // D
    cl = 0

    nb = GLA_BATCH
    assert b % nb == 0

    def specs(order):
        return [pl.BlockSpec((nb, G_CHUNK, wk), lambda i, n: (i, order(n), cq)),
                pl.BlockSpec((nb, G_CHUNK, wk), lambda i, n: (i, order(n), ck)),
                pl.BlockSpec((nb, G_CHUNK, D), lambda i, n: (i, order(n), cv)),
                pl.BlockSpec((nb, G_CHUNK, 128), lambda i, n: (i, order(n), cl))]

    out_sd = jax.ShapeDtypeStruct((b, t, D), BF16)
    state = pltpu.VMEM((nb, G_HEADS, G_DV, G_DK), F32)
    return pl.pallas_call(
        _gla_kernel,
        grid=(b // nb, n_blk),
        in_specs=specs(fwd) + specs(bwd) + [
            pl.BlockSpec((2, 2, 128, wk), lambda i, n: (0, 0, 0, 0)),
            pl.BlockSpec((2, 1, wk), lambda i, n: (0, 0, 0))],
        out_specs=[pl.BlockSpec((nb, G_CHUNK, D), lambda i, n: (i, fwd(n), 0)),
                   pl.BlockSpec((nb, G_CHUNK, D), lambda i, n: (i, bwd(n), 0))],
        out_shape=[out_sd, out_sd],
        scratch_shapes=[state, state],
        compiler_params=_cparams(("parallel", "arbitrary"), 24),
        name="gla",
    )(z, z, z, zl, z, z, z, zl, w_a2, b_a)


def _mix_kernel(ogf_ref, ogb_ref, gout_ref, orf_ref, orb_ref, bonus_ref, gr_ref, gg_ref, grw_ref, x_ref,
                gnw_ref, lnw_ref, lnb_ref, e_ref, et_ref, wout_ref, m2_ref, n2w_ref, m3_ref, m4_ref,
                h1_o, hmb_o):
    og = ogf_ref[0].astype(F32) + ogb_ref[0].astype(F32)
    parts = []
    for h in range(G_HEADS):
        seg = og[:, h * G_DV:(h + 1) * G_DV]
        parts.append(seg * lax.rsqrt(jnp.mean(seg * seg, axis=-1, keepdims=True) + EPS))
    gout = gout_ref[0].astype(F32)
    y_gla = jnp.concatenate(parts, axis=-1) * gnw_ref[...] * (gout * _sigmoid(gout))

    e = e_ref[...]
    et = et_ref[...]
    o_r = orf_ref[0].astype(F32) + orb_ref[0].astype(F32)
    mu = _dot_x2(_dot_x2(o_r, e) * (1.0 / R_N), et)
    dlt = o_r - mu
    var = _dot_x2(dlt * dlt, e) * (1.0 / R_N)
    o_r = (dlt * _dot_x2(lax.rsqrt(var + R_LN_EPS), et) * lnw_ref[...] + lnb_ref[...]
           + bonus_ref[0].astype(F32))
    y_rwkv = o_r * gr_ref[0].astype(F32)

    y = _sigmoid(gg_ref[0].astype(F32)) * y_gla + _sigmoid(grw_ref[0].astype(F32)) * y_rwkv
    h1 = x_ref[0] + m2_ref[0] * _dot(y.astype(BF16), wout_ref[...])
    h1_o[0] = h1
    hn = h1 * lax.rsqrt(jnp.mean(h1 * h1, axis=-1, keepdims=True) + EPS) * n2w_ref[...]
    hmb_o[0] = (hn * (1.0 + m4_ref[0]) + m3_ref[0]).astype(BF16)


def _mix(og_f, og_b, z, or_f, or_b, bonus, gate_r, x, gnw, lnw, lnb, e, et, w_out, m2, n2w, m3, m4):
    b, tl, _ = x.shape
    off = N_CTX // ROWS
    seq = lambda c: pl.BlockSpec((1, ROWS, D), lambda i, j: (i, j + off, c))
    full = lambda *s: pl.BlockSpec(s, lambda i, j: (0,) * len(s))
    per_b = pl.BlockSpec((1, 1, D), lambda i, j: (i, 0, 0))
    lat = pl.BlockSpec((1, ROWS, D), lambda i, j: (i, j, 0))
    return pl.pallas_call(
        _mix_kernel,
        grid=(b, tl // ROWS),
        in_specs=[seq(0), seq(0), seq((C_GLA + 2048) // D), seq(0), seq(0), seq(0), seq(0),
                  seq(C_GATE // D), seq(C_GATE // D + 1), lat,
                  full(1, D), full(1, D), full(1, D), full(D, R_HEADS), full(R_HEADS, D), full(D, D),
                  per_b, full(1, D), per_b, per_b],
        out_specs=[lat, lat],
        out_shape=[jax.ShapeDtypeStruct((b, tl, D), F32), jax.ShapeDtypeStruct((b, tl, D), BF16)],
        compiler_params=_cparams(("parallel", "parallel"), 52),
        name="mix",
    )(og_f, og_b, z, or_f, or_b, bonus, gate_r, z, z, x, gnw, lnw, lnb, e, et, w_out, m2, n2w, m3, m4)


ROUTE_TOK = 1024


N_CAND = 80


def _dup_bf16(x):
    hi = lax.bitcast_convert_type(x.astype(BF16).astype(F32), jnp.uint32)
    return hi | (hi >> 16)


def _route_kernel(x_ref, wq_ref, keys_ref, n_o, e1_o, rho_o, e2_o,
                  q_scr, s_scr, t1_scr, t2_scr, cand_scr, rho_scr, cnt_scr):
    q_scr[...] = _dot(wq_ref[...], x_ref[...]).astype(BF16)
    ninf = -jnp.inf
    half = P_TOPK // 2
    tn = x_ref.shape[1]
    top_row = jnp.full((1, tn), jnp.inf, F32)

    def next_below(vals, bound):
        below = vals < bound
        return below, jnp.max(jnp.where(below, vals, ninf), axis=0, keepdims=True)

    def head(h, carry):
        q1 = q_scr[pl.ds(pl.multiple_of(h * 2 * P_KEYS, P_KEYS), P_KEYS), :]
        q2 = q_scr[pl.ds(pl.multiple_of(h * 2 * P_KEYS + P_KEYS, P_KEYS), P_KEYS), :]
        s_scr[0] = _dot(keys_ref[2 * h], q1)
        s_scr[1] = _dot(keys_ref[2 * h + 1], q2)
        rho_scr[...] = jnp.full(rho_scr.shape, -1.0, F32)

        def extract(i, bounds):
            m1, m2 = bounds
            _, m1 = next_below(s_scr[0], m1)
            below2, m2 = next_below(s_scr[1], m2)
            rho_scr[...] += jnp.where(below2, 1.0, 0.0)
            t1_scr[pl.ds(i, 1), :] = m1
            t2_scr[pl.ds(i, 1), :] = m2
            return m1, m2

        _, last2 = lax.fori_loop(0, P_TOPK, extract, (top_row, top_row))
        s1 = s_scr[0]
        s2 = s_scr[1]
        t1 = t1_scr[...]
        t2 = t2_scr[...]
        cand_scr[0:P_TOPK] = t1[0:1] + t2
        for i in range(1, half):
            cand_scr[P_TOPK + half * (i - 1):P_TOPK + half * i] = t1[i:i + 1] + t2[0:half]
        cand_scr[N_CAND - half:N_CAND] = t1[half:P_TOPK] + t2[0:1]
        cmax = t1[0:1] + t2[0:1]

        def pick(i, st):
            m, z = st
            _, m = next_below(cand_scr[...], m)
            return m, z + jnp.exp(m - cmax)

        tau, zsum = lax.fori_loop(0, P_TOPK, pick, (top_row, jnp.zeros_like(cmax)))
        cnt_scr[...] = jnp.zeros_like(cnt_scr)

        def count(j, c):
            cnt_scr[...] += jnp.where(s_scr[0] + t2_scr[pl.ds(j, 1), :] >= tau, 1.0, 0.0)
            return c

        lax.fori_loop(0, P_TOPK, count, 0)
        sel1 = s1 >= t1[P_TOPK - 1:P_TOPK]
        sel2 = s2 >= last2
        rho = rho_scr[...] + jnp.where(s2 < last2, 1.0, 0.0)
        n_o[h] = _dup_bf16(jnp.where(sel1, cnt_scr[...], 0.0))
        e1_o[h] = _dup_bf16(jnp.where(sel1, jnp.exp(s1 - t1[0:1]), 0.0))
        rho_o[h] = rho.astype(BF16)
        e2_o[h] = jnp.where(sel2, jnp.exp(s2 - t2[0:1]) / zsum, 0.0).astype(BF16)
        return carry

    lax.fori_loop(0, P_HEADS, head, 0)


def _route(hm_t, wq_t, keys):
    n = hm_t.shape[1]
    tn = ROUTE_TOK
    big = pl.BlockSpec((P_HEADS, P_KEYS, tn), lambda i: (0, 0, i))
    sd = lambda dt: jax.ShapeDtypeStruct((P_HEADS, P_KEYS, n), dt)
    return pl.pallas_call(
        _route_kernel,
        grid=(n // tn,),
        in_specs=[pl.BlockSpec((D, tn), lambda i: (0, i)),
                  pl.BlockSpec((2 * P_HEADS * P_KEYS, D), lambda i: (0, 0)),
                  pl.BlockSpec((2 * P_HEADS, P_KEYS, P_KEYS), lambda i: (0, 0, 0))],
        out_specs=[big, big, big, big],
        out_shape=[sd(jnp.uint32), sd(jnp.uint32), sd(BF16), sd(BF16)],
        scratch_shapes=[pltpu.VMEM((2 * P_HEADS * P_KEYS, tn), BF16), pltpu.VMEM((2, P_KEYS, tn), F32),
                        pltpu.VMEM((P_TOPK, tn), F32), pltpu.VMEM((P_TOPK, tn), F32),
                        pltpu.VMEM((N_CAND, tn), F32),
                        pltpu.VMEM((P_KEYS, tn), F32), pltpu.VMEM((P_KEYS, tn), F32)],
        compiler_params=_cparams(("parallel",), 48),
        name="route",
    )(hm_t, wq_t, keys)


PEER_TOK = 512
PEER_EXP = 2048
PEER_SUB = 512


def _gelu(x):
    return 0.5 * x * (1.0 + lax.erf(x * (2.0 ** -0.5)))


BF16_ROWS = 16


def _peer_kernel(x_ref, u_ref, vt_ref, n_ref, e1_ref, rho_ref, e2_ref, h1_ref, m5_ref, fw_ref, o_ref, acc_ref):
    j = pl.program_id(1)

    @pl.when(j == 0)
    def _():
        acc_ref[...] = jnp.zeros_like(acc_ref)

    tn = x_ref.shape[1]
    x = x_ref[...]
    groups = P_KEYS // BF16_ROWS
    a_per_half = PEER_SUB // P_KEYS
    p_halves = []
    for half in range(PEER_EXP // PEER_SUB):
        rows_h = PEER_SUB
        act = _dot(u_ref[half * rows_h:(half + 1) * rows_h, :], x)
        gates = []
        for al in range(half * a_per_half, (half + 1) * a_per_half):
            g = [None] * groups
            for h in range(P_HEADS):
                row = lambda ref: pltpu.bitcast(jnp.broadcast_to(ref[h, al:al + 1, :], (8, tn)), BF16)
                n_row = row(n_ref)
                e1_row = row(e1_ref)
                for q in range(groups):
                    rows = slice(q * BF16_ROWS, (q + 1) * BF16_ROWS)
                    term = jnp.where(rho_ref[h, rows, :] < n_row, e2_ref[h, rows, :], 0.0) * e1_row
                    g[q] = term if g[q] is None else g[q] + term
            gates.extend(g)
        p_halves.append(_gelu(act).astype(BF16) * jnp.concatenate(gates, axis=0))
    acc_ref[...] += _dot(vt_ref[...], jnp.concatenate(p_halves, axis=0))

    @pl.when(j == pl.num_programs(1) - 1)
    def _():
        h = h1_ref[...] + m5_ref[0] * acc_ref[...].T
        o_ref[...] = h * lax.rsqrt(jnp.mean(h * h, axis=-1, keepdims=True) + EPS) * fw_ref[...]


def _peer(hm_t, u_b, vt_b, n_cnt, e1, rho, e2, h1, m5, final_w, seq):
    n = hm_t.shape[1]
    n_exp = u_b.shape[0]
    tn, te = PEER_TOK, PEER_EXP
    big = pl.BlockSpec((P_HEADS, P_KEYS, tn), lambda i, j: (0, 0, i))
    a_rows = pl.BlockSpec((P_HEADS, te // P_KEYS, tn), lambda i, j: (0, j, i))
    tok = pl.BlockSpec((tn, D), lambda i, j: (i, 0))
    return pl.pallas_call(
        _peer_kernel,
        grid=(n // tn, n_exp // te),
        in_specs=[pl.BlockSpec((D, tn), lambda i, j: (0, i)),
                  pl.BlockSpec((te, D), lambda i, j: (j, 0)),
                  pl.BlockSpec((D, te), lambda i, j: (0, j)),
                  a_rows, a_rows, big, big,
                  tok, pl.BlockSpec((1, 1, D), lambda i, j: (i // (seq // tn), 0, 0)),
                  pl.BlockSpec((1, D), lambda i, j: (0, 0))],
        out_specs=tok,
        out_shape=jax.ShapeDtypeStruct((n, D), F32),
        scratch_shapes=[pltpu.VMEM((D, tn), F32)],
        compiler_params=_cparams(("parallel", "arbitrary"), 52),
        name="peer",
    )(hm_t, u_b, vt_b, n_cnt, e1, rho, e2, h1, m5, final_w)


def _reorder_cols(w):
    gla_main, gla_lora = w[..., 0:3072], w[..., 3072:3104]
    rw_main, rw_lora = w[..., 3104:6176], w[..., 6176:6528]
    gates = w[..., 6528:8576]
    pad = jnp.zeros(w.shape[:-1] + (512 - 32 - 352,), w.dtype)
    return jnp.concatenate([rw_main, gla_main, gates, gla_lora, rw_lora, pad], axis=-1)


def _hi_lo(w):
    hi = w.astype(BF16)
    return jnp.stack([hi, (w - hi.astype(F32)).astype(BF16)])


def _to_chains(a):
    b, t, _ = a.shape
    return a.reshape(b, t, R_HEADS, R_N).transpose(1, 3, 0, 2).reshape(t, R_N, b * R_HEADS)


def _from_chains(a, b):
    t = a.shape[0]
    return a.reshape(t, R_N, b, R_HEADS).transpose(2, 0, 3, 1).reshape(b, t, D)


def kernel(x, c, ctx, c_ctx, norm1_w, w_mod, b_mod, w_in, gla_w_a2, gla_b_a, gla_norm_w, rwkv_mu, rwkv_w0, rwkv_w2, rwkv_a0, rwkv_a2, rwkv_g2, rwkv_k_k, rwkv_k_a, rwkv_r_k, rwkv_ln_w, rwkv_ln_b, w_out, norm2_w, peer_w_q, peer_sub_keys, peer_u, peer_v, final_norm_w):
    b, seq, _ = x.shape
    assert w_in.shape[0] == 1 and ctx.shape[1] == N_CTX and seq % ROWS == 0
    row = lambda v: v.reshape(1, -1)

    c16 = jnp.zeros((16, D), F32).at[:b].set(c).at[b].set(c_ctx)
    m = _mod(c16, w_mod[0], row(b_mod[0])).reshape(16, N_MOD, D)
    m_lat, m_ctx = m[:b], m[b]
    per_b = lambda i: m_lat[:, i].reshape(b, 1, D)
    shift1 = jnp.stack([jnp.broadcast_to(m_ctx[0], (b, D)), m_lat[:, 0]], axis=1).reshape(b, 2, 1, D)
    scale1 = jnp.stack([jnp.broadcast_to(m_ctx[1], (b, D)), m_lat[:, 1]], axis=1).reshape(b, 2, 1, D)

    h_all = jnp.concatenate([ctx, x], axis=1)
    z, z_lora = _inproj(h_all, row(norm1_w[0]), scale1, shift1, _reorder_cols(w_in[0]).astype(BF16))

    mu = rwkv_mu[0]
    mu_lora = jnp.concatenate([jnp.zeros((32,), F32), mu[3072:], jnp.zeros((128,), F32)]).reshape(1, 512)
    e = jnp.repeat(jnp.eye(R_HEADS, dtype=BF16), R_N, axis=0)
    w2f = _hi_lo(jnp.zeros((128, D), F32).at[L_WF:L_WF + R_W_LORA].set(rwkv_w2[0, 0]))
    w2b = _hi_lo(jnp.zeros((256, D), F32).at[L_WB:L_WB + R_W_LORA].set(rwkv_w2[0, 1]))
    a2 = _hi_lo(jnp.zeros((128, D), F32).at[L_A - 128:L_A - 128 + R_A_LORA].set(rwkv_a2[0]))
    g2 = _hi_lo(jnp.zeros((256, D), F32).at[L_G - 128:L_G - 128 + R_G_LORA].set(rwkv_g2[0]))
    wa = jnp.zeros((2, 128, G_HEADS * G_DK), F32)
    wa = wa.at[0, 0:G_LORA].set(gla_w_a2[0, 0]).at[1, G_LORA:2 * G_LORA].set(gla_w_a2[0, 1])
    wa = jnp.stack([_hi_lo(wa[0]), _hi_lo(wa[1])])
    r, k2, vr, kk, kka, d_f, d_b, bonus, gate_r = _rprep(
        z, z_lora, row(mu[:3072]), mu_lora, rwkv_w0[0].reshape(2, 1, D), w2f, w2b, row(rwkv_a0[0]), a2,
        g2, row(rwkv_k_k[0]), row(rwkv_k_a[0]), row(rwkv_r_k[0]), e, e.T)

    o_r = _rwkv(_to_chains(r), _to_chains(k2), _to_chains(vr), _to_chains(kk), _to_chains(kka),
                _to_chains(d_f), _to_chains(d_b))
    or_f, or_b = _from_chains(o_r[0], b), _from_chains(o_r[1], b)

    og_f, og_b = _gla(z, z_lora, wa, gla_b_a[0].reshape(2, 1, -1))

    h1, hm_b = _mix(og_f, og_b, z, or_f, or_b, bonus, gate_r, x, row(gla_norm_w[0]), row(rwkv_ln_w[0]),
                        row(rwkv_ln_b[0]), e, e.T, w_out[0].astype(BF16), per_b(2), row(norm2_w[0]),
                        per_b(3), per_b(4))

    n_tok = b * seq
    hm_t = hm_b.reshape(n_tok, D).T
    n_cnt, e1, rho, e2 = _route(hm_t, peer_w_q[0].T.astype(BF16),
                                peer_sub_keys[0].reshape(2 * P_HEADS, P_KEYS, -1).astype(BF16))
    assert seq % PEER_TOK == 0
    out = _peer(hm_t, peer_u[0].astype(BF16), peer_v[0].T.astype(BF16), n_cnt, e1, rho, e2,
                h1.reshape(n_tok, D), per_b(5), row(final_norm_w), seq)
    return out.reshape(b, seq, D)
```

```python
import jax
import jax.numpy as jnp
from jax import lax
from jax.experimental import pallas as pl
from jax.experimental.pallas import tpu as pltpu

F32 = jnp.float32
BF16 = jnp.bfloat16

D = 1024
GRID_COLS = 64
N_CTX = 256
EPS = 1e-6
N_MOD = 6

G_HEADS = 4
G_DV = D // G_HEADS
G_DK = G_DV // 2
G_LORA = 16
G_GATE_NORM = 16.0
G_CHUNK = 64

R_N = 64
R_HEADS = D // R_N
R_LN_EPS = 64e-5
R_W_LORA = 64
R_A_LORA = 64
R_G_LORA = 160

P_HEADS = 8
P_KEYS = 128
P_TOPK = 16

C_RWKV = 0
C_GLA = 3072
C_GATE = 6144
C_LORA = 8192
LORA_W = 512
N_PROJ = C_LORA + LORA_W
L_WF, L_WB, L_A, L_G = 32, 96, 160, 224

ROWS = 256
V7X_VMEM_BYTES = 64 * 1024 * 1024


def _cparams(sem, vmem_mb):
    return pltpu.CompilerParams(dimension_semantics=sem,
                                vmem_limit_bytes=min(vmem_mb * 1024 * 1024, V7X_VMEM_BYTES - (8 << 20)))


def _sigmoid(x):
    return 1.0 / (1.0 + jnp.exp(-x))


def _softplus(x):
    return jnp.maximum(x, 0.0) + jnp.log(1.0 + jnp.exp(-jnp.abs(x)))


def _dot(a, b):
    return jnp.dot(a, b, preferred_element_type=F32)


def _dot_nt(a, b):
    return lax.dot_general(a, b, (((1,), (1,)), ((), ())), preferred_element_type=F32)


def _dot_tn(a, b):
    return lax.dot_general(a, b, (((0,), (0,)), ((), ())), preferred_element_type=F32)


def _split2(x):
    hi = x.astype(BF16)
    return hi, (x - hi.astype(F32)).astype(BF16)


def _dot_x2(x, w):
    hi, lo = _split2(x)
    return _dot(hi, w) + _dot(lo, w)


def _dot_2x(w, x):
    hi, lo = _split2(x)
    return _dot(w, hi) + _dot(w, lo)


def _dot_x3(x, w_ref):
    hi, lo = _split2(x)
    return _dot(hi, w_ref[0]) + (_dot(lo, w_ref[0]) + _dot(hi, w_ref[1]))


def _mod_kernel(c_ref, w_ref, b_ref, o_ref):
    c = c_ref[...]
    s = c * _sigmoid(c)
    o_ref[...] = _dot(s.astype(BF16), w_ref[...].astype(BF16)) + b_ref[...]


def _mod(c16, w_mod, b_mod):
    n = w_mod.shape[1]
    tn = 768
    return pl.pallas_call(
        _mod_kernel,
        grid=(n // tn,),
        in_specs=[pl.BlockSpec((16, D), lambda j: (0, 0)),
                  pl.BlockSpec((D, tn), lambda j: (0, j)),
                  pl.BlockSpec((1, tn), lambda j: (0, j))],
        out_specs=pl.BlockSpec((16, tn), lambda j: (0, j)),
        out_shape=jax.ShapeDtypeStruct((16, n), F32),
        compiler_params=_cparams(("parallel",), 24),
        name="mod",
    )(c16, w_mod, b_mod)


INPROJ_ROWS = 2304


def _inproj_kernel(h_ref, nw_ref, sc_ref, sh_ref, w_ref, o_ref, ol_ref, a_scr):
    n = pl.program_id(2)

    @pl.when(n == 0)
    def _():
        x = h_ref[0]
        y = x * lax.rsqrt(jnp.mean(x * x, axis=-1, keepdims=True) + EPS) * nw_ref[...]
        t = pl.program_id(1) * INPROJ_ROWS + lax.broadcasted_iota(jnp.int32, (INPROJ_ROWS, 1), 0)
        is_ctx = t < N_CTX
        sc = jnp.where(is_ctx, sc_ref[0, 0], sc_ref[0, 1])
        sh = jnp.where(is_ctx, sh_ref[0, 0], sh_ref[0, 1])
        a_scr[...] = (y * (1.0 + sc) + sh).astype(BF16)

    acc = _dot(a_scr[...], w_ref[...])
    n_main = C_LORA // LORA_W

    @pl.when(n < n_main)
    def _():
        o_ref[0] = acc.astype(BF16)

    @pl.when(n == n_main)
    def _():
        ol_ref[0] = acc


def _inproj(h_all, norm_w, scale, shift, w_p):
    b, t, _ = h_all.shape
    tn = LORA_W
    n_main = C_LORA // tn
    mod = pl.BlockSpec((1, 2, 1, D), lambda i, j, n: (i, 0, 0, 0))
    return pl.pallas_call(
        _inproj_kernel,
        grid=(b, t // INPROJ_ROWS, N_PROJ // tn),
        in_specs=[pl.BlockSpec((1, INPROJ_ROWS, D), lambda i, j, n: (i, j, 0)),
                  pl.BlockSpec((1, D), lambda i, j, n: (0, 0)),
                  mod, mod,
                  pl.BlockSpec((D, tn), lambda i, j, n: (0, n))],
        out_specs=[pl.BlockSpec((1, INPROJ_ROWS, tn), lambda i, j, n: (i, j, jnp.minimum(n, n_main - 1))),
                   pl.BlockSpec((1, INPROJ_ROWS, tn), lambda i, j, n: (i, j, 0))],
        out_shape=[jax.ShapeDtypeStruct((b, t, C_LORA), BF16), jax.ShapeDtypeStruct((b, t, tn), F32)],
        scratch_shapes=[pltpu.VMEM((INPROJ_ROWS, D), BF16)],
        compiler_params=_cparams(("parallel", "parallel", "arbitrary"), 52),
        name="inproj",
    )(h_all, norm_w, scale, shift, w_p)


SEQ_R, SEQ_K, SEQ_V, SEQ_KK, SEQ_KKA, SEQ_LWF, SEQ_LWB = range(7)
N_SEQ = 7


def _shift_mix(zc, zp, zn, mu, is_ctx):
    rows, w = zc.shape
    row = lax.broadcasted_iota(jnp.int32, (rows, w), 0)
    lane = lax.broadcasted_iota(jnp.int32, (rows, w), 1)
    pmask = jnp.where(is_ctx, 1, 3)
    rmask = jnp.where(is_ctx, rows - 1, GRID_COLS - 1)
    cls = lane & pmask
    rr = row & rmask
    prev = pltpu.roll(zc, 1, 0)
    nxt = pltpu.roll(zc, rows - 1, 0)
    up = jnp.concatenate([zp, zc[:rows - GRID_COLS]], axis=0)
    down = jnp.concatenate([zc[GRID_COLS:], zn], axis=0)
    zs = jnp.where(cls == 0, jnp.where(rr != 0, prev, 0.0),
                   jnp.where(cls == 1, jnp.where(rr != rmask, nxt, 0.0),
                             jnp.where(cls == 2, up, down)))
    return zc + (zs - zc) * mu


def _rprep_kernel(zc_ref, zp_ref, zn_ref, lc_ref, lp_ref, ln_ref,
                  mu_ref, mul_ref, w0_ref, w2f_ref, w2b_ref, a0_ref, a2_ref, g2_ref,
                  kk_w_ref, ka_w_ref, rk_w_ref, e_ref, et_ref,
                  seq_o, bonus_o, gate_o):
    tb = pl.program_id(1)
    ntb = pl.num_programs(1)
    is_ctx = tb == 0
    has_up = (tb > 1).astype(F32)
    has_down = (tb < ntb - 1).astype(F32)

    zl = _shift_mix(lc_ref[0], lp_ref[0] * has_up, ln_ref[0] * has_down, mul_ref[...], is_ctx)
    def decay(window, w2p_ref, d):
        w = -_softplus(-(w0_ref[d] + _dot_x3(jnp.tanh(window), w2p_ref))) - 0.5
        return -jnp.exp(w)

    seq_o[0, SEQ_LWF] = decay(zl[:, 0:128], w2f_ref, 0).astype(BF16)
    seq_o[0, SEQ_LWB] = decay(zl[:, 0:256], w2b_ref, 1).astype(BF16)
    a = _sigmoid(a0_ref[...] + _dot_x3(zl[:, 128:256], a2_ref))
    gate_o[0] = _dot_x3(_sigmoid(zl[:, 128:384]), g2_ref).astype(BF16)

    def piece(i):
        sl = slice(i * D, (i + 1) * D)
        return _shift_mix(zc_ref[0, :, sl].astype(F32), zp_ref[0, :, sl].astype(F32) * has_up,
                          zn_ref[0, :, sl].astype(F32) * has_down, mu_ref[:, sl], is_ctx)

    r = piece(0)
    kr = piece(1)
    vr = piece(2)
    e = e_ref[...]
    et = et_ref[...]
    kkraw = kr * kk_w_ref[...]
    nrm = jnp.sqrt(_dot_x2(kkraw * kkraw, e))
    inv = 1.0 / jnp.maximum(nrm, 1e-12)
    kk = kkraw * _dot_x2(inv, et)
    k2 = kr * (1.0 + (a - 1.0) * ka_w_ref[...])
    rk = _dot_x2(r * k2 * rk_w_ref[...], e)
    seq_o[0, SEQ_R] = r.astype(BF16)
    seq_o[0, SEQ_K] = k2.astype(BF16)
    seq_o[0, SEQ_V] = vr.astype(BF16)
    seq_o[0, SEQ_KK] = kk.astype(BF16)
    seq_o[0, SEQ_KKA] = (kk * a).astype(BF16)
    bonus_o[0] = (_dot_x2(rk, et) * vr).astype(BF16)


def _rprep(z, zl, mu_main, mu_lora, w0, w2f, w2b, a0, a2, g2, k_k, k_a, r_k, e, et):
    b, t, _ = z.shape
    ntb = t // ROWS
    q = ROWS // GRID_COLS
    last64 = t // GRID_COLS - 1
    wm = 3 * D
    cm = C_RWKV // wm
    cl = 0
    cur = lambda c: (lambda i, j: (i, j, c))
    prv = lambda c: (lambda i, j: (i, jnp.maximum(j * q - 1, 0), c))
    nxt = lambda c: (lambda i, j: (i, jnp.minimum(j * q + q, last64), c))
    full = lambda *s: pl.BlockSpec(s, lambda i, j: (0,) * len(s))
    out_spec = pl.BlockSpec((1, ROWS, D), lambda i, j: (i, j, 0))
    return pl.pallas_call(
        _rprep_kernel,
        grid=(b, ntb),
        in_specs=[pl.BlockSpec((1, ROWS, wm), cur(cm)),
                  pl.BlockSpec((1, GRID_COLS, wm), prv(cm)),
                  pl.BlockSpec((1, GRID_COLS, wm), nxt(cm)),
                  pl.BlockSpec((1, ROWS, 512), cur(cl)),
                  pl.BlockSpec((1, GRID_COLS, 512), prv(cl)),
                  pl.BlockSpec((1, GRID_COLS, 512), nxt(cl)),
                  full(1, wm), full(1, 512), full(2, 1, D), full(2, 128, D), full(2, 256, D),
                  full(1, D), full(2, 128, D), full(2, 256, D),
                  full(1, D), full(1, D), full(1, D), full(D, R_HEADS), full(R_HEADS, D)],
        out_specs=[pl.BlockSpec((1, N_SEQ, ROWS, D), lambda i, j: (i, 0, j, 0)), out_spec, out_spec],
        out_shape=[jax.ShapeDtypeStruct((b, N_SEQ, t, D), BF16)] + [jax.ShapeDtypeStruct((b, t, D), BF16)] * 2,
        compiler_params=_cparams(("parallel", "parallel"), 52),
        name="rprep",
    )(z, z, z, zl, zl, zl, mu_main, mu_lora, w0, w2f, w2b, a0, a2, g2, k_k, k_a, r_k, e, et)


R_STEPS = 64


def _rwkv_kernel(r_ref, k_ref, v_ref, kk_ref, kka_ref, lwf_ref, lwb_ref, o_ref, s_ref, rows_ref):
    dirn = pl.program_id(0)
    tix = lambda i: jnp.where(dirn == 0, i, R_STEPS - 1 - i)

    @pl.when(pl.program_id(1) == 0)
    def _():
        s_ref[...] = jnp.zeros_like(s_ref)

    def weighted_sum(rows):
        acc = s_ref[0] * rows_ref[rows, 0:1, :]
        for k in range(1, R_N):
            acc = acc + s_ref[k] * rows_ref[rows, k:k + 1, :]
        return acc

    rows_ref[0] = kk_ref[0, tix(0)].astype(F32)
    sa0 = weighted_sum(0)

    def run(with_out):
        def step(i, carry):
            sa, lcum = carry
            t = tix(i)
            t_next = tix(jnp.minimum(i + 1, R_STEPS - 1))
            lcum = lcum + jnp.where(dirn == 0, lwf_ref[0, t], lwb_ref[0, t]).astype(F32)
            e_pos = jnp.exp(lcum)
            e_neg = jnp.exp(-lcum)
            rows_ref[0] = kk_ref[0, t_next].astype(F32) * e_pos
            rows_ref[1] = kka_ref[0, t].astype(F32) * e_neg
            rows_ref[2] = k_ref[0, t].astype(F32) * e_neg
            if with_out:
                rows_ref[3] = r_ref[0, t].astype(F32) * e_pos
            vv = v_ref[0, t].astype(F32)
            out = None
            sa_next = None
            for k in range(R_N):
                sk = s_ref[k] - sa * rows_ref[1, k:k + 1, :] + vv * rows_ref[2, k:k + 1, :]
                s_ref[k] = sk
                s_term = sk * rows_ref[0, k:k + 1, :]
                sa_next = s_term if sa_next is None else sa_next + s_term
                if with_out:
                    o_term = sk * rows_ref[3, k:k + 1, :]
                    out = o_term if out is None else out + o_term
            o_ref[0, t] = out.astype(BF16) if with_out else jnp.zeros(o_ref.shape[2:], BF16)
            return sa_next, lcum

        return lax.fori_loop(0, R_STEPS, step, (sa0, jnp.zeros((R_N, s_ref.shape[2]), F32)))[1]

    lcum = lax.cond(pl.program_id(1) < N_CTX // R_STEPS, lambda: run(False), lambda: run(True))
    rows_ref[0] = jnp.exp(lcum)
    for k in range(R_N):
        s_ref[k] = s_ref[k] * rows_ref[0, k:k + 1, :]


def _rwkv_order(dirn, nb, n_ctx_blk, n_blk):
    back = jnp.where(nb < n_ctx_blk, n_ctx_blk - 1 - nb, n_blk + n_ctx_blk - 1 - nb)
    return jnp.where(dirn == 0, nb, back)


def _rwkv(seq_t):
    _, t, _, nch = seq_t.shape
    n_blk = t // R_STEPS
    n_ctx_blk = N_CTX // R_STEPS
    order = lambda d, n: _rwkv_order(d, n, n_ctx_blk, n_blk)
    slot = lambda s: pl.BlockSpec((1, R_STEPS, R_N, nch), lambda d, n: (s, order(d, n), 0, 0))
    only = lambda s, own: pl.BlockSpec((1, R_STEPS, R_N, nch),
                                       lambda d, n: (s, jnp.where(d == own, order(d, n), 0), 0, 0))
    spec4 = pl.BlockSpec((1, R_STEPS, R_N, nch), lambda d, n: (d, order(d, n), 0, 0))
    return pl.pallas_call(
        _rwkv_kernel,
        grid=(2, n_blk),
        in_specs=[slot(SEQ_R), slot(SEQ_K), slot(SEQ_V), slot(SEQ_KK), slot(SEQ_KKA),
                  only(SEQ_LWF, 0), only(SEQ_LWB, 1)],
        out_specs=spec4,
        out_shape=jax.ShapeDtypeStruct((2, t, R_N, nch), BF16),
        scratch_shapes=[pltpu.VMEM((R_N, R_N, nch), F32), pltpu.VMEM((4, R_N, nch), F32)],
        compiler_params=_cparams(("arbitrary", "arbitrary"), 48),
        name="rwkv",
    )(seq_t, seq_t, seq_t, seq_t, seq_t, seq_t, seq_t)


def _log_sigmoid(x):
    return jnp.minimum(x, 0.0) - jnp.log(1.0 + jnp.exp(-jnp.abs(x)))


def _gla_dir(q, k, v, alo, wa, ba, st_ref, reverse, need_out):
    c = q.shape[0]
    q = q.astype(F32)
    k = k.astype(F32)
    g = _log_sigmoid(_dot_x3(alo, wa) + ba) * (1.0 / G_GATE_NORM)
    row = lax.broadcasted_iota(jnp.int32, (c, c), 0)
    col = lax.broadcasted_iota(jnp.int32, (c, c), 1)
    tri = (row <= col) if reverse else (row >= col)
    bcum = _dot_2x(tri.astype(BF16), g)
    blast = bcum[0:1] if reverse else bcum[c - 1:c]
    kl = (k * jnp.exp(blast - bcum)).astype(BF16)
    eb = jnp.exp(blast)
    vb = v.astype(BF16)
    if need_out:
        qd = (q * (G_DK ** -0.5) * jnp.exp(bcum)).astype(BF16)
        kd = (k * jnp.exp(-bcum)).astype(BF16)
    outs = []
    for h in range(G_HEADS):
        sk = slice(h * G_DK, (h + 1) * G_DK)
        sv = slice(h * G_DV, (h + 1) * G_DV)
        st = st_ref[h]
        if need_out:
            att = jnp.where(tri, _dot_nt(qd[:, sk], kd[:, sk]), 0.0)
            outs.append(_dot_nt(qd[:, sk], st.astype(BF16)) + _dot(att.astype(BF16), vb[:, sv]))
        st_ref[h] = st * eb[:, sk] + _dot_tn(vb[:, sv], kl[:, sk])
    return jnp.concatenate(outs, axis=-1).astype(BF16) if need_out else jnp.zeros(v.shape, BF16)


GLA_BATCH = 4


def _gla_kernel(qf, kf, vf, lf, qb, kb, vb, lb, wa_ref, ba_ref, of_ref, ob_ref, sf_ref, sb_ref):
    @pl.when(pl.program_id(1) == 0)
    def _():
        sf_ref[...] = jnp.zeros_like(sf_ref)
        sb_ref[...] = jnp.zeros_like(sb_ref)

    def run(need_out):
        for s in range(GLA_BATCH):
            of_ref[s] = _gla_dir(qf[s], kf[s], vf[s], lf[s], wa_ref.at[0], ba_ref[0], sf_ref.at[s], False, need_out)
            ob_ref[s] = _gla_dir(qb[s], kb[s], vb[s], lb[s], wa_ref.at[1], ba_ref[1], sb_ref.at[s], True, need_out)

    is_ctx = pl.program_id(1) < N_CTX // G_CHUNK
    pl.when(is_ctx)(lambda: run(False))
    pl.when(jnp.logical_not(is_ctx))(lambda: run(True))


def _gla(z, zl, w_a2, b_a):
    b, t, _ = z.shape
    n_blk = t // G_CHUNK
    n_ctx_blk = N_CTX // G_CHUNK
    fwd = lambda n: n
    bwd = lambda n: _rwkv_order(1, n, n_ctx_blk, n_blk)
    wk = G_HEADS * G_DK
    cq, ck, cv = C_GLA // wk, (C_GLA + wk) // wk, (C_GLA + 2 * wk) // D
    cl = 0

    nb = GLA_BATCH
    assert b % nb == 0

    def specs(order):
        return [pl.BlockSpec((nb, G_CHUNK, wk), lambda i, n: (i, order(n), cq)),
                pl.BlockSpec((nb, G_CHUNK, wk), lambda i, n: (i, order(n), ck)),
                pl.BlockSpec((nb, G_CHUNK, D), lambda i, n: (i, order(n), cv)),
                pl.BlockSpec((nb, G_CHUNK, 128), lambda i, n: (i, order(n), cl))]

    out_sd = jax.ShapeDtypeStruct((b, t, D), BF16)
    state = pltpu.VMEM((nb, G_HEADS, G_DV, G_DK), F32)
    return pl.pallas_call(
        _gla_kernel,
        grid=(b // nb, n_blk),
        in_specs=specs(fwd) + specs(bwd) + [
            pl.BlockSpec((2, 2, 128, wk), lambda i, n: (0, 0, 0, 0)),
            pl.BlockSpec((2, 1, wk), lambda i, n: (0, 0, 0))],
        out_specs=[pl.BlockSpec((nb, G_CHUNK, D), lambda i, n: (i, fwd(n), 0)),
                   pl.BlockSpec((nb, G_CHUNK, D), lambda i, n: (i, bwd(n), 0))],
        out_shape=[out_sd, out_sd],
        scratch_shapes=[state, state],
        compiler_params=_cparams(("parallel", "arbitrary"), 24),
        name="gla",
    )(z, z, z, zl, z, z, z, zl, w_a2, b_a)


def _mix_kernel(ogf_ref, ogb_ref, gout_ref, orf_ref, orb_ref, bonus_ref, gr_ref, gg_ref, grw_ref, x_ref,
                gnw_ref, lnw_ref, lnb_ref, e_ref, et_ref, wout_ref, m2_ref, n2w_ref, m3_ref, m4_ref,
                h1_o, hmb_o):
    og = ogf_ref[0].astype(F32) + ogb_ref[0].astype(F32)
    parts = []
    for h in range(G_HEADS):
        seg = og[:, h * G_DV:(h + 1) * G_DV]
        parts.append(seg * lax.rsqrt(jnp.mean(seg * seg, axis=-1, keepdims=True) + EPS))
    gout = gout_ref[0].astype(F32)
    y_gla = jnp.concatenate(parts, axis=-1) * gnw_ref[...] * (gout * _sigmoid(gout))

    e = e_ref[...]
    et = et_ref[...]
    o_r = orf_ref[0].astype(F32) + orb_ref[0].astype(F32)
    mu = _dot_x2(_dot_x2(o_r, e) * (1.0 / R_N), et)
    dlt = o_r - mu
    var = _dot_x2(dlt * dlt, e) * (1.0 / R_N)
    o_r = (dlt * _dot_x2(lax.rsqrt(var + R_LN_EPS), et) * lnw_ref[...] + lnb_ref[...]
           + bonus_ref[0].astype(F32))
    y_rwkv = o_r * gr_ref[0].astype(F32)

    y = _sigmoid(gg_ref[0].astype(F32)) * y_gla + _sigmoid(grw_ref[0].astype(F32)) * y_rwkv
    h1 = x_ref[0] + m2_ref[0] * _dot(y.astype(BF16), wout_ref[...])
    h1_o[0] = h1
    hn = h1 * lax.rsqrt(jnp.mean(h1 * h1, axis=-1, keepdims=True) + EPS) * n2w_ref[...]
    hmb_o[0] = (hn * (1.0 + m4_ref[0]) + m3_ref[0]).astype(BF16)


def _mix(og_f, og_b, z, or_f, or_b, bonus, gate_r, x, gnw, lnw, lnb, e, et, w_out, m2, n2w, m3, m4):
    b, tl, _ = x.shape
    off = N_CTX // ROWS
    seq = lambda c: pl.BlockSpec((1, ROWS, D), lambda i, j: (i, j + off, c))
    full = lambda *s: pl.BlockSpec(s, lambda i, j: (0,) * len(s))
    per_b = pl.BlockSpec((1, 1, D), lambda i, j: (i, 0, 0))
    lat = pl.BlockSpec((1, ROWS, D), lambda i, j: (i, j, 0))
    return pl.pallas_call(
        _mix_kernel,
        grid=(b, tl // ROWS),
        in_specs=[seq(0), seq(0), seq((C_GLA + 2048) // D), seq(0), seq(0), seq(0), seq(0),
                  seq(C_GATE // D), seq(C_GATE // D + 1), lat,
                  full(1, D), full(1, D), full(1, D), full(D, R_HEADS), full(R_HEADS, D), full(D, D),
                  per_b, full(1, D), per_b, per_b],
        out_specs=[lat, lat],
        out_shape=[jax.ShapeDtypeStruct((b, tl, D), F32), jax.ShapeDtypeStruct((b, tl, D), BF16)],
        compiler_params=_cparams(("parallel", "parallel"), 52),
        name="mix",
    )(og_f, og_b, z, or_f, or_b, bonus, gate_r, z, z, x, gnw, lnw, lnb, e, et, w_out, m2, n2w, m3, m4)


ROUTE_TOK = 1024


N_CAND = 80


def _dup_bf16(x):
    hi = lax.bitcast_convert_type(x.astype(BF16).astype(F32), jnp.uint32)
    return hi | (hi >> 16)


def _route_kernel(x_ref, wq_ref, keys_ref, n_o, e1_o, rho_o, e2_o,
                  q_scr, s_scr, t1_scr, t2_scr, cand_scr, rho_scr, cnt_scr):
    q_scr[...] = _dot(wq_ref[...], x_ref[...]).astype(BF16)
    ninf = -jnp.inf
    half = P_TOPK // 2
    tn = x_ref.shape[1]
    top_row = jnp.full((1, tn), jnp.inf, F32)

    def next_below(vals, bound):
        below = vals < bound
        return below, jnp.max(jnp.where(below, vals, ninf), axis=0, keepdims=True)

    def head(h, carry):
        q1 = q_scr[pl.ds(pl.multiple_of(h * 2 * P_KEYS, P_KEYS), P_KEYS), :]
        q2 = q_scr[pl.ds(pl.multiple_of(h * 2 * P_KEYS + P_KEYS, P_KEYS), P_KEYS), :]
        s_scr[0] = _dot(keys_ref[2 * h], q1)
        s_scr[1] = _dot(keys_ref[2 * h + 1], q2)
        rho_scr[...] = jnp.full(rho_scr.shape, -1.0, F32)

        def extract(i, bounds):
            m1, m2 = bounds
            _, m1 = next_below(s_scr[0], m1)
            below2, m2 = next_below(s_scr[1], m2)
            rho_scr[...] += jnp.where(below2, 1.0, 0.0)
            t1_scr[pl.ds(i, 1), :] = m1
            t2_scr[pl.ds(i, 1), :] = m2
            return m1, m2

        _, last2 = lax.fori_loop(0, P_TOPK, extract, (top_row, top_row))
        s1 = s_scr[0]
        s2 = s_scr[1]
        t1 = t1_scr[...]
        t2 = t2_scr[...]
        cand_scr[0:P_TOPK] = t1[0:1] + t2
        for i in range(1, half):
            cand_scr[P_TOPK + half * (i - 1):P_TOPK + half * i] = t1[i:i + 1] + t2[0:half]
        cand_scr[N_CAND - half:N_CAND] = t1[half:P_TOPK] + t2[0:1]
        cmax = t1[0:1] + t2[0:1]

        def pick(i, st):
            m, z = st
            _, m = next_below(cand_scr[...], m)
            return m, z + jnp.exp(m - cmax)

        tau, zsum = lax.fori_loop(0, P_TOPK, pick, (top_row, jnp.zeros_like(cmax)))
        cnt_scr[...] = jnp.zeros_like(cnt_scr)

        def count(j, c):
            cnt_scr[...] += jnp.where(s_scr[0] + t2_scr[pl.ds(j, 1), :] >= tau, 1.0, 0.0)
            return c

        lax.fori_loop(0, P_TOPK, count, 0)
        sel1 = s1 >= t1[P_TOPK - 1:P_TOPK]
        sel2 = s2 >= last2
        rho = rho_scr[...] + jnp.where(s2 < last2, 1.0, 0.0)
        n_o[h] = _dup_bf16(jnp.where(sel1, cnt_scr[...], 0.0))
        e1_o[h] = _dup_bf16(jnp.where(sel1, jnp.exp(s1 - t1[0:1]), 0.0))
        rho_o[h] = rho.astype(BF16)
        e2_o[h] = jnp.where(sel2, jnp.exp(s2 - t2[0:1]) / zsum, 0.0).astype(BF16)
        return carry

    lax.fori_loop(0, P_HEADS, head, 0)


def _route(hm_t, wq_t, keys):
    n = hm_t.shape[1]
    tn = ROUTE_TOK
    big = pl.BlockSpec((P_HEADS, P_KEYS, tn), lambda i: (0, 0, i))
    sd = lambda dt: jax.ShapeDtypeStruct((P_HEADS, P_KEYS, n), dt)
    return pl.pallas_call(
        _route_kernel,
        grid=(n // tn,),
        in_specs=[pl.BlockSpec((D, tn), lambda i: (0, i)),
                  pl.BlockSpec((2 * P_HEADS * P_KEYS, D), lambda i: (0, 0)),
                  pl.BlockSpec((2 * P_HEADS, P_KEYS, P_KEYS), lambda i: (0, 0, 0))],
        out_specs=[big, big, big, big],
        out_shape=[sd(jnp.uint32), sd(jnp.uint32), sd(BF16), sd(BF16)],
        scratch_shapes=[pltpu.VMEM((2 * P_HEADS * P_KEYS, tn), BF16), pltpu.VMEM((2, P_KEYS, tn), F32),
                        pltpu.VMEM((P_TOPK, tn), F32), pltpu.VMEM((P_TOPK, tn), F32),
                        pltpu.VMEM((N_CAND, tn), F32),
                        pltpu.VMEM((P_KEYS, tn), F32), pltpu.VMEM((P_KEYS, tn), F32)],
        compiler_params=_cparams(("parallel",), 48),
        name="route",
    )(hm_t, wq_t, keys)


PEER_TOK = 512
PEER_EXP = 2048
PEER_SUB = 512


def _gelu(x):
    return 0.5 * x * (1.0 + lax.erf(x * (2.0 ** -0.5)))


BF16_ROWS = 16


def _peer_kernel(x_ref, u_ref, vt_ref, n_ref, e1_ref, rho_ref, e2_ref, h1_ref, m5_ref, fw_ref, o_ref, acc_ref):
    j = pl.program_id(1)

    @pl.when(j == 0)
    def _():
        acc_ref[...] = jnp.zeros_like(acc_ref)

    tn = x_ref.shape[1]
    x = x_ref[...]
    groups = P_KEYS // BF16_ROWS
    a_per_half = PEER_SUB // P_KEYS
    p_halves = []
    for half in range(PEER_EXP // PEER_SUB):
        rows_h = PEER_SUB
        act = _dot(u_ref[half * rows_h:(half + 1) * rows_h, :], x)
        gates = []
        for al in range(half * a_per_half, (half + 1) * a_per_half):
            g = [None] * groups
            for h in range(P_HEADS):
                row = lambda ref: pltpu.bitcast(jnp.broadcast_to(ref[h, al:al + 1, :], (8, tn)), BF16)
                n_row = row(n_ref)
                e1_row = row(e1_ref)
                for q in range(groups):
                    rows = slice(q * BF16_ROWS, (q + 1) * BF16_ROWS)
                    term = jnp.where(rho_ref[h, rows, :] < n_row, e2_ref[h, rows, :], 0.0) * e1_row
                    g[q] = term if g[q] is None else g[q] + term
            gates.extend(g)
        p_halves.append(_gelu(act).astype(BF16) * jnp.concatenate(gates, axis=0))
    acc_ref[...] += _dot(vt_ref[...], jnp.concatenate(p_halves, axis=0))

    @pl.when(j == pl.num_programs(1) - 1)
    def _():
        h = h1_ref[...] + m5_ref[0] * acc_ref[...].T
        o_ref[...] = h * lax.rsqrt(jnp.mean(h * h, axis=-1, keepdims=True) + EPS) * fw_ref[...]


def _peer(hm_t, u_b, vt_b, n_cnt, e1, rho, e2, h1, m5, final_w, seq):
    n = hm_t.shape[1]
    n_exp = u_b.shape[0]
    tn, te = PEER_TOK, PEER_EXP
    big = pl.BlockSpec((P_HEADS, P_KEYS, tn), lambda i, j: (0, 0, i))
    a_rows = pl.BlockSpec((P_HEADS, te // P_KEYS, tn), lambda i, j: (0, j, i))
    tok = pl.BlockSpec((tn, D), lambda i, j: (i, 0))
    return pl.pallas_call(
        _peer_kernel,
        grid=(n // tn, n_exp // te),
        in_specs=[pl.BlockSpec((D, tn), lambda i, j: (0, i)),
                  pl.BlockSpec((te, D), lambda i, j: (j, 0)),
                  pl.BlockSpec((D, te), lambda i, j: (0, j)),
                  a_rows, a_rows, big, big,
                  tok, pl.BlockSpec((1, 1, D), lambda i, j: (i // (seq // tn), 0, 0)),
                  pl.BlockSpec((1, D), lambda i, j: (0, 0))],
        out_specs=tok,
        out_shape=jax.ShapeDtypeStruct((n, D), F32),
        scratch_shapes=[pltpu.VMEM((D, tn), F32)],
        compiler_params=_cparams(("parallel", "arbitrary"), 52),
        name="peer",
    )(hm_t, u_b, vt_b, n_cnt, e1, rho, e2, h1, m5, final_w)


def _reorder_cols(w):
    gla_main, gla_lora = w[..., 0:3072], w[..., 3072:3104]
    rw_main, rw_lora = w[..., 3104:6176], w[..., 6176:6528]
    gates = w[..., 6528:8576]
    pad = jnp.zeros(w.shape[:-1] + (512 - 32 - 352,), w.dtype)
    return jnp.concatenate([rw_main, gla_main, gates, gla_lora, rw_lora, pad], axis=-1)


def _hi_lo(w):
    hi = w.astype(BF16)
    return jnp.stack([hi, (w - hi.astype(F32)).astype(BF16)])


def _to_chains(a):
    b, s, t, _ = a.shape
    return a.reshape(b, s, t, R_HEADS, R_N).transpose(1, 2, 4, 0, 3).reshape(s, t, R_N, b * R_HEADS)


def _from_chains(a, b):
    t = a.shape[0]
    return a.reshape(t, R_N, b, R_HEADS).transpose(2, 0, 3, 1).reshape(b, t, D)


def kernel(x, c, ctx, c_ctx, norm1_w, w_mod, b_mod, w_in, gla_w_a2, gla_b_a, gla_norm_w, rwkv_mu, rwkv_w0, rwkv_w2, rwkv_a0, rwkv_a2, rwkv_g2, rwkv_k_k, rwkv_k_a, rwkv_r_k, rwkv_ln_w, rwkv_ln_b, w_out, norm2_w, peer_w_q, peer_sub_keys, peer_u, peer_v, final_norm_w):
    b, seq, _ = x.shape
    assert w_in.shape[0] == 1 and ctx.shape[1] == N_CTX and seq % ROWS == 0
    row = lambda v: v.reshape(1, -1)

    c16 = jnp.zeros((16, D), F32).at[:b].set(c).at[b].set(c_ctx)
    m = _mod(c16, w_mod[0], row(b_mod[0])).reshape(16, N_MOD, D)
    m_lat, m_ctx = m[:b], m[b]
    per_b = lambda i: m_lat[:, i].reshape(b, 1, D)
    shift1 = jnp.stack([jnp.broadcast_to(m_ctx[0], (b, D)), m_lat[:, 0]], axis=1).reshape(b, 2, 1, D)
    scale1 = jnp.stack([jnp.broadcast_to(m_ctx[1], (b, D)), m_lat[:, 1]], axis=1).reshape(b, 2, 1, D)

    h_all = jnp.concatenate([ctx, x], axis=1)
    z, z_lora = _inproj(h_all, row(norm1_w[0]), scale1, shift1, _reorder_cols(w_in[0]).astype(BF16))

    mu = rwkv_mu[0]
    mu_lora = jnp.concatenate([jnp.zeros((32,), F32), mu[3072:], jnp.zeros((128,), F32)]).reshape(1, 512)
    e = jnp.repeat(jnp.eye(R_HEADS, dtype=BF16), R_N, axis=0)
    w2f = _hi_lo(jnp.zeros((128, D), F32).at[L_WF:L_WF + R_W_LORA].set(rwkv_w2[0, 0]))
    w2b = _hi_lo(jnp.zeros((256, D), F32).at[L_WB:L_WB + R_W_LORA].set(rwkv_w2[0, 1]))
    a2 = _hi_lo(jnp.zeros((128, D), F32).at[L_A - 128:L_A - 128 + R_A_LORA].set(rwkv_a2[0]))
    g2 = _hi_lo(jnp.zeros((256, D), F32).at[L_G - 128:L_G - 128 + R_G_LORA].set(rwkv_g2[0]))
    wa = jnp.zeros((2, 128, G_HEADS * G_DK), F32)
    wa = wa.at[0, 0:G_LORA].set(gla_w_a2[0, 0]).at[1, G_LORA:2 * G_LORA].set(gla_w_a2[0, 1])
    wa = jnp.stack([_hi_lo(wa[0]), _hi_lo(wa[1])])
    seq_ops, bonus, gate_r = _rprep(
        z, z_lora, row(mu[:3072]), mu_lora, rwkv_w0[0].reshape(2, 1, D), w2f, w2b, row(rwkv_a0[0]), a2,
        g2, row(rwkv_k_k[0]), row(rwkv_k_a[0]), row(rwkv_r_k[0]), e, e.T)

    o_r = _rwkv(_to_chains(seq_ops))
    or_f, or_b = _from_chains(o_r[0], b), _from_chains(o_r[1], b)

    og_f, og_b = _gla(z, z_lora, wa, gla_b_a[0].reshape(2, 1, -1))

    h1, hm_b = _mix(og_f, og_b, z, or_f, or_b, bonus, gate_r, x, row(gla_norm_w[0]), row(rwkv_ln_w[0]),
                        row(rwkv_ln_b[0]), e, e.T, w_out[0].astype(BF16), per_b(2), row(norm2_w[0]),
                        per_b(3), per_b(4))

    n_tok = b * seq
    hm_t = hm_b.reshape(n_tok, D).T
    n_cnt, e1, rho, e2 = _route(hm_t, peer_w_q[0].T.astype(BF16),
                                peer_sub_keys[0].reshape(2 * P_HEADS, P_KEYS, -1).astype(BF16))
    assert seq % PEER_TOK == 0
    out = _peer(hm_t, peer_u[0].astype(BF16), peer_v[0].T.astype(BF16), n_cnt, e1, rho, e2,
                h1.reshape(n_tok, D), per_b(5), row(final_norm_w), seq)
    return out.reshape(b, seq, D)
```

```python
import jax
import jax.numpy as jnp
from jax import lax
from jax.experimental import pallas as pl
from jax.experimental.pallas import tpu as pltpu

F32 = jnp.float32
BF16 = jnp.bfloat16

D = 1024
GRID_COLS = 64
N_CTX = 256
EPS = 1e-6
N_MOD = 6

G_HEADS = 4
G_DV = D // G_HEADS
G_DK = G_DV // 2
G_LORA = 16
G_GATE_NORM = 16.0
G_CHUNK = 64

R_N = 64
R_HEADS = D // R_N
R_LN_EPS = 64e-5
R_W_LORA = 64
R_A_LORA = 64
R_G_LORA = 160

P_HEADS = 8
P_KEYS = 128
P_TOPK = 16

C_RWKV = 0
C_GLA = 3072
C_GATE = 6144
C_LORA = 8192
LORA_W = 512
N_PROJ = C_LORA + LORA_W
L_WF, L_WB, L_A, L_G = 32, 96, 160, 224

ROWS = 256
V7X_VMEM_BYTES = 64 * 1024 * 1024


def _cparams(sem, vmem_mb):
    return pltpu.CompilerParams(dimension_semantics=sem,
                                vmem_limit_bytes=min(vmem_mb * 1024 * 1024, V7X_VMEM_BYTES - (8 << 20)))


def _sigmoid(x):
    return 1.0 / (1.0 + jnp.exp(-x))


def _softplus(x):
    return jnp.maximum(x, 0.0) + jnp.log(1.0 + jnp.exp(-jnp.abs(x)))


def _dot(a, b):
    return jnp.dot(a, b, preferred_element_type=F32)


def _dot_nt(a, b):
    return lax.dot_general(a, b, (((1,), (1,)), ((), ())), preferred_element_type=F32)


def _split2(x):
    hi = x.astype(BF16)
    return hi, (x - hi.astype(F32)).astype(BF16)


def _dot_x2(x, w):
    hi, lo = _split2(x)
    return _dot(hi, w) + _dot(lo, w)


def _dot_2x(w, x):
    hi, lo = _split2(x)
    return _dot(w, hi) + _dot(w, lo)


def _dot_x3(x, w_ref):
    hi, lo = _split2(x)
    return _dot(hi, w_ref[0]) + (_dot(lo, w_ref[0]) + _dot(hi, w_ref[1]))


def _mod_kernel(c_ref, w_ref, b_ref, o_ref):
    c = c_ref[...]
    s = c * _sigmoid(c)
    o_ref[...] = _dot(s.astype(BF16), w_ref[...].astype(BF16)) + b_ref[...]


def _mod(c16, w_mod, b_mod):
    n = w_mod.shape[1]
    tn = 768
    return pl.pallas_call(
        _mod_kernel,
        grid=(n // tn,),
        in_specs=[pl.BlockSpec((16, D), lambda j: (0, 0)),
                  pl.BlockSpec((D, tn), lambda j: (0, j)),
                  pl.BlockSpec((1, tn), lambda j: (0, j))],
        out_specs=pl.BlockSpec((16, tn), lambda j: (0, j)),
        out_shape=jax.ShapeDtypeStruct((16, n), F32),
        compiler_params=_cparams(("parallel",), 24),
        name="mod",
    )(c16, w_mod, b_mod)


INPROJ_ROWS = 2304


def _inproj_kernel(h_ref, nw_ref, sc_ref, sh_ref, w_ref, o_ref, ol_ref, a_scr):
    n = pl.program_id(2)

    @pl.when(n == 0)
    def _():
        x = h_ref[0]
        y = x * lax.rsqrt(jnp.mean(x * x, axis=-1, keepdims=True) + EPS) * nw_ref[...]
        t = pl.program_id(1) * INPROJ_ROWS + lax.broadcasted_iota(jnp.int32, (INPROJ_ROWS, 1), 0)
        is_ctx = t < N_CTX
        sc = jnp.where(is_ctx, sc_ref[0, 0], sc_ref[0, 1])
        sh = jnp.where(is_ctx, sh_ref[0, 0], sh_ref[0, 1])
        a_scr[...] = (y * (1.0 + sc) + sh).astype(BF16)

    acc = _dot(a_scr[...], w_ref[...])
    n_main = C_LORA // LORA_W

    @pl.when(n < n_main)
    def _():
        o_ref[0] = acc.astype(BF16)

    @pl.when(n == n_main)
    def _():
        ol_ref[0] = acc


def _inproj(h_all, norm_w, scale, shift, w_p):
    b, t, _ = h_all.shape
    tn = LORA_W
    n_main = C_LORA // tn
    mod = pl.BlockSpec((1, 2, 1, D), lambda i, j, n: (i, 0, 0, 0))
    return pl.pallas_call(
        _inproj_kernel,
        grid=(b, t // INPROJ_ROWS, N_PROJ // tn),
        in_specs=[pl.BlockSpec((1, INPROJ_ROWS, D), lambda i, j, n: (i, j, 0)),
                  pl.BlockSpec((1, D), lambda i, j, n: (0, 0)),
                  mod, mod,
                  pl.BlockSpec((D, tn), lambda i, j, n: (0, n))],
        out_specs=[pl.BlockSpec((1, INPROJ_ROWS, tn), lambda i, j, n: (i, j, jnp.minimum(n, n_main - 1))),
                   pl.BlockSpec((1, INPROJ_ROWS, tn), lambda i, j, n: (i, j, 0))],
        out_shape=[jax.ShapeDtypeStruct((b, t, C_LORA), BF16), jax.ShapeDtypeStruct((b, t, tn), F32)],
        scratch_shapes=[pltpu.VMEM((INPROJ_ROWS, D), BF16)],
        compiler_params=_cparams(("parallel", "parallel", "arbitrary"), 52),
        name="inproj",
    )(h_all, norm_w, scale, shift, w_p)


def _shift_mix(zc, zp, zn, mu, is_ctx):
    rows, w = zc.shape
    row = lax.broadcasted_iota(jnp.int32, (rows, w), 0)
    lane = lax.broadcasted_iota(jnp.int32, (rows, w), 1)
    pmask = jnp.where(is_ctx, 1, 3)
    rmask = jnp.where(is_ctx, rows - 1, GRID_COLS - 1)
    cls = lane & pmask
    rr = row & rmask
    prev = pltpu.roll(zc, 1, 0)
    nxt = pltpu.roll(zc, rows - 1, 0)
    up = jnp.concatenate([zp, zc[:rows - GRID_COLS]], axis=0)
    down = jnp.concatenate([zc[GRID_COLS:], zn], axis=0)
    zs = jnp.where(cls == 0, jnp.where(rr != 0, prev, 0.0),
                   jnp.where(cls == 1, jnp.where(rr != rmask, nxt, 0.0),
                             jnp.where(cls == 2, up, down)))
    return zc + (zs - zc) * mu


def _rprep_kernel(zc_ref, zp_ref, zn_ref, lc_ref, lp_ref, ln_ref,
                  mu_ref, mul_ref, w0_ref, w2f_ref, w2b_ref, a0_ref, a2_ref, g2_ref,
                  kk_w_ref, ka_w_ref, rk_w_ref, e_ref, et_ref,
                  r_o, k_o, v_o, kk_o, kka_o, df_o, db_o, bonus_o, gate_o):
    tb = pl.program_id(1)
    ntb = pl.num_programs(1)
    is_ctx = tb == 0
    has_up = (tb > 1).astype(F32)
    has_down = (tb < ntb - 1).astype(F32)

    zl = _shift_mix(lc_ref[0], lp_ref[0] * has_up, ln_ref[0] * has_down, mul_ref[...], is_ctx)
    def decay(window, w2p_ref, d):
        w = -_softplus(-(w0_ref[d] + _dot_x3(jnp.tanh(window), w2p_ref))) - 0.5
        return -jnp.exp(w)

    df_o[0] = decay(zl[:, 0:128], w2f_ref, 0).astype(BF16)
    db_o[0] = decay(zl[:, 0:256], w2b_ref, 1).astype(BF16)
    a = _sigmoid(a0_ref[...] + _dot_x3(zl[:, 128:256], a2_ref))
    gate_o[0] = _dot_x3(_sigmoid(zl[:, 128:384]), g2_ref).astype(BF16)

    def piece(i):
        sl = slice(i * D, (i + 1) * D)
        return _shift_mix(zc_ref[0, :, sl].astype(F32), zp_ref[0, :, sl].astype(F32) * has_up,
                          zn_ref[0, :, sl].astype(F32) * has_down, mu_ref[:, sl], is_ctx)

    r = piece(0)
    kr = piece(1)
    vr = piece(2)
    e = e_ref[...]
    et = et_ref[...]
    kkraw = kr * kk_w_ref[...]
    nrm = jnp.sqrt(_dot_x2(kkraw * kkraw, e))
    inv = 1.0 / jnp.maximum(nrm, 1e-12)
    kk = kkraw * _dot_x2(inv, et)
    k2 = kr * (1.0 + (a - 1.0) * ka_w_ref[...])
    rk = _dot_x2(r * k2 * rk_w_ref[...], e)
    r_o[0] = r.astype(BF16)
    k_o[0] = k2.astype(BF16)
    v_o[0] = vr.astype(BF16)
    kk_o[0] = kk.astype(BF16)
    kka_o[0] = (kk * a).astype(BF16)
    bonus_o[0] = (_dot_x2(rk, et) * vr).astype(BF16)


def _rprep(z, zl, mu_main, mu_lora, w0, w2f, w2b, a0, a2, g2, k_k, k_a, r_k, e, et):
    b, t, _ = z.shape
    ntb = t // ROWS
    q = ROWS // GRID_COLS
    last64 = t // GRID_COLS - 1
    wm = 3 * D
    cm = C_RWKV // wm
    cl = 0
    cur = lambda c: (lambda i, j: (i, j, c))
    prv = lambda c: (lambda i, j: (i, jnp.maximum(j * q - 1, 0), c))
    nxt = lambda c: (lambda i, j: (i, jnp.minimum(j * q + q, last64), c))
    full = lambda *s: pl.BlockSpec(s, lambda i, j: (0,) * len(s))
    out_spec = pl.BlockSpec((1, ROWS, D), lambda i, j: (i, j, 0))
    return pl.pallas_call(
        _rprep_kernel,
        grid=(b, ntb),
        in_specs=[pl.BlockSpec((1, ROWS, wm), cur(cm)),
                  pl.BlockSpec((1, GRID_COLS, wm), prv(cm)),
                  pl.BlockSpec((1, GRID_COLS, wm), nxt(cm)),
                  pl.BlockSpec((1, ROWS, 512), cur(cl)),
                  pl.BlockSpec((1, GRID_COLS, 512), prv(cl)),
                  pl.BlockSpec((1, GRID_COLS, 512), nxt(cl)),
                  full(1, wm), full(1, 512), full(2, 1, D), full(2, 128, D), full(2, 256, D),
                  full(1, D), full(2, 128, D), full(2, 256, D),
                  full(1, D), full(1, D), full(1, D), full(D, R_HEADS), full(R_HEADS, D)],
        out_specs=[out_spec] * 9,
        out_shape=[jax.ShapeDtypeStruct((b, t, D), BF16)] * 9,
        compiler_params=_cparams(("parallel", "parallel"), 52),
        name="rprep",
    )(z, z, z, zl, zl, zl, mu_main, mu_lora, w0, w2f, w2b, a0, a2, g2, k_k, k_a, r_k, e, et)


R_STEPS = 64


def _rwkv_kernel(r_ref, k_ref, v_ref, kk_ref, kka_ref, lwf_ref, lwb_ref, o_ref, s_ref, rows_ref):
    dirn = pl.program_id(0)
    tix = lambda i: jnp.where(dirn == 0, i, R_STEPS - 1 - i)

    @pl.when(pl.program_id(1) == 0)
    def _():
        s_ref[...] = jnp.zeros_like(s_ref)

    def weighted_sum(rows):
        acc = s_ref[0] * rows_ref[rows, 0:1, :]
        for k in range(1, R_N):
            acc = acc + s_ref[k] * rows_ref[rows, k:k + 1, :]
        return acc

    rows_ref[0] = kk_ref[tix(0)].astype(F32)
    sa0 = weighted_sum(0)

    def run(with_out):
        def step(i, carry):
            sa, lcum = carry
            t = tix(i)
            t_next = tix(jnp.minimum(i + 1, R_STEPS - 1))
            lcum = lcum + jnp.where(dirn == 0, lwf_ref[t], lwb_ref[t]).astype(F32)
            e_pos = jnp.exp(lcum)
            e_neg = jnp.exp(-lcum)
            rows_ref[0] = kk_ref[t_next].astype(F32) * e_pos
            rows_ref[1] = kka_ref[t].astype(F32) * e_neg
            rows_ref[2] = k_ref[t].astype(F32) * e_neg
            if with_out:
                rows_ref[3] = r_ref[t].astype(F32) * e_pos
            vv = v_ref[t].astype(F32)
            out = None
            sa_next = None
            for k in range(R_N):
                sk = s_ref[k] - sa * rows_ref[1, k:k + 1, :] + vv * rows_ref[2, k:k + 1, :]
                s_ref[k] = sk
                s_term = sk * rows_ref[0, k:k + 1, :]
                sa_next = s_term if sa_next is None else sa_next + s_term
                if with_out:
                    o_term = sk * rows_ref[3, k:k + 1, :]
                    out = o_term if out is None else out + o_term
            o_ref[0, t] = out.astype(BF16) if with_out else jnp.zeros(o_ref.shape[2:], BF16)
            return sa_next, lcum

        return lax.fori_loop(0, R_STEPS, step, (sa0, jnp.zeros((R_N, s_ref.shape[2]), F32)))[1]

    lcum = lax.cond(pl.program_id(1) < N_CTX // R_STEPS, lambda: run(False), lambda: run(True))
    rows_ref[0] = jnp.exp(lcum)
    for k in range(R_N):
        s_ref[k] = s_ref[k] * rows_ref[0, k:k + 1, :]


def _rwkv_order(dirn, nb, n_ctx_blk, n_blk):
    back = jnp.where(nb < n_ctx_blk, n_ctx_blk - 1 - nb, n_blk + n_ctx_blk - 1 - nb)
    return jnp.where(dirn == 0, nb, back)


def _rwkv(r_t, k_t, v_t, kk_t, kka_t, lwf_t, lwb_t):
    t, _, nch = r_t.shape
    n_blk = t // R_STEPS
    n_ctx_blk = N_CTX // R_STEPS
    blk = lambda d, n: (_rwkv_order(d, n, n_ctx_blk, n_blk), 0, 0)
    blk4 = lambda d, n: (d, _rwkv_order(d, n, n_ctx_blk, n_blk), 0, 0)
    only = lambda own: (lambda d, n: (jnp.where(d == own, _rwkv_order(d, n, n_ctx_blk, n_blk), 0), 0, 0))
    spec = pl.BlockSpec((R_STEPS, R_N, nch), blk)
    spec4 = pl.BlockSpec((1, R_STEPS, R_N, nch), blk4)
    return pl.pallas_call(
        _rwkv_kernel,
        grid=(2, n_blk),
        in_specs=[spec, spec, spec, spec, spec,
                  pl.BlockSpec((R_STEPS, R_N, nch), only(0)), pl.BlockSpec((R_STEPS, R_N, nch), only(1))],
        out_specs=spec4,
        out_shape=jax.ShapeDtypeStruct((2, t, R_N, nch), BF16),
        scratch_shapes=[pltpu.VMEM((R_N, R_N, nch), F32), pltpu.VMEM((4, R_N, nch), F32)],
        compiler_params=_cparams(("arbitrary", "arbitrary"), 48),
        name="rwkv",
    )(r_t, k_t, v_t, kk_t, kka_t, lwf_t, lwb_t)


def _log_sigmoid(x):
    return jnp.minimum(x, 0.0) - jnp.log(1.0 + jnp.exp(-jnp.abs(x)))


def _gla_dir(q, k, v, alo, wa, ba, st_ref, reverse, need_out):
    c = q.shape[0]
    q = q.astype(F32)
    k = k.astype(F32)
    g = _log_sigmoid(_dot_x3(alo, wa) + ba) * (1.0 / G_GATE_NORM)
    row = lax.broadcasted_iota(jnp.int32, (c, c), 0)
    col = lax.broadcasted_iota(jnp.int32, (c, c), 1)
    tri = (row <= col) if reverse else (row >= col)
    bcum = _dot_2x(tri.astype(BF16), g)
    blast = bcum[0:1] if reverse else bcum[c - 1:c]
    pad_rows = lambda a: jnp.concatenate([a, jnp.zeros_like(a)], axis=0)
    kl = pad_rows((k * jnp.exp(blast - bcum)).astype(BF16))
    eb = jnp.exp(blast)
    vt = pad_rows(v.astype(F32)).T.astype(BF16)
    if need_out:
        qd = (q * (G_DK ** -0.5) * jnp.exp(bcum)).astype(BF16)
        kd = pad_rows((k * jnp.exp(-bcum)).astype(BF16))
        row2 = lax.broadcasted_iota(jnp.int32, (c, 2 * c), 0)
        col2 = lax.broadcasted_iota(jnp.int32, (c, 2 * c), 1)
        tri2 = (row2 <= col2) if reverse else (row2 >= col2)
    outs = []
    for h in range(G_HEADS):
        sk = slice(h * G_DK, (h + 1) * G_DK)
        st = st_ref[h]
        vth = vt[h * G_DV:(h + 1) * G_DV, :]
        if need_out:
            att = jnp.where(tri2, _dot_nt(qd[:, sk], kd[:, sk]), 0.0).astype(BF16)
            outs.append(_dot_nt(jnp.concatenate([qd[:, sk], att], axis=1),
                                jnp.concatenate([st.astype(BF16), vth], axis=1)))
        st_ref[h] = st * eb[:, sk] + _dot(vth, kl[:, sk])
    return jnp.concatenate(outs, axis=-1).astype(BF16) if need_out else jnp.zeros(v.shape, BF16)


GLA_BATCH = 4


def _gla_kernel(qf, kf, vf, lf, qb, kb, vb, lb, wa_ref, ba_ref, of_ref, ob_ref, sf_ref, sb_ref):
    @pl.when(pl.program_id(1) == 0)
    def _():
        sf_ref[...] = jnp.zeros_like(sf_ref)
        sb_ref[...] = jnp.zeros_like(sb_ref)

    def run(need_out):
        for s in range(GLA_BATCH):
            of_ref[s] = _gla_dir(qf[s], kf[s], vf[s], lf[s], wa_ref.at[0], ba_ref[0], sf_ref.at[s], False, need_out)
            ob_ref[s] = _gla_dir(qb[s], kb[s], vb[s], lb[s], wa_ref.at[1], ba_ref[1], sb_ref.at[s], True, need_out)

    is_ctx = pl.program_id(1) < N_CTX // G_CHUNK
    pl.when(is_ctx)(lambda: run(False))
    pl.when(jnp.logical_not(is_ctx))(lambda: run(True))


def _gla(z, zl, w_a2, b_a):
    b, t, _ = z.shape
    n_blk = t // G_CHUNK
    n_ctx_blk = N_CTX // G_CHUNK
    fwd = lambda n: n
    bwd = lambda n: _rwkv_order(1, n, n_ctx_blk, n_blk)
    wk = G_HEADS * G_DK
    cq, ck, cv = C_GLA // wk, (C_GLA + wk) // wk, (C_GLA + 2 * wk) // D
    cl = 0

    nb = GLA_BATCH
    assert b % nb == 0

    def specs(order):
        return [pl.BlockSpec((nb, G_CHUNK, wk), lambda i, n: (i, order(n), cq)),
                pl.BlockSpec((nb, G_CHUNK, wk), lambda i, n: (i, order(n), ck)),
                pl.BlockSpec((nb, G_CHUNK, D), lambda i, n: (i, order(n), cv)),
                pl.BlockSpec((nb, G_CHUNK, 128), lambda i, n: (i, order(n), cl))]

    out_sd = jax.ShapeDtypeStruct((b, t, D), BF16)
    state = pltpu.VMEM((nb, G_HEADS, G_DV, G_DK), F32)
    return pl.pallas_call(
        _gla_kernel,
        grid=(b // nb, n_blk),
        in_specs=specs(fwd) + specs(bwd) + [
            pl.BlockSpec((2, 2, 128, wk), lambda i, n: (0, 0, 0, 0)),
            pl.BlockSpec((2, 1, wk), lambda i, n: (0, 0, 0))],
        out_specs=[pl.BlockSpec((nb, G_CHUNK, D), lambda i, n: (i, fwd(n), 0)),
                   pl.BlockSpec((nb, G_CHUNK, D), lambda i, n: (i, bwd(n), 0))],
        out_shape=[out_sd, out_sd],
        scratch_shapes=[state, state],
        compiler_params=_cparams(("parallel", "arbitrary"), 24),
        name="gla",
    )(z, z, z, zl, z, z, z, zl, w_a2, b_a)


def _mix_kernel(ogf_ref, ogb_ref, gout_ref, orf_ref, orb_ref, bonus_ref, gr_ref, gg_ref, grw_ref, x_ref,
                gnw_ref, lnw_ref, lnb_ref, e_ref, et_ref, wout_ref, m2_ref, n2w_ref, m3_ref, m4_ref,
                h1_o, hmb_o):
    og = ogf_ref[0].astype(F32) + ogb_ref[0].astype(F32)
    parts = []
    for h in range(G_HEADS):
        seg = og[:, h * G_DV:(h + 1) * G_DV]
        parts.append(seg * lax.rsqrt(jnp.mean(seg * seg, axis=-1, keepdims=True) + EPS))
    gout = gout_ref[0].astype(F32)
    y_gla = jnp.concatenate(parts, axis=-1) * gnw_ref[...] * (gout * _sigmoid(gout))

    e = e_ref[...]
    et = et_ref[...]
    o_r = orf_ref[0].astype(F32) + orb_ref[0].astype(F32)
    mu = _dot_x2(_dot_x2(o_r, e) * (1.0 / R_N), et)
    dlt = o_r - mu
    var = _dot_x2(dlt * dlt, e) * (1.0 / R_N)
    o_r = (dlt * _dot_x2(lax.rsqrt(var + R_LN_EPS), et) * lnw_ref[...] + lnb_ref[...]
           + bonus_ref[0].astype(F32))
    y_rwkv = o_r * gr_ref[0].astype(F32)

    y = _sigmoid(gg_ref[0].astype(F32)) * y_gla + _sigmoid(grw_ref[0].astype(F32)) * y_rwkv
    h1 = x_ref[0] + m2_ref[0] * _dot(y.astype(BF16), wout_ref[...])
    h1_o[0] = h1
    hn = h1 * lax.rsqrt(jnp.mean(h1 * h1, axis=-1, keepdims=True) + EPS) * n2w_ref[...]
    hmb_o[0] = (hn * (1.0 + m4_ref[0]) + m3_ref[0]).astype(BF16)


def _mix(og_f, og_b, z, or_f, or_b, bonus, gate_r, x, gnw, lnw, lnb, e, et, w_out, m2, n2w, m3, m4):
    b, tl, _ = x.shape
    off = N_CTX // ROWS
    seq = lambda c: pl.BlockSpec((1, ROWS, D), lambda i, j: (i, j + off, c))
    full = lambda *s: pl.BlockSpec(s, lambda i, j: (0,) * len(s))
    per_b = pl.BlockSpec((1, 1, D), lambda i, j: (i, 0, 0))
    lat = pl.BlockSpec((1, ROWS, D), lambda i, j: (i, j, 0))
    return pl.pallas_call(
        _mix_kernel,
        grid=(b, tl // ROWS),
        in_specs=[seq(0), seq(0), seq((C_GLA + 2048) // D), seq(0), seq(0), seq(0), seq(0),
                  seq(C_GATE // D), seq(C_GATE // D + 1), lat,
                  full(1, D), full(1, D), full(1, D), full(D, R_HEADS), full(R_HEADS, D), full(D, D),
                  per_b, full(1, D), per_b, per_b],
        out_specs=[lat, lat],
        out_shape=[jax.ShapeDtypeStruct((b, tl, D), F32), jax.ShapeDtypeStruct((b, tl, D), BF16)],
        compiler_params=_cparams(("parallel", "parallel"), 52),
        name="mix",
    )(og_f, og_b, z, or_f, or_b, bonus, gate_r, z, z, x, gnw, lnw, lnb, e, et, w_out, m2, n2w, m3, m4)


ROUTE_TOK = 1024


N_CAND = 80


def _dup_bf16(x):
    hi = lax.bitcast_convert_type(x.astype(BF16).astype(F32), jnp.uint32)
    return hi | (hi >> 16)


def _route_kernel(x_ref, wq_ref, keys_ref, n_o, e1_o, rho_o, e2_o,
                  q_scr, s_scr, t1_scr, t2_scr, cand_scr, rho_scr, cnt_scr):
    q_scr[...] = _dot(wq_ref[...], x_ref[...]).astype(BF16)
    ninf = -jnp.inf
    half = P_TOPK // 2
    tn = x_ref.shape[1]
    top_row = jnp.full((1, tn), jnp.inf, F32)

    def next_below(vals, bound):
        below = vals < bound
        return below, jnp.max(jnp.where(below, vals, ninf), axis=0, keepdims=True)

    def head(h, carry):
        q1 = q_scr[pl.ds(pl.multiple_of(h * 2 * P_KEYS, P_KEYS), P_KEYS), :]
        q2 = q_scr[pl.ds(pl.multiple_of(h * 2 * P_KEYS + P_KEYS, P_KEYS), P_KEYS), :]
        s_scr[0] = _dot(keys_ref[2 * h], q1)
        s_scr[1] = _dot(keys_ref[2 * h + 1], q2)
        rho_scr[...] = jnp.full(rho_scr.shape, -1.0, F32)

        def extract(i, bounds):
            m1, m2 = bounds
            _, m1 = next_below(s_scr[0], m1)
            below2, m2 = next_below(s_scr[1], m2)
            rho_scr[...] += jnp.where(below2, 1.0, 0.0)
            t1_scr[pl.ds(i, 1), :] = m1
            t2_scr[pl.ds(i, 1), :] = m2
            return m1, m2

        _, last2 = lax.fori_loop(0, P_TOPK, extract, (top_row, top_row))
        s1 = s_scr[0]
        s2 = s_scr[1]
        t1 = t1_scr[...]
        t2 = t2_scr[...]
        cand_scr[0:P_TOPK] = t1[0:1] + t2
        for i in range(1, half):
            cand_scr[P_TOPK + half * (i - 1):P_TOPK + half * i] = t1[i:i + 1] + t2[0:half]
        cand_scr[N_CAND - half:N_CAND] = t1[half:P_TOPK] + t2[0:1]
        cmax = t1[0:1] + t2[0:1]

        def pick(i, st):
            m, z = st
            _, m = next_below(cand_scr[...], m)
            return m, z + jnp.exp(m - cmax)

        tau, zsum = lax.fori_loop(0, P_TOPK, pick, (top_row, jnp.zeros_like(cmax)))
        cnt_scr[...] = jnp.zeros_like(cnt_scr)

        def count(j, c):
            cnt_scr[...] += jnp.where(s_scr[0] + t2_scr[pl.ds(j, 1), :] >= tau, 1.0, 0.0)
            return c

        lax.fori_loop(0, P_TOPK, count, 0)
        sel1 = s1 >= t1[P_TOPK - 1:P_TOPK]
        sel2 = s2 >= last2
        rho = rho_scr[...] + jnp.where(s2 < last2, 1.0, 0.0)
        n_o[h] = _dup_bf16(jnp.where(sel1, cnt_scr[...], 0.0))
        e1_o[h] = _dup_bf16(jnp.where(sel1, jnp.exp(s1 - t1[0:1]), 0.0))
        rho_o[h] = rho.astype(BF16)
        e2_o[h] = jnp.where(sel2, jnp.exp(s2 - t2[0:1]) * (RSQRT2 / zsum), 0.0).astype(BF16)
        return carry

    lax.fori_loop(0, P_HEADS, head, 0)


def _route(hm_t, wq_t, keys):
    n = hm_t.shape[1]
    tn = ROUTE_TOK
    big = pl.BlockSpec((P_HEADS, P_KEYS, tn), lambda i: (0, 0, i))
    sd = lambda dt: jax.ShapeDtypeStruct((P_HEADS, P_KEYS, n), dt)
    return pl.pallas_call(
        _route_kernel,
        grid=(n // tn,),
        in_specs=[pl.BlockSpec((D, tn), lambda i: (0, i)),
                  pl.BlockSpec((2 * P_HEADS * P_KEYS, D), lambda i: (0, 0)),
                  pl.BlockSpec((2 * P_HEADS, P_KEYS, P_KEYS), lambda i: (0, 0, 0))],
        out_specs=[big, big, big, big],
        out_shape=[sd(jnp.uint32), sd(jnp.uint32), sd(BF16), sd(BF16)],
        scratch_shapes=[pltpu.VMEM((2 * P_HEADS * P_KEYS, tn), BF16), pltpu.VMEM((2, P_KEYS, tn), F32),
                        pltpu.VMEM((P_TOPK, tn), F32), pltpu.VMEM((P_TOPK, tn), F32),
                        pltpu.VMEM((N_CAND, tn), F32),
                        pltpu.VMEM((P_KEYS, tn), F32), pltpu.VMEM((P_KEYS, tn), F32)],
        compiler_params=_cparams(("parallel",), 48),
        name="route",
    )(hm_t, wq_t, keys)


PEER_TOK = 512
PEER_EXP = 2048
PEER_SUB = 512


RSQRT2 = 2.0 ** -0.5


def _gelu_of_scaled(y):
    return y * (1.0 + lax.erf(y))


BF16_ROWS = 16


def _peer_kernel(x_ref, u_ref, vt_ref, n_ref, e1_ref, rho_ref, e2_ref, h1_ref, m5_ref, fw_ref, o_ref, acc_ref):
    j = pl.program_id(1)

    @pl.when(j == 0)
    def _():
        acc_ref[...] = jnp.zeros_like(acc_ref)

    tn = x_ref.shape[1]
    x = x_ref[...]
    groups = P_KEYS // BF16_ROWS
    a_per_half = PEER_SUB // P_KEYS
    p_halves = []
    for half in range(PEER_EXP // PEER_SUB):
        rows_h = PEER_SUB
        act = _dot(u_ref[half * rows_h:(half + 1) * rows_h, :], x)
        gates = []
        for al in range(half * a_per_half, (half + 1) * a_per_half):
            g = [None] * groups
            for h in range(P_HEADS):
                row = lambda ref: pltpu.bitcast(jnp.broadcast_to(ref[h, al:al + 1, :], (8, tn)), BF16)
                n_row = row(n_ref)
                e1_row = row(e1_ref)
                for q in range(groups):
                    rows = slice(q * BF16_ROWS, (q + 1) * BF16_ROWS)
                    term = jnp.where(rho_ref[h, rows, :] < n_row, e2_ref[h, rows, :], 0.0) * e1_row
                    g[q] = term if g[q] is None else g[q] + term
            gates.extend(g)
        p_halves.append(_gelu_of_scaled(act).astype(BF16) * jnp.concatenate(gates, axis=0))
    acc_ref[...] += _dot(vt_ref[...], jnp.concatenate(p_halves, axis=0))

    @pl.when(j == pl.num_programs(1) - 1)
    def _():
        h = h1_ref[...] + m5_ref[0] * acc_ref[...].T
        o_ref[...] = h * lax.rsqrt(jnp.mean(h * h, axis=-1, keepdims=True) + EPS) * fw_ref[...]


def _peer(hm_t, u_b, vt_b, n_cnt, e1, rho, e2, h1, m5, final_w, seq):
    n = hm_t.shape[1]
    n_exp = u_b.shape[0]
    tn, te = PEER_TOK, PEER_EXP
    big = pl.BlockSpec((P_HEADS, P_KEYS, tn), lambda i, j: (0, 0, i))
    a_rows = pl.BlockSpec((P_HEADS, te // P_KEYS, tn), lambda i, j: (0, j, i))
    tok = pl.BlockSpec((tn, D), lambda i, j: (i, 0))
    return pl.pallas_call(
        _peer_kernel,
        grid=(n // tn, n_exp // te),
        in_specs=[pl.BlockSpec((D, tn), lambda i, j: (0, i)),
                  pl.BlockSpec((te, D), lambda i, j: (j, 0)),
                  pl.BlockSpec((D, te), lambda i, j: (0, j)),
                  a_rows, a_rows, big, big,
                  tok, pl.BlockSpec((1, 1, D), lambda i, j: (i // (seq // tn), 0, 0)),
                  pl.BlockSpec((1, D), lambda i, j: (0, 0))],
        out_specs=tok,
        out_shape=jax.ShapeDtypeStruct((n, D), F32),
        scratch_shapes=[pltpu.VMEM((D, tn), F32)],
        compiler_params=_cparams(("parallel", "arbitrary"), 52),
        name="peer",
    )(hm_t, u_b, vt_b, n_cnt, e1, rho, e2, h1, m5, final_w)


def _reorder_cols(w):
    gla_main, gla_lora = w[..., 0:3072], w[..., 3072:3104]
    rw_main, rw_lora = w[..., 3104:6176], w[..., 6176:6528]
    gates = w[..., 6528:8576]
    pad = jnp.zeros(w.shape[:-1] + (512 - 32 - 352,), w.dtype)
    return jnp.concatenate([rw_main, gla_main, gates, gla_lora, rw_lora, pad], axis=-1)


def _hi_lo(w):
    hi = w.astype(BF16)
    return jnp.stack([hi, (w - hi.astype(F32)).astype(BF16)])


def _to_chains(a):
    b, t, _ = a.shape
    return a.reshape(b, t, R_HEADS, R_N).transpose(1, 3, 0, 2).reshape(t, R_N, b * R_HEADS)


def _from_chains(a, b):
    t = a.shape[0]
    return a.reshape(t, R_N, b, R_HEADS).transpose(2, 0, 3, 1).reshape(b, t, D)


def kernel(x, c, ctx, c_ctx, norm1_w, w_mod, b_mod, w_in, gla_w_a2, gla_b_a, gla_norm_w, rwkv_mu, rwkv_w0, rwkv_w2, rwkv_a0, rwkv_a2, rwkv_g2, rwkv_k_k, rwkv_k_a, rwkv_r_k, rwkv_ln_w, rwkv_ln_b, w_out, norm2_w, peer_w_q, peer_sub_keys, peer_u, peer_v, final_norm_w):
    b, seq, _ = x.shape
    assert w_in.shape[0] == 1 and ctx.shape[1] == N_CTX and seq % ROWS == 0
    row = lambda v: v.reshape(1, -1)

    c16 = jnp.zeros((16, D), F32).at[:b].set(c).at[b].set(c_ctx)
    m = _mod(c16, w_mod[0], row(b_mod[0])).reshape(16, N_MOD, D)
    m_lat, m_ctx = m[:b], m[b]
    per_b = lambda i: m_lat[:, i].reshape(b, 1, D)
    shift1 = jnp.stack([jnp.broadcast_to(m_ctx[0], (b, D)), m_lat[:, 0]], axis=1).reshape(b, 2, 1, D)
    scale1 = jnp.stack([jnp.broadcast_to(m_ctx[1], (b, D)), m_lat[:, 1]], axis=1).reshape(b, 2, 1, D)

    h_all = jnp.concatenate([ctx, x], axis=1)
    z, z_lora = _inproj(h_all, row(norm1_w[0]), scale1, shift1, _reorder_cols(w_in[0]).astype(BF16))

    mu = rwkv_mu[0]
    mu_lora = jnp.concatenate([jnp.zeros((32,), F32), mu[3072:], jnp.zeros((128,), F32)]).reshape(1, 512)
    e = jnp.repeat(jnp.eye(R_HEADS, dtype=BF16), R_N, axis=0)
    w2f = _hi_lo(jnp.zeros((128, D), F32).at[L_WF:L_WF + R_W_LORA].set(rwkv_w2[0, 0]))
    w2b = _hi_lo(jnp.zeros((256, D), F32).at[L_WB:L_WB + R_W_LORA].set(rwkv_w2[0, 1]))
    a2 = _hi_lo(jnp.zeros((128, D), F32).at[L_A - 128:L_A - 128 + R_A_LORA].set(rwkv_a2[0]))
    g2 = _hi_lo(jnp.zeros((256, D), F32).at[L_G - 128:L_G - 128 + R_G_LORA].set(rwkv_g2[0]))
    wa = jnp.zeros((2, 128, G_HEADS * G_DK), F32)
    wa = wa.at[0, 0:G_LORA].set(gla_w_a2[0, 0]).at[1, G_LORA:2 * G_LORA].set(gla_w_a2[0, 1])
    wa = jnp.stack([_hi_lo(wa[0]), _hi_lo(wa[1])])
    r, k2, vr, kk, kka, d_f, d_b, bonus, gate_r = _rprep(
        z, z_lora, row(mu[:3072]), mu_lora, rwkv_w0[0].reshape(2, 1, D), w2f, w2b, row(rwkv_a0[0]), a2,
        g2, row(rwkv_k_k[0]), row(rwkv_k_a[0]), row(rwkv_r_k[0]), e, e.T)

    o_r = _rwkv(_to_chains(r), _to_chains(k2), _to_chains(vr), _to_chains(kk), _to_chains(kka),
                _to_chains(d_f), _to_chains(d_b))
    or_f, or_b = _from_chains(o_r[0], b), _from_chains(o_r[1], b)

    og_f, og_b = _gla(z, z_lora, wa, gla_b_a[0].reshape(2, 1, -1))

    h1, hm_b = _mix(og_f, og_b, z, or_f, or_b, bonus, gate_r, x, row(gla_norm_w[0]), row(rwkv_ln_w[0]),
                        row(rwkv_ln_b[0]), e, e.T, w_out[0].astype(BF16), per_b(2), row(norm2_w[0]),
                        per_b(3), per_b(4))

    n_tok = b * seq
    hm_t = hm_b.reshape(n_tok, D).T
    n_cnt, e1, rho, e2 = _route(hm_t, peer_w_q[0].T.astype(BF16),
                                peer_sub_keys[0].reshape(2 * P_HEADS, P_KEYS, -1).astype(BF16))
    assert seq % PEER_TOK == 0
    out = _peer(hm_t, (peer_u[0] * RSQRT2).astype(BF16), peer_v[0].T.astype(BF16), n_cnt, e1, rho, e2,
                h1.reshape(n_tok, D), per_b(5), row(final_norm_w), seq)
    return out.reshape(b, seq, D)
```

```python
import jax
import jax.numpy as jnp
from jax import lax
from jax.experimental import pallas as pl
from jax.experimental.pallas import tpu as pltpu

F32 = jnp.float32
BF16 = jnp.bfloat16

D = 1024
GRID_COLS = 64
N_CTX = 256
EPS = 1e-6
N_MOD = 6

G_HEADS = 4
G_DV = D // G_HEADS
G_DK = G_DV // 2
G_LORA = 16
G_GATE_NORM = 16.0
G_CHUNK = 64

R_N = 64
R_HEADS = D // R_N
R_LN_EPS = 64e-5
R_W_LORA = 64
R_A_LORA = 64
R_G_LORA = 160

P_HEADS = 8
P_KEYS = 128
P_TOPK = 16

C_RWKV = 0
C_GLA = 3072
C_GATE = 6144
C_LORA = 8192
LORA_W = 512
N_PROJ = C_LORA + LORA_W
L_WF, L_WB, L_A, L_G = 32, 96, 160, 224

ROWS = 256
V7X_VMEM_BYTES = 64 * 1024 * 1024


def _cparams(sem, vmem_mb):
    return pltpu.CompilerParams(dimension_semantics=sem,
                                vmem_limit_bytes=min(vmem_mb * 1024 * 1024, V7X_VMEM_BYTES - (8 << 20)))


def _sigmoid(x):
    return 1.0 / (1.0 + jnp.exp(-x))


def _softplus(x):
    return jnp.maximum(x, 0.0) + jnp.log(1.0 + jnp.exp(-jnp.abs(x)))


def _dot(a, b):
    return jnp.dot(a, b, preferred_element_type=F32)


def _dot_nt(a, b):
    return lax.dot_general(a, b, (((1,), (1,)), ((), ())), preferred_element_type=F32)


def _split2(x):
    hi = x.astype(BF16)
    return hi, (x - hi.astype(F32)).astype(BF16)


def _dot_x2(x, w):
    hi, lo = _split2(x)
    return _dot(hi, w) + _dot(lo, w)


def _dot_2x(w, x):
    hi, lo = _split2(x)
    return _dot(w, hi) + _dot(w, lo)


def _dot_x3(x, w_ref):
    hi, lo = _split2(x)
    return _dot(hi, w_ref[0]) + (_dot(lo, w_ref[0]) + _dot(hi, w_ref[1]))


def _mod_kernel(c_ref, w_ref, b_ref, o_ref):
    c = c_ref[...]
    s = c * _sigmoid(c)
    o_ref[...] = _dot(s.astype(BF16), w_ref[...].astype(BF16)) + b_ref[...]


def _mod(c16, w_mod, b_mod):
    n = w_mod.shape[1]
    tn = 768
    return pl.pallas_call(
        _mod_kernel,
        grid=(n // tn,),
        in_specs=[pl.BlockSpec((16, D), lambda j: (0, 0)),
                  pl.BlockSpec((D, tn), lambda j: (0, j)),
                  pl.BlockSpec((1, tn), lambda j: (0, j))],
        out_specs=pl.BlockSpec((16, tn), lambda j: (0, j)),
        out_shape=jax.ShapeDtypeStruct((16, n), F32),
        compiler_params=_cparams(("parallel",), 24),
        name="mod",
    )(c16, w_mod, b_mod)


INPROJ_ROWS = 2304


def _inproj_kernel(h_ref, nw_ref, sc_ref, sh_ref, w_ref, o_ref, ol_ref, a_scr):
    n = pl.program_id(2)

    @pl.when(n == 0)
    def _():
        x = h_ref[0]
        y = x * lax.rsqrt(jnp.mean(x * x, axis=-1, keepdims=True) + EPS) * nw_ref[...]
        t = pl.program_id(1) * INPROJ_ROWS + lax.broadcasted_iota(jnp.int32, (INPROJ_ROWS, 1), 0)
        is_ctx = t < N_CTX
        sc = jnp.where(is_ctx, sc_ref[0, 0], sc_ref[0, 1])
        sh = jnp.where(is_ctx, sh_ref[0, 0], sh_ref[0, 1])
        a_scr[...] = (y * (1.0 + sc) + sh).astype(BF16)

    acc = _dot(a_scr[...], w_ref[...])
    n_main = C_LORA // LORA_W

    @pl.when(n < n_main)
    def _():
        o_ref[0] = acc.astype(BF16)

    @pl.when(n == n_main)
    def _():
        ol_ref[0] = acc


def _inproj(h_all, norm_w, scale, shift, w_p):
    b, t, _ = h_all.shape
    tn = LORA_W
    n_main = C_LORA // tn
    mod = pl.BlockSpec((1, 2, 1, D), lambda i, j, n: (i, 0, 0, 0))
    return pl.pallas_call(
        _inproj_kernel,
        grid=(b, t // INPROJ_ROWS, N_PROJ // tn),
        in_specs=[pl.BlockSpec((1, INPROJ_ROWS, D), lambda i, j, n: (i, j, 0)),
                  pl.BlockSpec((1, D), lambda i, j, n: (0, 0)),
                  mod, mod,
                  pl.BlockSpec((D, tn), lambda i, j, n: (0, n))],
        out_specs=[pl.BlockSpec((1, INPROJ_ROWS, tn), lambda i, j, n: (i, j, jnp.minimum(n, n_main - 1))),
                   pl.BlockSpec((1, INPROJ_ROWS, tn), lambda i, j, n: (i, j, 0))],
        out_shape=[jax.ShapeDtypeStruct((b, t, C_LORA), BF16), jax.ShapeDtypeStruct((b, t, tn), F32)],
        scratch_shapes=[pltpu.VMEM((INPROJ_ROWS, D), BF16)],
        compiler_params=_cparams(("parallel", "parallel", "arbitrary"), 52),
        name="inproj",
    )(h_all, norm_w, scale, shift, w_p)


def _shift_mix(zc, zp, zn, mu, is_ctx):
    rows, w = zc.shape
    row = lax.broadcasted_iota(jnp.int32, (rows, w), 0)
    lane = lax.broadcasted_iota(jnp.int32, (rows, w), 1)
    pmask = jnp.where(is_ctx, 1, 3)
    rmask = jnp.where(is_ctx, rows - 1, GRID_COLS - 1)
    cls = lane & pmask
    rr = row & rmask
    prev = pltpu.roll(zc, 1, 0)
    nxt = pltpu.roll(zc, rows - 1, 0)
    up = jnp.concatenate([zp, zc[:rows - GRID_COLS]], axis=0)
    down = jnp.concatenate([zc[GRID_COLS:], zn], axis=0)
    zs = jnp.where(cls == 0, jnp.where(rr != 0, prev, 0.0),
                   jnp.where(cls == 1, jnp.where(rr != rmask, nxt, 0.0),
                             jnp.where(cls == 2, up, down)))
    return zc + (zs - zc) * mu


def _rprep_kernel(zc_ref, zp_ref, zn_ref, lc_ref, lp_ref, ln_ref,
                  mu_ref, mul_ref, w0_ref, w2f_ref, w2b_ref, a0_ref, a2_ref, g2_ref,
                  kk_w_ref, ka_w_ref, rk_w_ref, e_ref, et_ref,
                  r_o, k_o, v_o, kk_o, kka_o, df_o, db_o, bonus_o, gate_o):
    tb = pl.program_id(1)
    ntb = pl.num_programs(1)
    is_ctx = tb == 0
    has_up = (tb > 1).astype(F32)
    has_down = (tb < ntb - 1).astype(F32)

    zl = _shift_mix(lc_ref[0], lp_ref[0] * has_up, ln_ref[0] * has_down, mul_ref[...], is_ctx)
    def decay(window, w2p_ref, d):
        w = -_softplus(-(w0_ref[d] + _dot_x3(jnp.tanh(window), w2p_ref))) - 0.5
        return -jnp.exp(w)

    df_o[0] = decay(zl[:, 0:128], w2f_ref, 0).astype(BF16)
    db_o[0] = decay(zl[:, 0:256], w2b_ref, 1).astype(BF16)
    a = _sigmoid(a0_ref[...] + _dot_x3(zl[:, 128:256], a2_ref))
    gate_o[0] = _dot_x3(_sigmoid(zl[:, 128:384]), g2_ref).astype(BF16)

    def piece(i):
        sl = slice(i * D, (i + 1) * D)
        return _shift_mix(zc_ref[0, :, sl].astype(F32), zp_ref[0, :, sl].astype(F32) * has_up,
                          zn_ref[0, :, sl].astype(F32) * has_down, mu_ref[:, sl], is_ctx)

    r = piece(0)
    kr = piece(1)
    vr = piece(2)
    e = e_ref[...]
    et = et_ref[...]
    kkraw = kr * kk_w_ref[...]
    nrm = jnp.sqrt(_dot_x2(kkraw * kkraw, e))
    inv = 1.0 / jnp.maximum(nrm, 1e-12)
    kk = kkraw * _dot_x2(inv, et)
    k2 = kr * (1.0 + (a - 1.0) * ka_w_ref[...])
    rk = _dot_x2(r * k2 * rk_w_ref[...], e)
    r_o[0] = r.astype(BF16)
    k_o[0] = k2.astype(BF16)
    v_o[0] = vr.astype(BF16)
    kk_o[0] = kk.astype(BF16)
    kka_o[0] = (kk * a).astype(BF16)
    bonus_o[0] = (_dot_x2(rk, et) * vr).astype(BF16)


def _rprep(z, zl, mu_main, mu_lora, w0, w2f, w2b, a0, a2, g2, k_k, k_a, r_k, e, et):
    b, t, _ = z.shape
    ntb = t // ROWS
    q = ROWS // GRID_COLS
    last64 = t // GRID_COLS - 1
    wm = 3 * D
    cm = C_RWKV // wm
    cl = 0
    cur = lambda c: (lambda i, j: (i, j, c))
    prv = lambda c: (lambda i, j: (i, jnp.maximum(j * q - 1, 0), c))
    nxt = lambda c: (lambda i, j: (i, jnp.minimum(j * q + q, last64), c))
    full = lambda *s: pl.BlockSpec(s, lambda i, j: (0,) * len(s))
    out_spec = pl.BlockSpec((1, ROWS, D), lambda i, j: (i, j, 0))
    return pl.pallas_call(
        _rprep_kernel,
        grid=(b, ntb),
        in_specs=[pl.BlockSpec((1, ROWS, wm), cur(cm)),
                  pl.BlockSpec((1, GRID_COLS, wm), prv(cm)),
                  pl.BlockSpec((1, GRID_COLS, wm), nxt(cm)),
                  pl.BlockSpec((1, ROWS, 512), cur(cl)),
                  pl.BlockSpec((1, GRID_COLS, 512), prv(cl)),
                  pl.BlockSpec((1, GRID_COLS, 512), nxt(cl)),
                  full(1, wm), full(1, 512), full(2, 1, D), full(2, 128, D), full(2, 256, D),
                  full(1, D), full(2, 128, D), full(2, 256, D),
                  full(1, D), full(1, D), full(1, D), full(D, R_HEADS), full(R_HEADS, D)],
        out_specs=[out_spec] * 9,
        out_shape=[jax.ShapeDtypeStruct((b, t, D), BF16)] * 9,
        compiler_params=_cparams(("parallel", "parallel"), 52),
        name="rprep",
    )(z, z, z, zl, zl, zl, mu_main, mu_lora, w0, w2f, w2b, a0, a2, g2, k_k, k_a, r_k, e, et)


R_STEPS = 64


def _rwkv_kernel(r_ref, k_ref, v_ref, kk_ref, kka_ref, lwf_ref, lwb_ref, o_ref, s_ref, rows_ref):
    dirn = pl.program_id(0)
    tix = lambda i: jnp.where(dirn == 0, i, R_STEPS - 1 - i)

    @pl.when(pl.program_id(1) == 0)
    def _():
        s_ref[...] = jnp.zeros_like(s_ref)

    def weighted_sum(rows):
        acc = s_ref[0] * rows_ref[rows, 0:1, :]
        for k in range(1, R_N):
            acc = acc + s_ref[k] * rows_ref[rows, k:k + 1, :]
        return acc

    rows_ref[0] = kk_ref[tix(0)].astype(F32)
    sa0 = weighted_sum(0)

    def run(with_out):
        def step(i, carry):
            sa, lcum = carry
            t = tix(i)
            t_next = tix(jnp.minimum(i + 1, R_STEPS - 1))
            lcum = lcum + jnp.where(dirn == 0, lwf_ref[t], lwb_ref[t]).astype(F32)
            e_pos = jnp.exp(lcum)
            e_neg = jnp.exp(-lcum)
            rows_ref[0] = kk_ref[t_next].astype(F32) * e_pos
            rows_ref[1] = kka_ref[t].astype(F32) * e_neg
            rows_ref[2] = k_ref[t].astype(F32) * e_neg
            if with_out:
                rows_ref[3] = r_ref[t].astype(F32) * e_pos
            vv = v_ref[t].astype(F32)
            out = None
            sa_next = None
            for k in range(R_N):
                sk = s_ref[k] - sa * rows_ref[1, k:k + 1, :] + vv * rows_ref[2, k:k + 1, :]
                s_ref[k] = sk
                s_term = sk * rows_ref[0, k:k + 1, :]
                sa_next = s_term if sa_next is None else sa_next + s_term
                if with_out:
                    o_term = sk * rows_ref[3, k:k + 1, :]
                    out = o_term if out is None else out + o_term
            o_ref[0, t] = out.astype(BF16) if with_out else jnp.zeros(o_ref.shape[2:], BF16)
            return sa_next, lcum

        return lax.fori_loop(0, R_STEPS, step, (sa0, jnp.zeros((R_N, s_ref.shape[2]), F32)))[1]

    lcum = lax.cond(pl.program_id(1) < N_CTX // R_STEPS, lambda: run(False), lambda: run(True))
    rows_ref[0] = jnp.exp(lcum)
    for k in range(R_N):
        s_ref[k] = s_ref[k] * rows_ref[0, k:k + 1, :]


def _rwkv_order(dirn, nb, n_ctx_blk, n_blk):
    back = jnp.where(nb < n_ctx_blk, n_ctx_blk - 1 - nb, n_blk + n_ctx_blk - 1 - nb)
    return jnp.where(dirn == 0, nb, back)


def _rwkv(r_t, k_t, v_t, kk_t, kka_t, lwf_t, lwb_t):
    t, _, nch = r_t.shape
    n_blk = t // R_STEPS
    n_ctx_blk = N_CTX // R_STEPS
    blk = lambda d, n: (_rwkv_order(d, n, n_ctx_blk, n_blk), 0, 0)
    blk4 = lambda d, n: (d, _rwkv_order(d, n, n_ctx_blk, n_blk), 0, 0)
    only = lambda own: (lambda d, n: (jnp.where(d == own, _rwkv_order(d, n, n_ctx_blk, n_blk), 0), 0, 0))
    spec = pl.BlockSpec((R_STEPS, R_N, nch), blk)
    spec4 = pl.BlockSpec((1, R_STEPS, R_N, nch), blk4)
    return pl.pallas_call(
        _rwkv_kernel,
        grid=(2, n_blk),
        in_specs=[spec, spec, spec, spec, spec,
                  pl.BlockSpec((R_STEPS, R_N, nch), only(0)), pl.BlockSpec((R_STEPS, R_N, nch), only(1))],
        out_specs=spec4,
        out_shape=jax.ShapeDtypeStruct((2, t, R_N, nch), BF16),
        scratch_shapes=[pltpu.VMEM((R_N, R_N, nch), F32), pltpu.VMEM((4, R_N, nch), F32)],
        compiler_params=_cparams(("arbitrary", "arbitrary"), 48),
        name="rwkv",
    )(r_t, k_t, v_t, kk_t, kka_t, lwf_t, lwb_t)


def _log_sigmoid(x):
    return jnp.minimum(x, 0.0) - jnp.log(1.0 + jnp.exp(-jnp.abs(x)))


def _gla_dir(q, k, v, alo, wa, ba, st_ref, reverse, need_out):
    c = q.shape[0]
    q = q.astype(F32)
    k = k.astype(F32)
    g = _log_sigmoid(_dot_x3(alo, wa) + ba) * (1.0 / G_GATE_NORM)
    row = lax.broadcasted_iota(jnp.int32, (c, c), 0)
    col = lax.broadcasted_iota(jnp.int32, (c, c), 1)
    tri = (row <= col) if reverse else (row >= col)
    bcum = _dot_2x(tri.astype(BF16), g)
    blast = bcum[0:1] if reverse else bcum[c - 1:c]
    pad_rows = lambda a: jnp.concatenate([a, jnp.zeros_like(a)], axis=0)
    kl = pad_rows((k * jnp.exp(blast - bcum)).astype(BF16))
    eb = jnp.exp(blast)
    vt = pad_rows(v.astype(F32)).T.astype(BF16)
    if need_out:
        qd = (q * (G_DK ** -0.5) * jnp.exp(bcum)).astype(BF16)
        kd = pad_rows((k * jnp.exp(-bcum)).astype(BF16))
        row2 = lax.broadcasted_iota(jnp.int32, (c, 2 * c), 0)
        col2 = lax.broadcasted_iota(jnp.int32, (c, 2 * c), 1)
        tri2 = (row2 <= col2) if reverse else (row2 >= col2)
    outs = []
    for h in range(G_HEADS):
        sk = slice(h * G_DK, (h + 1) * G_DK)
        st = st_ref[h]
        vth = vt[h * G_DV:(h + 1) * G_DV, :]
        if need_out:
            att = jnp.where(tri2, _dot_nt(qd[:, sk], kd[:, sk]), 0.0).astype(BF16)
            outs.append(_dot_nt(jnp.concatenate([qd[:, sk], att], axis=1),
                                jnp.concatenate([st.astype(BF16), vth], axis=1)))
        st_ref[h] = st * eb[:, sk] + _dot(vth, kl[:, sk])
    return jnp.concatenate(outs, axis=-1).astype(BF16) if need_out else jnp.zeros(v.shape, BF16)


GLA_BATCH = 4


def _gla_kernel(qf, kf, vf, lf, qb, kb, vb, lb, wa_ref, ba_ref, of_ref, ob_ref, sf_ref, sb_ref):
    @pl.when(pl.program_id(1) == 0)
    def _():
        sf_ref[...] = jnp.zeros_like(sf_ref)
        sb_ref[...] = jnp.zeros_like(sb_ref)

    def run(need_out):
        for s in range(GLA_BATCH):
            of_ref[s] = _gla_dir(qf[s], kf[s], vf[s], lf[s], wa_ref.at[0], ba_ref[0], sf_ref.at[s], False, need_out)
            ob_ref[s] = _gla_dir(qb[s], kb[s], vb[s], lb[s], wa_ref.at[1], ba_ref[1], sb_ref.at[s], True, need_out)

    is_ctx = pl.program_id(1) < N_CTX // G_CHUNK
    pl.when(is_ctx)(lambda: run(False))
    pl.when(jnp.logical_not(is_ctx))(lambda: run(True))


def _gla(z, zl, w_a2, b_a):
    b, t, _ = z.shape
    n_blk = t // G_CHUNK
    n_ctx_blk = N_CTX // G_CHUNK
    fwd = lambda n: n
    bwd = lambda n: _rwkv_order(1, n, n_ctx_blk, n_blk)
    wk = G_HEADS * G_DK
    cq, ck, cv = C_GLA // wk, (C_GLA + wk) // wk, (C_GLA + 2 * wk) // D
    cl = 0

    nb = GLA_BATCH
    assert b % nb == 0

    def specs(order):
        return [pl.BlockSpec((nb, G_CHUNK, wk), lambda i, n: (i, order(n), cq)),
                pl.BlockSpec((nb, G_CHUNK, wk), lambda i, n: (i, order(n), ck)),
                pl.BlockSpec((nb, G_CHUNK, D), lambda i, n: (i, order(n), cv)),
                pl.BlockSpec((nb, G_CHUNK, 128), lambda i, n: (i, order(n), cl))]

    out_sd = jax.ShapeDtypeStruct((b, t, D), BF16)
    state = pltpu.VMEM((nb, G_HEADS, G_DV, G_DK), F32)
    return pl.pallas_call(
        _gla_kernel,
        grid=(b // nb, n_blk),
        in_specs=specs(fwd) + specs(bwd) + [
            pl.BlockSpec((2, 2, 128, wk), lambda i, n: (0, 0, 0, 0)),
            pl.BlockSpec((2, 1, wk), lambda i, n: (0, 0, 0))],
        out_specs=[pl.BlockSpec((nb, G_CHUNK, D), lambda i, n: (i, fwd(n), 0)),
                   pl.BlockSpec((nb, G_CHUNK, D), lambda i, n: (i, bwd(n), 0))],
        out_shape=[out_sd, out_sd],
        scratch_shapes=[state, state],
        compiler_params=_cparams(("parallel", "arbitrary"), 24),
        name="gla",
    )(z, z, z, zl, z, z, z, zl, w_a2, b_a)


def _mix_kernel(ogf_ref, ogb_ref, gout_ref, orf_ref, orb_ref, bonus_ref, gr_ref, gg_ref, grw_ref, x_ref,
                gnw_ref, lnw_ref, lnb_ref, e_ref, et_ref, wout_ref, m2_ref, n2w_ref, m3_ref, m4_ref,
                h1_o, hmb_o):
    og = ogf_ref[0].astype(F32) + ogb_ref[0].astype(F32)
    parts = []
    for h in range(G_HEADS):
        seg = og[:, h * G_DV:(h + 1) * G_DV]
        parts.append(seg * lax.rsqrt(jnp.mean(seg * seg, axis=-1, keepdims=True) + EPS))
    gout = gout_ref[0].astype(F32)
    y_gla = jnp.concatenate(parts, axis=-1) * gnw_ref[...] * (gout * _sigmoid(gout))

    e = e_ref[...]
    et = et_ref[...]
    o_r = orf_ref[0].astype(F32) + orb_ref[0].astype(F32)
    mu = _dot_x2(_dot_x2(o_r, e) * (1.0 / R_N), et)
    dlt = o_r - mu
    var = _dot_x2(dlt * dlt, e) * (1.0 / R_N)
    o_r = (dlt * _dot_x2(lax.rsqrt(var + R_LN_EPS), et) * lnw_ref[...] + lnb_ref[...]
           + bonus_ref[0].astype(F32))
    y_rwkv = o_r * gr_ref[0].astype(F32)

    y = _sigmoid(gg_ref[0].astype(F32)) * y_gla + _sigmoid(grw_ref[0].astype(F32)) * y_rwkv
    h1 = x_ref[0] + m2_ref[0] * _dot(y.astype(BF16), wout_ref[...])
    h1_o[0] = h1
    hn = h1 * lax.rsqrt(jnp.mean(h1 * h1, axis=-1, keepdims=True) + EPS) * n2w_ref[...]
    hmb_o[0] = (hn * (1.0 + m4_ref[0]) + m3_ref[0]).astype(BF16)


def _mix(og_f, og_b, z, or_f, or_b, bonus, gate_r, x, gnw, lnw, lnb, e, et, w_out, m2, n2w, m3, m4):
    b, tl, _ = x.shape
    off = N_CTX // ROWS
    seq = lambda c: pl.BlockSpec((1, ROWS, D), lambda i, j: (i, j + off, c))
    full = lambda *s: pl.BlockSpec(s, lambda i, j: (0,) * len(s))
    per_b = pl.BlockSpec((1, 1, D), lambda i, j: (i, 0, 0))
    lat = pl.BlockSpec((1, ROWS, D), lambda i, j: (i, j, 0))
    return pl.pallas_call(
        _mix_kernel,
        grid=(b, tl // ROWS),
        in_specs=[seq(0), seq(0), seq((C_GLA + 2048) // D), seq(0), seq(0), seq(0), seq(0),
                  seq(C_GATE // D), seq(C_GATE // D + 1), lat,
                  full(1, D), full(1, D), full(1, D), full(D, R_HEADS), full(R_HEADS, D), full(D, D),
                  per_b, full(1, D), per_b, per_b],
        out_specs=[lat, lat],
        out_shape=[jax.ShapeDtypeStruct((b, tl, D), F32), jax.ShapeDtypeStruct((b, tl, D), BF16)],
        compiler_params=_cparams(("parallel", "parallel"), 52),
        name="mix",
    )(og_f, og_b, z, or_f, or_b, bonus, gate_r, z, z, x, gnw, lnw, lnb, e, et, w_out, m2, n2w, m3, m4)


ROUTE_TOK = 1024


N_CAND = 80


def _route_kernel(x_ref, wq_ref, keys_ref, n_o, e1_o, rho_o, e2_o,
                  q_scr, s_scr, t1_scr, t2_scr, cand_scr, rho_scr, cnt_scr):
    q_scr[...] = _dot(wq_ref[...], x_ref[...]).astype(BF16)
    ninf = -jnp.inf
    half = P_TOPK // 2
    tn = x_ref.shape[1]
    top_row = jnp.full((1, tn), jnp.inf, F32)

    def next_below(vals, bound):
        below = vals < bound
        return below, jnp.max(jnp.where(below, vals, ninf), axis=0, keepdims=True)

    def head(h, carry):
        q1 = q_scr[pl.ds(pl.multiple_of(h * 2 * P_KEYS, P_KEYS), P_KEYS), :]
        q2 = q_scr[pl.ds(pl.multiple_of(h * 2 * P_KEYS + P_KEYS, P_KEYS), P_KEYS), :]
        s_scr[0] = _dot(keys_ref[2 * h], q1)
        s_scr[1] = _dot(keys_ref[2 * h + 1], q2)
        rho_scr[...] = jnp.full(rho_scr.shape, -1.0, F32)

        def extract(i, bounds):
            m1, m2 = bounds
            _, m1 = next_below(s_scr[0], m1)
            below2, m2 = next_below(s_scr[1], m2)
            rho_scr[...] += jnp.where(below2, 1.0, 0.0)
            t1_scr[pl.ds(i, 1), :] = m1
            t2_scr[pl.ds(i, 1), :] = m2
            return m1, m2

        _, last2 = lax.fori_loop(0, P_TOPK, extract, (top_row, top_row))
        s1 = s_scr[0]
        s2 = s_scr[1]
        t1 = t1_scr[...]
        t2 = t2_scr[...]
        cand_scr[0:P_TOPK] = t1[0:1] + t2
        for i in range(1, half):
            cand_scr[P_TOPK + half * (i - 1):P_TOPK + half * i] = t1[i:i + 1] + t2[0:half]
        cand_scr[N_CAND - half:N_CAND] = t1[half:P_TOPK] + t2[0:1]
        cmax = t1[0:1] + t2[0:1]

        def pick(i, st):
            m, z = st
            _, m = next_below(cand_scr[...], m)
            return m, z + jnp.exp(m - cmax)

        tau, zsum = lax.fori_loop(0, P_TOPK, pick, (top_row, jnp.zeros_like(cmax)))
        cnt_scr[...] = jnp.zeros_like(cnt_scr)

        def count(j, c):
            cnt_scr[...] += jnp.where(s_scr[0] + t2_scr[pl.ds(j, 1), :] >= tau, 1.0, 0.0)
            return c

        lax.fori_loop(0, P_TOPK, count, 0)
        sel1 = s1 >= t1[P_TOPK - 1:P_TOPK]
        sel2 = s2 >= last2
        rho = rho_scr[...] + jnp.where(s2 < last2, 1.0, 0.0)
        n_o[h] = jnp.where(sel1, cnt_scr[...], 0.0)
        e1_o[h] = jnp.where(sel1, jnp.exp(s1 - t1[0:1]), 0.0)
        rho_o[h] = rho.astype(BF16)
        e2_o[h] = jnp.where(sel2, jnp.exp(s2 - t2[0:1]) * (RSQRT2 / zsum), 0.0).astype(BF16)
        return carry

    lax.fori_loop(0, P_HEADS, head, 0)


def _route(hm_t, wq_t, keys):
    n = hm_t.shape[1]
    tn = ROUTE_TOK
    big = pl.BlockSpec((P_HEADS, P_KEYS, tn), lambda i: (0, 0, i))
    sd = lambda dt: jax.ShapeDtypeStruct((P_HEADS, P_KEYS, n), dt)
    return pl.pallas_call(
        _route_kernel,
        grid=(n // tn,),
        in_specs=[pl.BlockSpec((D, tn), lambda i: (0, i)),
                  pl.BlockSpec((2 * P_HEADS * P_KEYS, D), lambda i: (0, 0)),
                  pl.BlockSpec((2 * P_HEADS, P_KEYS, P_KEYS), lambda i: (0, 0, 0))],
        out_specs=[big, big, big, big],
        out_shape=[sd(F32), sd(F32), sd(BF16), sd(BF16)],
        scratch_shapes=[pltpu.VMEM((2 * P_HEADS * P_KEYS, tn), BF16), pltpu.VMEM((2, P_KEYS, tn), F32),
                        pltpu.VMEM((P_TOPK, tn), F32), pltpu.VMEM((P_TOPK, tn), F32),
                        pltpu.VMEM((N_CAND, tn), F32),
                        pltpu.VMEM((P_KEYS, tn), F32), pltpu.VMEM((P_KEYS, tn), F32)],
        compiler_params=_cparams(("parallel",), 48),
        name="route",
    )(hm_t, wq_t, keys)


PEER_TOK = 512
PEER_EXP = 2048
PEER_SUB = 512


RSQRT2 = 2.0 ** -0.5


def _gelu_of_scaled(y):
    return y * (1.0 + lax.erf(y))


BF16_ROWS = 16


def _peer_kernel(x_ref, u_ref, vt_ref, n_ref, e1_ref, rho_ref, e2_ref, h1_ref, m5_ref, fw_ref, o_ref, acc_ref):
    j = pl.program_id(1)

    @pl.when(j == 0)
    def _():
        acc_ref[...] = jnp.zeros_like(acc_ref)

    tn = x_ref.shape[1]
    x = x_ref[...]
    groups = P_KEYS // BF16_ROWS
    a_per_half = PEER_SUB // P_KEYS
    p_halves = []
    for half in range(PEER_EXP // PEER_SUB):
        rows_h = PEER_SUB
        act = _dot(u_ref[half * rows_h:(half + 1) * rows_h, :], x)
        gates = []
        for al in range(half * a_per_half, (half + 1) * a_per_half):
            g = [None] * groups
            for h in range(P_HEADS):
                row = lambda ref: jnp.broadcast_to(ref[h, al:al + 1, :], (BF16_ROWS, tn)).astype(BF16)
                n_row = row(n_ref)
                e1_row = row(e1_ref)
                for q in range(groups):
                    rows = slice(q * BF16_ROWS, (q + 1) * BF16_ROWS)
                    term = jnp.where(rho_ref[h, rows, :] < n_row, e2_ref[h, rows, :], 0.0) * e1_row
                    g[q] = term if g[q] is None else g[q] + term
            gates.extend(g)
        p_halves.append(_gelu_of_scaled(act).astype(BF16) * jnp.concatenate(gates, axis=0))
    acc_ref[...] += _dot(vt_ref[...], jnp.concatenate(p_halves, axis=0))

    @pl.when(j == pl.num_programs(1) - 1)
    def _():
        h = h1_ref[...] + m5_ref[0] * acc_ref[...].T
        o_ref[...] = h * lax.rsqrt(jnp.mean(h * h, axis=-1, keepdims=True) + EPS) * fw_ref[...]


def _peer(hm_t, u_b, vt_b, n_cnt, e1, rho, e2, h1, m5, final_w, seq):
    n = hm_t.shape[1]
    n_exp = u_b.shape[0]
    tn, te = PEER_TOK, PEER_EXP
    big = pl.BlockSpec((P_HEADS, P_KEYS, tn), lambda i, j: (0, 0, i))
    a_rows = pl.BlockSpec((P_HEADS, te // P_KEYS, tn), lambda i, j: (0, j, i))
    tok = pl.BlockSpec((tn, D), lambda i, j: (i, 0))
    return pl.pallas_call(
        _peer_kernel,
        grid=(n // tn, n_exp // te),
        in_specs=[pl.BlockSpec((D, tn), lambda i, j: (0, i)),
                  pl.BlockSpec((te, D), lambda i, j: (j, 0)),
                  pl.BlockSpec((D, te), lambda i, j: (0, j)),
                  a_rows, a_rows, big, big,
                  tok, pl.BlockSpec((1, 1, D), lambda i, j: (i // (seq // tn), 0, 0)),
                  pl.BlockSpec((1, D), lambda i, j: (0, 0))],
        out_specs=tok,
        out_shape=jax.ShapeDtypeStruct((n, D), F32),
        scratch_shapes=[pltpu.VMEM((D, tn), F32)],
        compiler_params=_cparams(("parallel", "arbitrary"), 52),
        name="peer",
    )(hm_t, u_b, vt_b, n_cnt, e1, rho, e2, h1, m5, final_w)


def _reorder_cols(w):
    gla_main, gla_lora = w[..., 0:3072], w[..., 3072:3104]
    rw_main, rw_lora = w[..., 3104:6176], w[..., 6176:6528]
    gates = w[..., 6528:8576]
    pad = jnp.zeros(w.shape[:-1] + (512 - 32 - 352,), w.dtype)
    return jnp.concatenate([rw_main, gla_main, gates, gla_lora, rw_lora, pad], axis=-1)


def _hi_lo(w):
    hi = w.astype(BF16)
    return jnp.stack([hi, (w - hi.astype(F32)).astype(BF16)])


def _to_chains(a):
    b, t, _ = a.shape
    return a.reshape(b, t, R_HEADS, R_N).transpose(1, 3, 0, 2).reshape(t, R_N, b * R_HEADS)


def _from_chains(a, b):
    t = a.shape[0]
    return a.reshape(t, R_N, b, R_HEADS).transpose(2, 0, 3, 1).reshape(b, t, D)


def kernel(x, c, ctx, c_ctx, norm1_w, w_mod, b_mod, w_in, gla_w_a2, gla_b_a, gla_norm_w, rwkv_mu, rwkv_w0, rwkv_w2, rwkv_a0, rwkv_a2, rwkv_g2, rwkv_k_k, rwkv_k_a, rwkv_r_k, rwkv_ln_w, rwkv_ln_b, w_out, norm2_w, peer_w_q, peer_sub_keys, peer_u, peer_v, final_norm_w):
    b, seq, _ = x.shape
    assert w_in.shape[0] == 1 and ctx.shape[1] == N_CTX and seq % ROWS == 0
    row = lambda v: v.reshape(1, -1)

    c16 = jnp.zeros((16, D), F32).at[:b].set(c).at[b].set(c_ctx)
    m = _mod(c16, w_mod[0], row(b_mod[0])).reshape(16, N_MOD, D)
    m_lat, m_ctx = m[:b], m[b]
    per_b = lambda i: m_lat[:, i].reshape(b, 1, D)
    shift1 = jnp.stack([jnp.broadcast_to(m_ctx[0], (b, D)), m_lat[:, 0]], axis=1).reshape(b, 2, 1, D)
    scale1 = jnp.stack([jnp.broadcast_to(m_ctx[1], (b, D)), m_lat[:, 1]], axis=1).reshape(b, 2, 1, D)

    h_all = jnp.concatenate([ctx, x], axis=1)
    z, z_lora = _inproj(h_all, row(norm1_w[0]), scale1, shift1, _reorder_cols(w_in[0]).astype(BF16))

    mu = rwkv_mu[0]
    mu_lora = jnp.concatenate([jnp.zeros((32,), F32), mu[3072:], jnp.zeros((128,), F32)]).reshape(1, 512)
    e = jnp.repeat(jnp.eye(R_HEADS, dtype=BF16), R_N, axis=0)
    w2f = _hi_lo(jnp.zeros((128, D), F32).at[L_WF:L_WF + R_W_LORA].set(rwkv_w2[0, 0]))
    w2b = _hi_lo(jnp.zeros((256, D), F32).at[L_WB:L_WB + R_W_LORA].set(rwkv_w2[0, 1]))
    a2 = _hi_lo(jnp.zeros((128, D), F32).at[L_A - 128:L_A - 128 + R_A_LORA].set(rwkv_a2[0]))
    g2 = _hi_lo(jnp.zeros((256, D), F32).at[L_G - 128:L_G - 128 + R_G_LORA].set(rwkv_g2[0]))
    wa = jnp.zeros((2, 128, G_HEADS * G_DK), F32)
    wa = wa.at[0, 0:G_LORA].set(gla_w_a2[0, 0]).at[1, G_LORA:2 * G_LORA].set(gla_w_a2[0, 1])
    wa = jnp.stack([_hi_lo(wa[0]), _hi_lo(wa[1])])
    r, k2, vr, kk, kka, d_f, d_b, bonus, gate_r = _rprep(
        z, z_lora, row(mu[:3072]), mu_lora, rwkv_w0[0].reshape(2, 1, D), w2f, w2b, row(rwkv_a0[0]), a2,
        g2, row(rwkv_k_k[0]), row(rwkv_k_a[0]), row(rwkv_r_k[0]), e, e.T)

    o_r = _rwkv(_to_chains(r), _to_chains(k2), _to_chains(vr), _to_chains(kk), _to_chains(kka),
                _to_chains(d_f), _to_chains(d_b))
    or_f, or_b = _from_chains(o_r[0], b), _from_chains(o_r[1], b)

    og_f, og_b = _gla(z, z_lora, wa, gla_b_a[0].reshape(2, 1, -1))

    h1, hm_b = _mix(og_f, og_b, z, or_f, or_b, bonus, gate_r, x, row(gla_norm_w[0]), row(rwkv_ln_w[0]),
                        row(rwkv_ln_b[0]), e, e.T, w_out[0].astype(BF16), per_b(2), row(norm2_w[0]),
                        per_b(3), per_b(4))

    n_tok = b * seq
    hm_t = hm_b.reshape(n_tok, D).T
    n_cnt, e1, rho, e2 = _route(hm_t, peer_w_q[0].T.astype(BF16),
                                peer_sub_keys[0].reshape(2 * P_HEADS, P_KEYS, -1).astype(BF16))
    assert seq % PEER_TOK == 0
    out = _peer(hm_t, (peer_u[0] * RSQRT2).astype(BF16), peer_v[0].T.astype(BF16), n_cnt, e1, rho, e2,
                h1.reshape(n_tok, D), per_b(5), row(final_norm_w), seq)
    return out.reshape(b, seq, D)
```
